```python
import math
import jax, jax.numpy as jnp
from jax import lax
import numpy as np

D_MODEL = 1024
BATCH = 4
SEQ = 8192
DEPTH = 2
DEC_BATCH = 32
DEC_SEQ = 1
PAST_LEN = 16384
PAGE_SIZE = 128

HEAD_DIM = 64
ROPE_DIM = HEAD_DIM // 4
ROPE_THETA = 500000.0
N_HEADS_DSA = 4
N_HEADS_NSA = 4
N_HEADS_MOBA = 4
N_HEADS_MEM = 4
BRANCH_WIDTH = N_HEADS_DSA * HEAD_DIM
N_IDX_HEADS = 4
IDX_DIM = HEAD_DIM
DSA_TOPK = 256
NSA_CMP_LEN = 32
NSA_CMP_STRIDE = 16
NSA_SEL_BLOCK = 64
NSA_SEL_TOPN = 16
NSA_WINDOW = 512
MOBA_BLOCK = 256
MOBA_TOPK = 3
N_MEM = 256
N_BRANCH = 4
D_FF = 2816
CONV_WIDTH = 3
Q_BLOCK = 128
RMS_EPS = 1e-6

IN_NAMES = ('q_a', 'k_a', 'v_a', 'iq', 'ik', 'iw',
            'q_b', 'kc', 'vc', 'ks', 'vs', 'kw', 'vw', 'g_b',
            'q_c', 'k_c', 'v_c', 'q_m', 'gate')
IN_SIZES = (N_HEADS_DSA * HEAD_DIM, HEAD_DIM, HEAD_DIM, N_IDX_HEADS * IDX_DIM, IDX_DIM, N_IDX_HEADS,
            N_HEADS_NSA * HEAD_DIM, HEAD_DIM, HEAD_DIM, HEAD_DIM, HEAD_DIM, HEAD_DIM, HEAD_DIM, 3 * N_HEADS_NSA,
            N_HEADS_MOBA * HEAD_DIM, N_HEADS_MOBA * HEAD_DIM, N_HEADS_MOBA * HEAD_DIM,
            N_HEADS_MEM * HEAD_DIM, N_BRANCH * D_MODEL)
D_IN = sum(IN_SIZES)

kernel_name = 'hybrid_dsa_nsa_moba_decoder_step'


def rms_norm(x, g):
    xf = x.astype(jnp.float32)
    y = xf * lax.rsqrt(jnp.mean(xf * xf, axis=-1, keepdims=True) + RMS_EPS)
    return (y * g.astype(jnp.float32)).astype(x.dtype)


def partial_rope(x, pos):
    half = ROPE_DIM // 2
    inv_freq = ROPE_THETA ** (-jnp.arange(half, dtype=jnp.float32) / half)
    ang = pos.astype(jnp.float32)[:, None] * inv_freq[None, :]
    cos, sin = jnp.cos(ang)[:, None, :], jnp.sin(ang)[:, None, :]
    xr = x[..., :ROPE_DIM].astype(jnp.float32)
    x1, x2 = xr[..., :half], xr[..., half:]
    rot = jnp.concatenate([x1 * cos - x2 * sin, x2 * cos + x1 * sin], axis=-1)
    return jnp.concatenate([rot.astype(x.dtype), x[..., ROPE_DIM:]], axis=-1)


def masked_softmax(s, mask):
    s = jnp.where(mask, s.astype(jnp.float32), -jnp.inf)
    m = jnp.max(s, axis=-1, keepdims=True)
    m = jnp.where(jnp.isfinite(m), m, 0.0)
    e = jnp.where(mask, jnp.exp(s - m), 0.0)
    return e / jnp.maximum(jnp.sum(e, axis=-1, keepdims=True), 1e-30)


def gather_rows(src, idx):
    return jax.vmap(lambda s, i: s[i])(src, idx)


def page_gather(pool, page_table):
    g = pool[page_table]
    return g.reshape(page_table.shape[0], page_table.shape[1] * pool.shape[1], *pool.shape[2:])


def map_query_blocks(fn, pos0, *arrays):
    B, T = arrays[0].shape[:2]
    qb = math.gcd(T, Q_BLOCK)
    nb = T // qb
    blocks = [jnp.moveaxis(a.reshape(B, nb, qb, *a.shape[2:]), 1, 0) for a in arrays]
    starts = pos0 + qb * jnp.arange(nb, dtype=jnp.int32)
    out = lax.map(lambda args: fn(*args), (starts, *blocks))
    return jnp.moveaxis(out, 0, 1).reshape(B, T, *out.shape[3:])


def dsa_attention(q, k, v, iq, ik, iw, pos0):
    L = k.shape[1]
    n_top = min(DSA_TOPK, L // 4)
    kpos = jnp.arange(L, dtype=jnp.int32)
    scale = HEAD_DIM ** -0.5
    ikf = ik.astype(jnp.float32)

    def block(p0, qb, iqb, iwb):
        pos = p0 + jnp.arange(qb.shape[1], dtype=jnp.int32)
        logits = jnp.einsum('bthd,bsd->bths', iqb.astype(jnp.float32), ikf)
        score = jnp.einsum('bth,bths->bts', iwb.astype(jnp.float32), jax.nn.relu(logits))
        score = jnp.where((kpos[None, :] <= pos[:, None])[None], score, -jnp.inf)
        top_val, top_idx = lax.top_k(score, n_top)
        kg = gather_rows(k, top_idx)
        vg = gather_rows(v, top_idx)
        s = jnp.einsum('bthd,btkd->bthk', qb, kg) * scale
        prob = masked_softmax(s, jnp.isfinite(top_val)[:, :, None, :])
        return jnp.einsum('bthk,btkd->bthd', prob.astype(vg.dtype), vg)

    return map_query_blocks(block, pos0, q, iq, iw)


def nsa_compress(rows, w1, w2, pe):
    B, L, dh = rows.shape
    nc = (L - NSA_CMP_LEN) // NSA_CMP_STRIDE + 1
    idx = NSA_CMP_STRIDE * jnp.arange(nc)[:, None] + jnp.arange(NSA_CMP_LEN)[None, :]
    blocks = rows[:, idx] + pe
    h = jax.nn.gelu(blocks.reshape(B, nc, NSA_CMP_LEN * dh) @ w1)
    return h @ w2


def nsa_attention(q, rows, win, win_start, gates, w_cmp1, w_cmp2, pe_cmp, pos0):
    B, L = rows.shape[:2]
    scale = HEAD_DIM ** -0.5
    k_cmp = nsa_compress(rows[:, :, 0], w_cmp1[0], w_cmp2[0], pe_cmp[0])
    v_cmp = nsa_compress(rows[:, :, 1], w_cmp1[1], w_cmp2[1], pe_cmp[1])
    nc = k_cmp.shape[1]
    cmp_start = NSA_CMP_STRIDE * jnp.arange(nc, dtype=jnp.int32)
    cmp_end = cmp_start + NSA_CMP_LEN - 1
    ns = -(-L // NSA_SEL_BLOCK)
    n_sel = min(NSA_SEL_TOPN, ns)
    pad = ns * NSA_SEL_BLOCK - L

    def to_blocks(a):
        return jnp.pad(a, ((0, 0), (0, pad), (0, 0))).reshape(B, ns, NSA_SEL_BLOCK, HEAD_DIM)

    k_blk, v_blk = to_blocks(rows[:, :, 2]), to_blocks(rows[:, :, 3])
    blk = jnp.arange(ns, dtype=jnp.int32)
    cover = ((cmp_start[:, None] < (blk[None, :] + 1) * NSA_SEL_BLOCK)
             & (cmp_end[:, None] >= blk[None, :] * NSA_SEL_BLOCK)).astype(jnp.float32)
    k_win = jnp.pad(win[:, :, 0], ((0, 0), (NSA_WINDOW, 0), (0, 0)))
    v_win = jnp.pad(win[:, :, 1], ((0, 0), (NSA_WINDOW, 0), (0, 0)))

    def block(p0, qb, gb):
        nq = qb.shape[1]
        pos = p0 + jnp.arange(nq, dtype=jnp.int32)
        s_c = jnp.einsum('bthd,bnd->bthn', qb, k_cmp) * scale
        p_c = masked_softmax(s_c, (cmp_end[None, :] <= pos[:, None])[None, :, None, :])
        o_c = jnp.einsum('bthn,bnd->bthd', p_c.astype(v_cmp.dtype), v_cmp)
        imp = jnp.einsum('bthn,ns->bts', p_c, cover)
        cur = (pos // NSA_SEL_BLOCK)[:, None]
        visible = blk[None, :] <= cur
        forced = (blk[None, :] == 0) | (blk[None, :] == cur) | (blk[None, :] == cur - 1)
        imp = jnp.where(forced, jnp.inf, jnp.where(visible, imp, -jnp.inf))
        top_val, top_idx = lax.top_k(imp, n_sel)
        kg = gather_rows(k_blk, top_idx).reshape(B, nq, n_sel * NSA_SEL_BLOCK, HEAD_DIM)
        vg = gather_rows(v_blk, top_idx).reshape(B, nq, n_sel * NSA_SEL_BLOCK, HEAD_DIM)
        gpos = (top_idx[..., None] * NSA_SEL_BLOCK
                + jnp.arange(NSA_SEL_BLOCK, dtype=jnp.int32)).reshape(B, nq, n_sel * NSA_SEL_BLOCK)
        m_s = (gpos <= pos[None, :, None]) & jnp.repeat(top_val > -jnp.inf, NSA_SEL_BLOCK, axis=-1)
        s_s = jnp.einsum('bthd,btkd->bthk', qb, kg) * scale
        p_s = masked_softmax(s_s, m_s[:, :, None, :])
        o_s = jnp.einsum('bthk,btkd->bthd', p_s.astype(vg.dtype), vg)
        nw = NSA_WINDOW + nq - 1
        j0 = p0 - win_start + 1
        kw = lax.dynamic_slice_in_dim(k_win, j0, nw, axis=1)
        vw = lax.dynamic_slice_in_dim(v_win, j0, nw, axis=1)
        wpos = p0 - NSA_WINDOW + 1 + jnp.arange(nw, dtype=jnp.int32)
        m_w = ((wpos[None, :] >= win_start) & (wpos[None, :] <= pos[:, None])
               & (pos[:, None] - wpos[None, :] < NSA_WINDOW))
        s_w = jnp.einsum('bthd,bkd->bthk', qb, kw) * scale
        p_w = masked_softmax(s_w, m_w[None, :, None, :])
        o_w = jnp.einsum('bthk,bkd->bthd', p_w.astype(vw.dtype), vw)
        gb = gb.astype(o_c.dtype)
        return gb[..., 0:1] * o_c + gb[..., 1:2] * o_s + gb[..., 2:3] * o_w

    return map_query_blocks(block, pos0, q, gates)


def moba_attention(q, k, v, pos0):
    B, L, H, _ = k.shape
    scale = HEAD_DIM ** -0.5
    nb = -(-L // MOBA_BLOCK)
    pad = nb * MOBA_BLOCK - L
    kb = jnp.pad(k, ((0, 0), (0, pad), (0, 0), (0, 0))).reshape(B, nb, MOBA_BLOCK, H, HEAD_DIM)
    vb = jnp.pad(v, ((0, 0), (0, pad), (0, 0), (0, 0))).reshape(B, nb, MOBA_BLOCK, H, HEAD_DIM)
    k_mean = jnp.mean(kb.astype(jnp.float32), axis=2)
    kb_h, vb_h = jnp.moveaxis(kb, 3, 1), jnp.moveaxis(vb, 3, 1)
    n_sel = min(MOBA_TOPK, nb)
    blk = jnp.arange(nb, dtype=jnp.int32)
    gather_bh = jax.vmap(jax.vmap(lambda s, i: s[i]))

    def block(p0, qb):
        nq = qb.shape[1]
        pos = p0 + jnp.arange(nq, dtype=jnp.int32)
        gate = jnp.einsum('bthd,bnhd->bthn', qb.astype(jnp.float32), k_mean)
        past = blk[None, :] < (pos // MOBA_BLOCK)[:, None]
        gate = jnp.where(past[None, :, None, :], gate, -jnp.inf)
        top_val, top_idx = lax.top_k(gate, n_sel)
        idx_h = jnp.moveaxis(top_idx, 2, 1)
        kg = gather_bh(kb_h, idx_h)
        vg = gather_bh(vb_h, idx_h)
        s_sel = (jnp.einsum('bthd,bhtnkd->bthnk', qb, kg) * scale).reshape(B, nq, H, n_sel * MOBA_BLOCK)
        m_sel = jnp.repeat(jnp.isfinite(top_val), MOBA_BLOCK, axis=-1)
        c0 = p0 // MOBA_BLOCK
        k_own = lax.dynamic_index_in_dim(kb, c0, axis=1, keepdims=False)
        v_own = lax.dynamic_index_in_dim(vb, c0, axis=1, keepdims=False)
        own_pos = c0 * MOBA_BLOCK + jnp.arange(MOBA_BLOCK, dtype=jnp.int32)
        s_own = jnp.einsum('bthd,bkhd->bthk', qb, k_own) * scale
        m_own = jnp.broadcast_to((own_pos[None, :] <= pos[:, None])[None, :, None, :], s_own.shape)
        prob = masked_softmax(jnp.concatenate([s_sel, s_own], axis=-1),
                              jnp.concatenate([m_sel, m_own], axis=-1))
        p_sel = prob[..., :n_sel * MOBA_BLOCK].reshape(B, nq, H, n_sel, MOBA_BLOCK).astype(vg.dtype)
        p_own = prob[..., n_sel * MOBA_BLOCK:].astype(v_own.dtype)
        return (jnp.einsum('bthnk,bhtnkd->bthd', p_sel, vg)
                + jnp.einsum('bthk,bkhd->bthd', p_own, v_own))

    return map_query_blocks(block, pos0, q)


def memory_kv(mem, g_ln, w_kv, g_k):
    B, M, _ = mem.shape
    kv = (rms_norm(mem, g_ln) @ w_kv).reshape(B, M, 2, N_HEADS_MEM, HEAD_DIM)
    return jnp.stack([rms_norm(kv[:, :, 0], g_k), kv[:, :, 1]], axis=2)


def memory_attention(q, mk, mv):
    s = jnp.einsum('bthd,bmhd->bthm', q, mk) * (HEAD_DIM ** -0.5)
    prob = jax.nn.softmax(s.astype(jnp.float32), axis=-1)
    return jnp.einsum('bthm,bmhd->bthd', prob.astype(mv.dtype), mv)


def mixer_projections(xn, pos, p):
    B, T, _ = xn.shape
    z = xn @ p['w_in']
    c = {}
    off = 0
    for name, size in zip(IN_NAMES, IN_SIZES):
        c[name] = z[..., off:off + size]
        off += size

    def heads(a, h):
        return a.reshape(B, T, h, -1)

    def qk(a, h, g):
        return partial_rope(rms_norm(heads(a, h), g), pos)

    def key_row(a, g):
        return partial_rope(rms_norm(a[:, :, None, :], g), pos)[:, :, 0]

    g_dsa, g_nsa, g_moba, g_mem = p['g_dsa'], p['g_nsa'], p['g_moba'], p['g_mem']
    return {
        'q_a': qk(c['q_a'], N_HEADS_DSA, g_dsa[0]),
        'iq': partial_rope(heads(c['iq'], N_IDX_HEADS), pos),
        'iw': c['iw'],
        'q_b': qk(c['q_b'], N_HEADS_NSA, g_nsa[0]),
        'g_b': jax.nn.sigmoid(heads(c['g_b'], N_HEADS_NSA)),
        'q_c': qk(c['q_c'], N_HEADS_MOBA, g_moba[0]),
        'q_m': rms_norm(heads(c['q_m'], N_HEADS_MEM), g_mem[0]),
        'gate': jax.nn.sigmoid(heads(c['gate'], N_BRANCH)),
        'dsa': jnp.stack([key_row(c['k_a'], g_dsa[1]), c['v_a'],
                          partial_rope(c['ik'][:, :, None, :], pos)[:, :, 0]], axis=2),
        'nsa': jnp.stack([key_row(c['kc'], g_nsa[1]), c['vc'],
                          key_row(c['ks'], g_nsa[2]), c['vs']], axis=2),
        'win': jnp.stack([key_row(c['kw'], g_nsa[3]), c['vw']], axis=2),
        'moba': jnp.stack([qk(c['k_c'], N_HEADS_MOBA, g_moba[1]), heads(c['v_c'], N_HEADS_MOBA)], axis=2),
    }


def conv_ffn(x, hist, g, w_up, conv_w, conv_b, w_down):
    T = x.shape[1]
    u = rms_norm(x, g) @ w_up
    ext = jnp.concatenate([hist.astype(u.dtype), u], axis=1)
    c = conv_b
    for j in range(CONV_WIDTH):
        c = c + conv_w[j] * ext[:, j:j + T]
    a, b = jnp.split(c, 2, axis=-1)
    return (jax.nn.silu(a) * b) @ w_down, ext[:, T:]


def decoder_layer(x, pos0, mem_kv, conv_hist, past, p):
    B, T, _ = x.shape
    pos = pos0 + jnp.arange(T, dtype=jnp.int32)
    pr = mixer_projections(rms_norm(x, p['ln'][0]), pos, p)
    if past is None:
        dsa, nsa, moba, win, win_start = pr['dsa'], pr['nsa'], pr['moba'], pr['win'], 0
    else:
        dsa_past, nsa_past, moba_past, win_buf = past
        dsa = jnp.concatenate([dsa_past, pr['dsa']], axis=1)
        nsa = jnp.concatenate([nsa_past, pr['nsa']], axis=1)
        moba = jnp.concatenate([moba_past, pr['moba']], axis=1)
        win = jnp.concatenate([win_buf, pr['win']], axis=1)
        win_start = pos0 - win_buf.shape[1]
    o_a = dsa_attention(pr['q_a'], dsa[:, :, 0], dsa[:, :, 1], pr['iq'], dsa[:, :, 2], pr['iw'], pos0)
    o_b = nsa_attention(pr['q_b'], nsa, win, win_start, pr['g_b'], p['w_cmp1'], p['w_cmp2'], p['pe_cmp'], pos0)
    o_c = moba_attention(pr['q_c'], moba[:, :, 0], moba[:, :, 1], pos0)
    o_m = memory_attention(pr['q_m'], mem_kv[:, :, 0], mem_kv[:, :, 1])
    h = jnp.zeros_like(x)
    for b, o in enumerate((o_a, o_b, o_c, o_m)):
        h = h + pr['gate'][:, :, b] * (o.reshape(B, T, BRANCH_WIDTH) @ p['w_branch'][b])
    x = x + h @ p['w_out']
    f, conv_state = conv_ffn(x, conv_hist, p['ln'][1], p['w_up'], p['conv_w'], p['conv_b'], p['w_down'])
    return x + f, pr['dsa'], pr['nsa'], pr['moba'], win, conv_state


def setup_inputs(seed: int = 0) -> dict:
    key = jax.random.key(seed)
    keys = iter(jax.random.split(key, 40))
    f32 = jnp.float32
    n_pages = PAST_LEN // PAGE_SIZE
    n_pool = (DEC_BATCH * n_pages * 5) // 4
    win_keep = min(NSA_WINDOW, PAST_LEN)

    def nrm(shape, scale=1.0):
        return scale * jax.random.normal(next(keys), shape, f32)

    def gain(shape):
        return 1.0 + 0.02 * jax.random.normal(next(keys), shape, f32)

    page_table = jax.random.permutation(next(keys), n_pool)[:DEC_BATCH * n_pages]
    page_table = page_table.reshape(DEC_BATCH, n_pages).astype(jnp.int32)
    return {
        'x_prompt': nrm((BATCH, SEQ, D_MODEL)),
        'x_sample': nrm((DEC_BATCH, DEC_SEQ, D_MODEL)),
        'cache_dsa': nrm((DEPTH, n_pool, PAGE_SIZE, 3, HEAD_DIM)),
        'cache_nsa': nrm((DEPTH, n_pool, PAGE_SIZE, 4, HEAD_DIM)),
        'cache_moba': nrm((DEPTH, n_pool, PAGE_SIZE, 2, N_HEADS_MOBA, HEAD_DIM)),
        'state_nsa_win': nrm((DEPTH, DEC_BATCH, win_keep, 2, HEAD_DIM)),
        'cache_mem': nrm((DEPTH, DEC_BATCH, N_MEM, 2, N_HEADS_MEM, HEAD_DIM)),
        'state_ffn_conv': nrm((DEPTH, DEC_BATCH, CONV_WIDTH - 1, 2 * D_FF)),
        'page_table': page_table,
        'mem_prompt': nrm((BATCH, N_MEM, D_MODEL)),
        'ln': gain((DEPTH, 3, D_MODEL)),
        'w_in': nrm((DEPTH, D_MODEL, D_IN), D_MODEL ** -0.5),
        'g_dsa': gain((DEPTH, 2, HEAD_DIM)),
        'g_nsa': gain((DEPTH, 4, HEAD_DIM)),
        'g_moba': gain((DEPTH, 2, HEAD_DIM)),
        'g_mem': gain((DEPTH, 2, HEAD_DIM)),
        'w_mem_kv': nrm((DEPTH, D_MODEL, 2 * N_HEADS_MEM * HEAD_DIM), D_MODEL ** -0.5),
        'w_cmp1': nrm((DEPTH, 2, NSA_CMP_LEN * HEAD_DIM, HEAD_DIM), (NSA_CMP_LEN * HEAD_DIM) ** -0.5),
        'w_cmp2': nrm((DEPTH, 2, HEAD_DIM, HEAD_DIM), HEAD_DIM ** -0.5),
        'pe_cmp': nrm((DEPTH, 2, NSA_CMP_LEN, HEAD_DIM), 0.1),
        'w_branch': nrm((DEPTH, N_BRANCH, BRANCH_WIDTH, D_MODEL), BRANCH_WIDTH ** -0.5),
        'w_out': nrm((DEPTH, D_MODEL, D_MODEL), D_MODEL ** -0.5),
        'w_up': nrm((DEPTH, D_MODEL, 2 * D_FF), D_MODEL ** -0.5),
        'conv_w': nrm((DEPTH, CONV_WIDTH, 2 * D_FF), CONV_WIDTH ** -0.5),
        'conv_b': nrm((DEPTH, 2 * D_FF), 0.01),
        'w_down': nrm((DEPTH, D_FF, D_MODEL), D_FF ** -0.5),
    }


def reference(x_prompt, x_sample, cache_dsa, cache_nsa, cache_moba, state_nsa_win, cache_mem, state_ffn_conv,
              page_table, mem_prompt, ln, w_in, g_dsa, g_nsa, g_moba, g_mem, w_mem_kv, w_cmp1, w_cmp2, pe_cmp,
              w_branch, w_out, w_up, conv_w, conv_b, w_down):
    S = x_prompt.shape[1]
    win_keep_p = min(NSA_WINDOW, S)
    win_keep_s = state_nsa_win.shape[2]
    conv_hist_p = jnp.zeros((x_prompt.shape[0], CONV_WIDTH - 1, 2 * D_FF), x_prompt.dtype)
    xp, xs = x_prompt, x_sample
    dsa_p, nsa_p, moba_p, win_p, memkv_p, conv_p = [], [], [], [], [], []
    dsa_s, nsa_s, moba_s, win_s, conv_s = [], [], [], [], []
    for l in range(DEPTH):
        p = {'ln': ln[l], 'w_in': w_in[l], 'g_dsa': g_dsa[l], 'g_nsa': g_nsa[l], 'g_moba': g_moba[l],
             'g_mem': g_mem[l], 'w_cmp1': w_cmp1[l], 'w_cmp2': w_cmp2[l], 'pe_cmp': pe_cmp[l],
             'w_branch': w_branch[l], 'w_out': w_out[l], 'w_up': w_up[l], 'conv_w': conv_w[l],
             'conv_b': conv_b[l], 'w_down': w_down[l]}
        mkv = memory_kv(mem_prompt, ln[l, 2], w_mem_kv[l], g_mem[l, 1])
        xp, r_dsa, r_nsa, r_moba, r_win, r_conv = decoder_layer(xp, 0, mkv, conv_hist_p, None, p)
        dsa_p.append(r_dsa)
        nsa_p.append(r_nsa)
        moba_p.append(r_moba)
        win_p.append(r_win[:, S - win_keep_p:])
        memkv_p.append(mkv)
        conv_p.append(r_conv)
        past = (page_gather(cache_dsa[l], page_table), page_gather(cache_nsa[l], page_table),
                page_gather(cache_moba[l], page_table), state_nsa_win[l])
        xs, r_dsa, r_nsa, r_moba, r_win, r_conv = decoder_layer(xs, PAST_LEN, cache_mem[l], state_ffn_conv[l], past, p)
        dsa_s.append(r_dsa)
        nsa_s.append(r_nsa)
        moba_s.append(r_moba)
        win_s.append(r_win[:, r_win.shape[1] - win_keep_s:])
        conv_s.append(r_conv)
    return (xp, xs, jnp.stack(dsa_p), jnp.stack(dsa_s), jnp.stack(nsa_p), jnp.stack(nsa_s),
            jnp.stack(moba_p), jnp.stack(moba_s), jnp.stack(win_p), jnp.stack(win_s),
            jnp.stack(memkv_p), jnp.stack(conv_p), jnp.stack(conv_s))
```

```python
import functools
import math

import numpy as np
import jax
import jax.numpy as jnp
from jax import lax
from jax.experimental import pallas as pl
from jax.experimental.pallas import tpu as pltpu

HEAD_DIM = 64
ROPE_DIM = HEAD_DIM // 4
ROPE_THETA = 500000.0
N_HEADS = 4
DSA_TOPK = 256
NSA_CMP_LEN = 32
NSA_CMP_STRIDE = 16
NSA_SEL_BLOCK = 64
NSA_SEL_TOPN = 16
NSA_WINDOW = 512
MOBA_BLOCK = 256
MOBA_TOPK = 3
N_BRANCH = 4
CONV_WIDTH = 3
RMS_EPS = 1e-6

LANES = 128
SUBLANES = 8
VMEM_LIMIT = 56 * 1024 * 1024
DEC_ROWS = SUBLANES

F32 = jnp.float32
BF16 = jnp.bfloat16
NEG_INF = float("-inf")
SCALE = HEAD_DIM ** -0.5

KEY_NEG_INF = int(np.uint32(0xFF800000) ^ np.uint32(0x7FFFFFFF)) - 2 ** 32
KEY_POS_INF = 0x7F800000
INT_MIN = -2 ** 31

_NT = (((1,), (1,)), ((), ()))

_ORIG = dict(q_a=0, k_a=256, v_a=320, iq=384, ik=640, iw=704, q_b=708, kc=964, vc=1028, ks=1092, vs=1156,
             kw=1220, vw=1284, g_b=1348, q_c=1360, k_c=1616, v_c=1872, q_m=2128)
GATE_ORIG = 2384
_NEW = dict(q_a=0, iq=256, q_b=512, q_c=768, q_m=1024, k_c=1280, v_c=1536, k_a=1792, v_a=1856, ik=1920,
            iw=1984, kc=2048, vc=2112, ks=2176, vs=2240, kw=2304, vw=2368, g_b=2432)
_WIDTH = dict(q_a=256, iq=256, q_b=256, q_c=256, q_m=256, k_c=256, v_c=256, k_a=64, v_a=64, ik=64, iw=4,
              kc=64, vc=64, ks=64, vs=64, kw=64, vw=64, g_b=12)
N_HEADCOLS = 2560
_NORMED = ("q_a", "q_b", "q_c", "q_m", "k_c", "k_a", "kc", "ks", "kw")
_ROPED = ("q_a", "iq", "q_b", "q_c", "k_c", "k_a", "ik", "kc", "ks", "kw")
_SIGMOID = ("g_b",)

BLK_KV_A, BLK_IK, BLK_CMP, BLK_SEL, BLK_WIN, BLK_GB = 14, 15, 16, 17, 18, 19
BLK_KC, BLK_VC = 10, 12


def _head_layout():
    src = np.full((N_HEADCOLS,), -1, np.int64)
    masks = {k: np.zeros((N_HEADCOLS,), np.float32) for k in ("norm", "rope", "sig")}
    for name, new in _NEW.items():
        w = _WIDTH[name]
        src[new:new + w] = _ORIG[name] + np.arange(w)
        if name in _NORMED:
            masks["norm"][new:new + w] = 1
        if name in _ROPED:
            masks["rope"][new:new + w] = 1
        if name in _SIGMOID:
            masks["sig"][new:new + w] = 1
    return src, masks


_COL_SRC, _COL_MASKS = _head_layout()


def _chunk_any(mask):
    return tuple(bool(mask[c * LANES:(c + 1) * LANES].any()) for c in range(mask.shape[0] // LANES))


def _params(sem):
    return pltpu.CompilerParams(dimension_semantics=sem, vmem_limit_bytes=VMEM_LIMIT)


def _lane_iota(n=LANES):
    return lax.broadcasted_iota(jnp.int32, (1, n), 1)


def _split_bf16(x):
    hi = x.astype(BF16)
    lo = (x - hi.astype(F32)).astype(BF16)
    return hi, lo


def _dot(a, b):
    return jnp.dot(a, b, preferred_element_type=F32)


def _dot_nt(a, b):
    return lax.dot_general(a, b, _NT, preferred_element_type=F32)


def _proj_kernel(x_ref, g_ref, w_ref, cfg_ref, rope_ref, o_ref, *, norm_chunks, rope_chunks, sig_chunks):
    x = x_ref[...]
    xn = x * lax.rsqrt(jnp.mean(x * x, axis=-1, keepdims=True) + RMS_EPS) * g_ref[...]
    z = _dot(xn.astype(BF16), w_ref[...])
    rr = lax.broadcasted_iota(jnp.int32, (LANES, LANES), 0) // HEAD_DIM
    cc = lax.broadcasted_iota(jnp.int32, (LANES, LANES), 1) // HEAD_DIM
    seg = (rr == cc).astype(BF16)
    for c in range(z.shape[1] // LANES):
        sl = slice(c * LANES, (c + 1) * LANES)
        zc = z[:, sl]
        if norm_chunks[c]:
            hi, lo = _split_bf16(zc * zc)
            ss = _dot(hi, seg) + _dot(lo, seg)
            r = lax.rsqrt(ss * (1.0 / HEAD_DIM) + RMS_EPS)
            zc = jnp.where(cfg_ref[0:1, sl] > 0, zc * r * cfg_ref[1:2, sl], zc)
        if rope_chunks[c]:
            rot = (zc * rope_ref[:, 0:LANES]
                   + pltpu.roll(zc, LANES - ROPE_DIM // 2, 1) * rope_ref[:, LANES:2 * LANES]
                   + pltpu.roll(zc, ROPE_DIM // 2, 1) * rope_ref[:, 2 * LANES:3 * LANES])
            zc = jnp.where(cfg_ref[2:3, sl] > 0, rot, zc)
        if sig_chunks[c]:
            zc = jnp.where(cfg_ref[3:4, sl] > 0, jax.nn.sigmoid(zc), zc)
        o_ref[:, sl] = zc


def _project(x2d, g, w_bf16, cfg, rope, masks, *, tm, tn):
    m, d = x2d.shape
    n = w_bf16.shape[1]
    tn = min(tn, n)
    tm = min(tm, m)
    flags = {k: _chunk_any(v) for k, v in masks.items()}
    per_tile = tn // LANES
    for k, v in flags.items():
        assert all(v[t * per_tile:(t + 1) * per_tile] == v[:per_tile] for t in range(n // tn)), k
    rope_tiles = rope.shape[0] // tm
    kern = functools.partial(_proj_kernel, norm_chunks=flags["norm"][:per_tile],
                             rope_chunks=flags["rope"][:per_tile], sig_chunks=flags["sig"][:per_tile])
    return pl.pallas_call(
        kern,
        grid=(m // tm, n // tn),
        in_specs=[pl.BlockSpec((tm, d), lambda i, j: (i, 0)),
                  pl.BlockSpec((1, d), lambda i, j: (0, 0)),
                  pl.BlockSpec((d, tn), lambda i, j: (0, j)),
                  pl.BlockSpec((SUBLANES, tn), lambda i, j: (0, j)),
                  pl.BlockSpec((tm, 3 * LANES), lambda i, j: (i % rope_tiles, 0))],
        out_specs=pl.BlockSpec((tm, tn), lambda i, j: (i, j)),
        out_shape=jax.ShapeDtypeStruct((m, n), F32),
        compiler_params=_params(("parallel", "arbitrary")),
        name="project",
    )(x2d, g, w_bf16, cfg, rope)


def _rope_table(pos):
    half = ROPE_DIM // 2
    inv_freq = ROPE_THETA ** (-jnp.arange(half, dtype=F32) / half)
    ang = pos.astype(F32)[:, None] * inv_freq[None, :]
    cos, sin = jnp.cos(ang), jnp.sin(ang)
    t = pos.shape[0]
    ones = jnp.ones((t, HEAD_DIM - ROPE_DIM), F32)
    zeros = jnp.zeros((t, HEAD_DIM - ROPE_DIM), F32)
    zh = jnp.zeros((t, half), F32)
    c64 = jnp.concatenate([cos, cos, ones], axis=1)
    s1 = jnp.concatenate([-sin, zh, zeros], axis=1)
    s2 = jnp.concatenate([zh, sin, zeros], axis=1)
    return jnp.concatenate([c64, c64, s1, s1, s2, s2], axis=1)


def _head_weights(w_in_l, g_dsa, g_nsa, g_moba, g_mem):
    d = w_in_l.shape[0]
    pieces, at = [], 0
    for name, new in sorted(_NEW.items(), key=lambda kv: kv[1]):
        if new > at:
            pieces.append(jnp.zeros((d, new - at), w_in_l.dtype))
        pieces.append(w_in_l[:, _ORIG[name]:_ORIG[name] + _WIDTH[name]])
        at = new + _WIDTH[name]
    pieces.append(jnp.zeros((d, N_HEADCOLS - at), w_in_l.dtype))
    w = jnp.concatenate(pieces, axis=1).astype(BF16)
    gain = jnp.ones((N_HEADCOLS,), F32)
    for name, gvec in (("q_a", g_dsa[0]), ("k_a", g_dsa[1]), ("q_b", g_nsa[0]), ("kc", g_nsa[1]),
                       ("ks", g_nsa[2]), ("kw", g_nsa[3]), ("q_c", g_moba[0]), ("k_c", g_moba[1]),
                       ("q_m", g_mem[0])):
        reps = _WIDTH[name] // HEAD_DIM
        gain = lax.dynamic_update_slice(gain, jnp.tile(gvec.astype(F32), reps), (_NEW[name],))
    zero = jnp.zeros((N_HEADCOLS,), F32)
    cfg = jnp.stack([jnp.asarray(_COL_MASKS["norm"]), gain, jnp.asarray(_COL_MASKS["rope"]),
                     jnp.asarray(_COL_MASKS["sig"]), zero, zero, zero, zero])
    return w, cfg


def _gate_cfg(n):
    z = jnp.zeros((n,), F32)
    o = jnp.ones((n,), F32)
    return jnp.stack([z, o, z, o, z, z, z, z])


def _projections(x2d, pos, ln0, w_heads, cfg, w_gate, *, tm):
    d = x2d.shape[1]
    rope = _rope_table(pos)
    g = ln0.reshape(1, d)
    zh = _project(x2d, g, w_heads, cfg, rope, _COL_MASKS, tm=tm, tn=N_HEADCOLS)
    n_g = w_gate.shape[1]
    gmask = dict(norm=np.zeros((n_g,), np.float32), rope=np.zeros((n_g,), np.float32),
                 sig=np.ones((n_g,), np.float32))
    gate = _project(x2d, g, w_gate, _gate_cfg(n_g), rope, gmask, tm=tm, tn=1024)
    return zh, gate


def _stack_heads(x256):
    lane = _lane_iota()
    parts = []
    for c in range(2):
        ch = x256[:, c * LANES:(c + 1) * LANES]
        parts.append(jnp.where(lane < HEAD_DIM, ch, 0.0))
        parts.append(jnp.where(lane < HEAD_DIM, pltpu.roll(ch, HEAD_DIM, 1), 0.0))
    return jnp.concatenate(parts, axis=0)


def _unstack_heads(o_heads):
    lane = _lane_iota()
    chunks = []
    for c in range(2):
        chunks.append(jnp.where(lane < HEAD_DIM, pltpu.roll(o_heads[2 * c], HEAD_DIM, 1), o_heads[2 * c + 1]))
    return jnp.concatenate(chunks, axis=1)


def _pair_heads(q128):
    lane = _lane_iota()
    return jnp.concatenate([jnp.where(lane < HEAD_DIM, q128, 0.0), jnp.where(lane >= HEAD_DIM, q128, 0.0)], axis=0)


def _softmax_init(m_ref, l_ref, acc_ref):
    m_ref[...] = jnp.full(m_ref.shape, NEG_INF, F32)
    l_ref[...] = jnp.zeros(l_ref.shape, F32)
    acc_ref[...] = jnp.zeros(acc_ref.shape, F32)


def _softmax_step(rows, s, valid, kv_bf16, m_ref, l_ref, acc_ref):
    s = jnp.where(valid, s, NEG_INF)
    m_old = m_ref[rows, :]
    m_new = jnp.maximum(m_old, jnp.max(s, axis=-1, keepdims=True))
    m_safe = jnp.where(m_new == NEG_INF, 0.0, m_new)
    p = jnp.exp(s - m_safe)
    alpha = jnp.exp(m_old - m_safe)
    l_ref[rows, :] = alpha * l_ref[rows, :] + jnp.sum(p, axis=-1, keepdims=True)
    acc_ref[rows, :] = alpha * acc_ref[rows, :] + _dot(p.astype(BF16), kv_bf16)
    m_ref[rows, :] = m_new


def _softmax_result(rows, l_ref, acc_ref):
    return acc_ref[rows, :] / jnp.maximum(l_ref[rows, :], 1e-30)


def _masked_softmax(s, valid):
    s = jnp.where(valid, s, NEG_INF)
    m = jnp.max(s, axis=-1, keepdims=True)
    m = jnp.where(m == NEG_INF, 0.0, m)
    e = jnp.exp(s - m)
    return e / jnp.maximum(jnp.sum(e, axis=-1, keepdims=True), 1e-30)


def _chunk_loader(ref, kc, lanes=slice(None)):
    def load(c):
        rows = pl.ds(pl.multiple_of(c * kc, kc), kc)
        x = ref[0, rows, lanes] if len(ref.shape) == 3 else ref[rows, lanes]
        return x.astype(BF16)
    return load


def _softmax_scratch(rows):
    return [pltpu.VMEM((rows, 1), F32), pltpu.VMEM((rows, 1), F32), pltpu.VMEM((rows, LANES), F32)]


def _rank_rows(val_t, src_ref, n_rows):
    src_ref[...] = val_t
    j = lax.broadcasted_iota(jnp.int32, (val_t.shape[0], 1), 0)

    def body(i, rank):
        row = src_ref[pl.ds(i, 1), :]
        beats = (row > val_t) | ((row == val_t) & (i < j))
        return rank + jnp.where(beats, 1, 0)

    return lax.fori_loop(0, n_rows, body, jnp.zeros(val_t.shape, jnp.int32))


def _rank_cols(val, n_cols):
    j = _lane_iota(val.shape[1])

    def body(i, rank):
        col = jnp.sum(jnp.where(j == i, val, 0.0), axis=1, keepdims=True)
        beats = (col > val) | ((col == val) & (i < j))
        return rank + jnp.where(beats, 1, 0)

    return lax.fori_loop(0, n_cols, body, jnp.zeros(val.shape, jnp.int32))


def _select_top(val, rank_ref, n_valid, n_sel):
    if val.shape[0] >= LANES:
        val_t = val.T
        rank = _rank_rows(val_t, rank_ref, n_valid)
        return jnp.where((rank < n_sel) & (val_t > NEG_INF), 1.0, 0.0).T
    rank = _rank_cols(val, n_valid)
    return jnp.where((rank < n_sel) & (val > NEG_INF), 1.0, 0.0)


def _dsa_core(q, iq, iw, load_kv, load_ik, key_ref, m_ref, l_ref, acc_ref, *, tq, kc, n_top, p0, idx_bits):
    n_chunks = (p0 + tq - 1) // kc + 1
    pos = p0 + lax.broadcasted_iota(jnp.int32, (tq, 1), 0)
    iqs = _stack_heads(iq).astype(BF16)
    qs = (_stack_heads(q) * SCALE).astype(BF16)
    w_cols = [iw[:, HEAD_DIM + h:HEAD_DIM + h + 1] for h in range(N_HEADS)]

    def chunk_pos(c):
        return c * kc + _lane_iota(kc)

    def score_chunk(c, carry):
        lg = _dot_nt(iqs, load_ik(c))
        sc = w_cols[0] * jnp.maximum(lg[0:tq], 0.0)
        for h in range(1, N_HEADS):
            sc = sc + w_cols[h] * jnp.maximum(lg[h * tq:(h + 1) * tq], 0.0)
        sc = jnp.where(chunk_pos(c) <= pos, sc, NEG_INF)
        bits = pltpu.bitcast(sc, jnp.int32)
        key_ref[c] = bits ^ ((bits >> 31) & 0x7FFFFFFF)
        return carry

    lax.fori_loop(0, n_chunks, score_chunk, 0)

    def count(pred):
        def body(c, acc):
            hit = pred(key_ref[c], chunk_pos(c))
            for t in range(kc // LANES):
                acc = acc + jnp.where(hit[:, t * LANES:(t + 1) * LANES], 1, 0)
            return acc
        acc = lax.fori_loop(0, n_chunks, body, jnp.zeros((tq, LANES), jnp.int32))
        return jnp.sum(acc, axis=1, keepdims=True)

    def thr_bit(b, thr):
        cand = thr + jnp.left_shift(jnp.int32(1), 31 - b)
        cnt = count(lambda key, kpos: key >= cand)
        return jnp.where(cnt >= n_top, cand, thr)

    thr = lax.fori_loop(0, 32, thr_bit, jnp.full((tq, 1), INT_MIN, jnp.int32))
    need = n_top - count(lambda key, kpos: key > thr)

    def idx_bit(b, last):
        cand = last + jnp.left_shift(jnp.int32(1), idx_bits - 1 - b)
        cnt = count(lambda key, kpos: (key == thr) & (kpos < cand))
        return jnp.where(cnt < need, cand, last)

    last = lax.fori_loop(0, idx_bits, idx_bit, jnp.zeros((tq, 1), jnp.int32))

    _softmax_init(m_ref, l_ref, acc_ref)

    def attend(c, carry):
        kvc = load_kv(c)
        s = _dot_nt(qs, kvc)
        key = key_ref[c]
        sel = (key > thr) | ((key == thr) & (chunk_pos(c) <= last))
        sel = sel & (key > KEY_NEG_INF) & (key < KEY_POS_INF)
        for h in range(N_HEADS):
            _softmax_step(pl.ds(h * tq, tq), s[h * tq:(h + 1) * tq], sel, kvc, m_ref, l_ref, acc_ref)
        return carry

    lax.fori_loop(0, n_chunks, attend, 0)
    return _unstack_heads([_softmax_result(pl.ds(h * tq, tq), l_ref, acc_ref) for h in range(N_HEADS)])


def _dsa_kernel(q_ref, iq_ref, iw_ref, kv_ref, ik_ref, o_ref, key_ref, m_ref, l_ref, acc_ref,
                *, tq, kc, n_top, idx_bits):
    o_ref[0] = _dsa_core(q_ref[0], iq_ref[0], iw_ref[0], _chunk_loader(kv_ref, kc), _chunk_loader(ik_ref, kc),
                         key_ref, m_ref, l_ref, acc_ref, tq=tq, kc=kc, n_top=n_top,
                         p0=pl.program_id(1) * tq, idx_bits=idx_bits)


def _dsa_decode_kernel(pt_ref, q_ref, iq_ref, iw_ref, page_ref, new_ref, o_ref, buf_ref, key_ref, m_ref, l_ref,
                       acc_ref, *, tq, kc, n_top, idx_bits, n_pages, page):
    p = pl.program_id(1)

    @pl.when(p < n_pages)
    def _():
        rows = page_ref[0, 0]
        pad = jnp.zeros((page, 2 * LANES - rows.shape[1]), F32)
        buf_ref[pl.ds(pl.multiple_of(p * page, page), page), :] = jnp.concatenate([rows, pad], axis=1).astype(BF16)

    @pl.when(p == n_pages)
    def _():
        buf_ref[pl.ds(n_pages * page, kc), :] = new_ref[0].astype(BF16)
        o_ref[0] = _dsa_core(q_ref[0], iq_ref[0], iw_ref[0], _chunk_loader(buf_ref, kc, slice(0, LANES)),
                             _chunk_loader(buf_ref, kc, slice(LANES, 2 * LANES)), key_ref, m_ref, l_ref, acc_ref,
                             tq=tq, kc=kc, n_top=n_top, p0=n_pages * page, idx_bits=idx_bits)


def _dsa_scratch(n_kc, tq, kc):
    return [pltpu.VMEM((n_kc, tq, kc), jnp.int32)] + _softmax_scratch(N_HEADS * tq)


def _page_map(layer, last, lane_blk=0):
    return lambda bi, p, pt: (layer, pt[bi, jnp.minimum(p, last)], 0, lane_blk)


def _dsa_decode_call(zq, cache, layer, page_table, new_rows, *, kc=512):
    b, tq, _ = zq.shape
    n_pages = page_table.shape[1]
    page = cache.shape[2]
    l_rows = n_pages * page + kc
    n_kc = l_rows // kc
    kern = functools.partial(_dsa_decode_kernel, tq=tq, kc=kc, n_top=min(DSA_TOPK, (n_pages * page + 1) // 4),
                             idx_bits=max(1, math.ceil(math.log2(l_rows))), n_pages=n_pages, page=page)
    grid_spec = pltpu.PrefetchScalarGridSpec(
        num_scalar_prefetch=1,
        grid=(b, n_pages + 1),
        in_specs=[pl.BlockSpec((1, tq, 2 * LANES), lambda bi, p, pt: (bi, 0, _NEW["q_a"] // 256)),
                  pl.BlockSpec((1, tq, 2 * LANES), lambda bi, p, pt: (bi, 0, _NEW["iq"] // 256)),
                  pl.BlockSpec((1, tq, LANES), lambda bi, p, pt: (bi, 0, BLK_IK)),
                  pl.BlockSpec((1, 1, page, cache.shape[3]), _page_map(layer, n_pages - 1)),
                  pl.BlockSpec((1, kc, 2 * LANES), lambda bi, p, pt: (bi, 0, 0))],
        out_specs=pl.BlockSpec((1, tq, 2 * LANES), lambda bi, p, pt: (bi, 0, 0)),
        scratch_shapes=[pltpu.VMEM((l_rows, 2 * LANES), BF16)] + _dsa_scratch(n_kc, tq, kc))
    return pl.pallas_call(
        kern, grid_spec=grid_spec,
        out_shape=jax.ShapeDtypeStruct((b, tq, 2 * LANES), F32),
        compiler_params=_params(("parallel", "arbitrary")),
        name="dsa_decode",
    )(page_table, zq, zq, zq, cache, new_rows)


def _dsa_call(zq, kv, ik, *, kv_blk, ik_blk, tq, n_keys, kc=512):
    b, t_q, _ = zq.shape
    l_rows = kv.shape[1]
    n_kc = l_rows // kc
    kern = functools.partial(_dsa_kernel, tq=tq, kc=kc, n_top=min(DSA_TOPK, n_keys // 4),
                             idx_bits=max(1, math.ceil(math.log2(l_rows))))
    return pl.pallas_call(
        kern,
        grid=(b, t_q // tq),
        in_specs=[pl.BlockSpec((1, tq, 2 * LANES), lambda bi, i: (bi, i, _NEW["q_a"] // 256)),
                  pl.BlockSpec((1, tq, 2 * LANES), lambda bi, i: (bi, i, _NEW["iq"] // 256)),
                  pl.BlockSpec((1, tq, LANES), lambda bi, i: (bi, i, BLK_IK)),
                  pl.BlockSpec((1, l_rows, LANES), lambda bi, i: (bi, 0, kv_blk)),
                  pl.BlockSpec((1, l_rows, LANES), lambda bi, i: (bi, 0, ik_blk))],
        out_specs=pl.BlockSpec((1, tq, 2 * LANES), lambda bi, i: (bi, i, 0)),
        out_shape=jax.ShapeDtypeStruct((b, t_q, 2 * LANES), F32),
        scratch_shapes=_dsa_scratch(n_kc, tq, kc),
        compiler_params=_params(("parallel", "arbitrary")),
        name="dsa_attention",
    )(zq, zq, zq, kv, ik)


def _compress(load_rows, wlo_ref, whi_ref, pe_ref, w2_ref, n_blocks):
    a = jnp.zeros((n_blocks, LANES), F32)
    b = jnp.zeros((n_blocks, LANES), F32)
    for p in range(NSA_CMP_STRIDE):
        xp = load_rows(p)
        a = a + _dot((xp + pe_ref[p:p + 1, :]).astype(BF16), wlo_ref[p])
        b = b + _dot((xp + pe_ref[NSA_CMP_STRIDE + p:NSA_CMP_STRIDE + p + 1, :]).astype(BF16), whi_ref[p])
    h = jax.nn.gelu(a + pltpu.roll(b, n_blocks - 1, 0))
    return _dot(h.astype(BF16), w2_ref[...])


def _cmp_kernel(rows_ref, wlo_ref, whi_ref, pe_ref, w2_ref, o_ref, *, n_blocks):
    o_ref[0] = _compress(lambda p: rows_ref[0, pl.ds(p, n_blocks, stride=NSA_CMP_STRIDE), :],
                         wlo_ref, whi_ref, pe_ref, w2_ref, n_blocks)


def _block_diag2(m0, m1):
    z = jnp.zeros_like(m0)
    return jnp.concatenate([jnp.concatenate([m0, z], axis=-1), jnp.concatenate([z, m1], axis=-1)], axis=-2)


def _compress_weights(w_cmp1, w_cmp2, pe_cmp):
    w1 = w_cmp1.reshape(2, NSA_CMP_LEN, HEAD_DIM, HEAD_DIM)
    wlo = _block_diag2(w1[0, :NSA_CMP_STRIDE], w1[1, :NSA_CMP_STRIDE]).astype(BF16)
    whi = _block_diag2(w1[0, NSA_CMP_STRIDE:], w1[1, NSA_CMP_STRIDE:]).astype(BF16)
    pe = jnp.concatenate([pe_cmp[0], pe_cmp[1]], axis=-1)
    w2 = _block_diag2(w_cmp2[0], w_cmp2[1]).astype(BF16)
    return wlo, whi, pe, w2


def _compress_specs(index_map3, index_map2):
    return [pl.BlockSpec((NSA_CMP_STRIDE, LANES, LANES), index_map3),
            pl.BlockSpec((NSA_CMP_STRIDE, LANES, LANES), index_map3),
            pl.BlockSpec((NSA_CMP_LEN, LANES), index_map2),
            pl.BlockSpec((LANES, LANES), index_map2)]


def _nsa_compress(rows, cmp_w, *, blk):
    b, l_rows, _ = rows.shape
    n_blocks = l_rows // NSA_CMP_STRIDE
    return pl.pallas_call(
        functools.partial(_cmp_kernel, n_blocks=n_blocks),
        grid=(b,),
        in_specs=[pl.BlockSpec((1, l_rows, LANES), lambda bi: (bi, 0, blk))]
        + _compress_specs(lambda bi: (0, 0, 0), lambda bi: (0, 0)),
        out_specs=pl.BlockSpec((1, n_blocks, LANES), lambda bi: (bi, 0, 0)),
        out_shape=jax.ShapeDtypeStruct((b, n_blocks, LANES), F32),
        compiler_params=_params(("parallel",)),
        name="nsa_compress",
    )(rows, *cmp_w)


def _nsa_core(q, g, cmpv, load_sel, wrows, wpos, rank_ref, m_ref, l_ref, acc_ref,
              *, tq, kc, p0, n_cmp, n_blk, n_sel):
    pos = p0 + lax.broadcasted_iota(jnp.int32, (tq, 1), 0)
    qs = (_stack_heads(q) * SCALE).astype(BF16)

    ncp = cmpv.shape[0]
    s_c = _dot_nt(qs, cmpv)
    n_idx = _lane_iota(ncp)
    valid_c = (n_idx * NSA_CMP_STRIDE + (NSA_CMP_LEN - 1) <= pos) & (n_idx < n_cmp)
    o_c = []
    p_sum = jnp.zeros((tq, ncp), F32)
    for h in range(N_HEADS):
        p = _masked_softmax(s_c[h * tq:(h + 1) * tq], valid_c)
        p_sum = p_sum + p
        o_c.append(_dot(p.astype(BF16), cmpv))

    nbp = -(-n_blk // LANES) * LANES
    nn = lax.broadcasted_iota(jnp.int32, (ncp, nbp), 0)
    jb = lax.broadcasted_iota(jnp.int32, (ncp, nbp), 1)
    cover = ((nn * NSA_CMP_STRIDE < (jb + 1) * NSA_SEL_BLOCK)
             & (nn * NSA_CMP_STRIDE + (NSA_CMP_LEN - 1) >= jb * NSA_SEL_BLOCK) & (nn < n_cmp)).astype(BF16)
    hi, lo = _split_bf16(p_sum)
    imp = _dot(hi, cover) + _dot(lo, cover)
    j = _lane_iota(nbp)
    cur = pos // NSA_SEL_BLOCK
    forced = (j == 0) | (j == cur) | (j == cur - 1)
    imp = jnp.where(forced, jnp.inf, jnp.where(j <= cur, imp, NEG_INF))
    n_vis = jnp.minimum((p0 + tq - 1) // NSA_SEL_BLOCK + 1, n_blk)
    sel = _select_top(imp, rank_ref, n_vis, n_sel).astype(BF16)

    _softmax_init(m_ref, l_ref, acc_ref)
    n_chunks = (p0 + tq - 1) // kc + 1
    eb = lax.broadcasted_iota(jnp.int32, (nbp, kc), 0)
    ek = lax.broadcasted_iota(jnp.int32, (nbp, kc), 1)

    def attend(c, carry):
        rows = load_sel(c)
        s = _dot_nt(qs, rows)
        expand = (eb == (c * kc + ek) // NSA_SEL_BLOCK).astype(BF16)
        picked = _dot(sel, expand)
        valid = (picked > 0.5) & (c * kc + _lane_iota(kc) <= pos)
        for h in range(N_HEADS):
            _softmax_step(pl.ds(h * tq, tq), s[h * tq:(h + 1) * tq], valid, rows, m_ref, l_ref, acc_ref)
        return carry

    lax.fori_loop(0, n_chunks, attend, 0)

    s_w = _dot_nt(qs, wrows)
    valid_w = (wpos <= pos) & (pos - wpos < NSA_WINDOW)
    heads = []
    for h in range(N_HEADS):
        p_w = _masked_softmax(s_w[h * tq:(h + 1) * tq], valid_w)
        o_w = _dot(p_w.astype(BF16), wrows)
        o_s = _softmax_result(pl.ds(h * tq, tq), l_ref, acc_ref)
        heads.append(g[:, 3 * h:3 * h + 1] * o_c[h] + g[:, 3 * h + 1:3 * h + 2] * o_s
                     + g[:, 3 * h + 2:3 * h + 3] * o_w)
    return _unstack_heads(heads)


def _nsa_kernel(q_ref, g_ref, cmp_ref, sel_ref, win_ref, o_ref, rank_ref, m_ref, l_ref, acc_ref,
                *, tq, kc, n_cmp, n_blk, n_sel, win_rows):
    p0 = pl.program_id(1) * tq
    start = pl.multiple_of(jnp.clip(p0 - NSA_WINDOW, 0, win_ref.shape[1] - win_rows), SUBLANES)
    wrows = win_ref[0, pl.ds(start, win_rows), :].astype(BF16)
    o_ref[0] = _nsa_core(q_ref[0], g_ref[0], cmp_ref[0].astype(BF16), _chunk_loader(sel_ref, kc), wrows,
                         start + _lane_iota(win_rows), rank_ref, m_ref, l_ref, acc_ref,
                         tq=tq, kc=kc, p0=p0, n_cmp=n_cmp, n_blk=n_blk, n_sel=n_sel)


def _nsa_decode_kernel(pt_ref, q_ref, g_ref, page_ref, new_ref, win_ref, wlo_ref, whi_ref, pe_ref, w2_ref, o_ref,
                       cbuf_ref, sbuf_ref, rank_ref, m_ref, l_ref, acc_ref,
                       *, tq, kc, n_cmp, n_blk, n_sel, n_pages, page):
    p = pl.program_id(1)

    @pl.when(p < n_pages)
    def _():
        at = pl.ds(pl.multiple_of(p * page, page), page)
        cbuf_ref[at, :] = page_ref[0, 0, :, 0:LANES]
        sbuf_ref[at, :] = page_ref[0, 0, :, LANES:2 * LANES].astype(BF16)

    @pl.when(p == n_pages)
    def _():
        past = n_pages * page
        sbuf_ref[pl.ds(past, kc), :] = new_ref[0, :, LANES:2 * LANES].astype(BF16)
        n_blocks = past // NSA_CMP_STRIDE
        cmpv = _compress(lambda r: cbuf_ref[pl.ds(r, n_blocks, stride=NSA_CMP_STRIDE), :],
                         wlo_ref, whi_ref, pe_ref, w2_ref, n_blocks).astype(BF16)
        wrows = win_ref[0].astype(BF16)
        wpos = past - NSA_WINDOW + _lane_iota(wrows.shape[0])
        o_ref[0] = _nsa_core(q_ref[0], g_ref[0], cmpv, _chunk_loader(sbuf_ref, kc), wrows, wpos,
                             rank_ref, m_ref, l_ref, acc_ref, tq=tq, kc=kc, p0=past,
                             n_cmp=n_cmp, n_blk=n_blk, n_sel=n_sel)


def _nsa_sizes(n_keys):
    n_cmp = (n_keys - NSA_CMP_LEN) // NSA_CMP_STRIDE + 1
    n_blk = -(-n_keys // NSA_SEL_BLOCK)
    return n_cmp, n_blk, min(NSA_SEL_TOPN, n_blk), -(-n_blk // LANES) * LANES


def _nsa_decode_call(zq, cache, layer, page_table, new_rows, win_rows, cmp_w, *, kc=512):
    b, tq, _ = zq.shape
    n_pages = page_table.shape[1]
    page = cache.shape[2]
    past = n_pages * page
    n_cmp, n_blk, n_sel, nbp = _nsa_sizes(past + 1)
    kern = functools.partial(_nsa_decode_kernel, tq=tq, kc=kc, n_cmp=n_cmp, n_blk=n_blk, n_sel=n_sel,
                             n_pages=n_pages, page=page)
    grid_spec = pltpu.PrefetchScalarGridSpec(
        num_scalar_prefetch=1,
        grid=(b, n_pages + 1),
        in_specs=[pl.BlockSpec((1, tq, 2 * LANES), lambda bi, p, pt: (bi, 0, _NEW["q_b"] // 256)),
                  pl.BlockSpec((1, tq, LANES), lambda bi, p, pt: (bi, 0, BLK_GB)),
                  pl.BlockSpec((1, 1, page, cache.shape[3]), _page_map(layer, n_pages - 1)),
                  pl.BlockSpec((1, kc, 2 * LANES), lambda bi, p, pt: (bi, 0, 0)),
                  pl.BlockSpec((1,) + win_rows.shape[1:], lambda bi, p, pt: (bi, 0, 0))]
        + _compress_specs(lambda bi, p, pt: (0, 0, 0), lambda bi, p, pt: (0, 0)),
        out_specs=pl.BlockSpec((1, tq, 2 * LANES), lambda bi, p, pt: (bi, 0, 0)),
        scratch_shapes=[pltpu.VMEM((past, LANES), F32), pltpu.VMEM((past + kc, LANES), BF16),
                        pltpu.VMEM((nbp, max(tq, LANES)), F32)] + _softmax_scratch(N_HEADS * tq))
    return pl.pallas_call(
        kern, grid_spec=grid_spec,
        out_shape=jax.ShapeDtypeStruct((b, tq, 2 * LANES), F32),
        compiler_params=_params(("parallel", "arbitrary")),
        name="nsa_decode",
    )(page_table, zq, zq, cache, new_rows, win_rows, *cmp_w)


def _nsa_call(zq, cmp, sel, win, *, sel_blk, win_blk, tq, n_keys, kc=512):
    b, t_q, _ = zq.shape
    l_rows = sel.shape[1]
    n_cmp, n_blk, n_sel, nbp = _nsa_sizes(n_keys)
    kern = functools.partial(_nsa_kernel, tq=tq, kc=kc, n_cmp=n_cmp, n_blk=n_blk, n_sel=n_sel,
                             win_rows=min(NSA_WINDOW + tq, l_rows))
    return pl.pallas_call(
        kern,
        grid=(b, t_q // tq),
        in_specs=[pl.BlockSpec((1, tq, 2 * LANES), lambda bi, i: (bi, i, _NEW["q_b"] // 256)),
                  pl.BlockSpec((1, tq, LANES), lambda bi, i: (bi, i, BLK_GB)),
                  pl.BlockSpec((1, cmp.shape[1], LANES), lambda bi, i: (bi, 0, 0)),
                  pl.BlockSpec((1, l_rows, LANES), lambda bi, i: (bi, 0, sel_blk)),
                  pl.BlockSpec((1, l_rows, LANES), lambda bi, i: (bi, 0, win_blk))],
        out_specs=pl.BlockSpec((1, tq, 2 * LANES), lambda bi, i: (bi, i, 0)),
        out_shape=jax.ShapeDtypeStruct((b, t_q, 2 * LANES), F32),
        scratch_shapes=[pltpu.VMEM((nbp, tq), F32)] + _softmax_scratch(N_HEADS * tq),
        compiler_params=_params(("parallel", "arbitrary")),
        name="nsa_attention",
    )(zq, zq, cmp, sel, win)


def _moba_core(q, kmean, load_k, load_v, rank_ref, m_ref, l_ref, acc_ref, *, tq, c0, n_sel):
    nbp = kmean.shape[0]
    qst = _pair_heads(q)
    q_hi, q_lo = _split_bf16(qst)
    k_hi, k_lo = _split_bf16(kmean)
    gate = _dot_nt(q_hi, k_hi) + _dot_nt(q_lo, k_hi) + _dot_nt(q_hi, k_lo)
    gate = jnp.where(_lane_iota(nbp) < c0, gate, NEG_INF)
    sel = _select_top(gate, rank_ref, c0, n_sel).astype(BF16)
    qs = (qst * SCALE).astype(BF16)
    rows_all = pl.ds(0, 2 * tq)
    eb = lax.broadcasted_iota(jnp.int32, (nbp, MOBA_BLOCK), 0)

    _softmax_init(m_ref, l_ref, acc_ref)

    def attend(blk, carry):
        picked = _dot(sel, (eb == blk).astype(BF16))
        _softmax_step(rows_all, _dot_nt(qs, load_k(blk)), picked > 0.5, load_v(blk), m_ref, l_ref, acc_ref)
        return carry

    lax.fori_loop(0, c0, attend, 0)

    qi = lax.broadcasted_iota(jnp.int32, (tq, MOBA_BLOCK), 0)
    ki = lax.broadcasted_iota(jnp.int32, (tq, MOBA_BLOCK), 1)
    causal = ki <= qi
    _softmax_step(rows_all, _dot_nt(qs, load_k(c0)), jnp.concatenate([causal, causal], axis=0), load_v(c0),
                  m_ref, l_ref, acc_ref)
    o = _softmax_result(rows_all, l_ref, acc_ref)
    return jnp.where(_lane_iota() < HEAD_DIM, o[0:tq], o[tq:2 * tq])


def _moba_kernel(q_ref, k_ref, v_ref, o_ref, kmean_ref, rank_ref, m_ref, l_ref, acc_ref, *, tq, n_blocks, n_sel):
    i = pl.program_id(2)

    @pl.when(i == 0)
    def _():
        kmean_ref[...] = jnp.zeros(kmean_ref.shape, F32)

        def mean_block(blk, carry):
            rows = k_ref[0, pl.ds(pl.multiple_of(blk * MOBA_BLOCK, MOBA_BLOCK), MOBA_BLOCK), :]
            kmean_ref[pl.ds(blk, 1), :] = jnp.sum(rows, axis=0, keepdims=True) * (1.0 / MOBA_BLOCK)
            return carry

        lax.fori_loop(0, n_blocks, mean_block, 0)

    o_ref[0] = _moba_core(q_ref[0], kmean_ref[...], _chunk_loader(k_ref, MOBA_BLOCK), _chunk_loader(v_ref, MOBA_BLOCK),
                          rank_ref, m_ref, l_ref, acc_ref, tq=tq, c0=i, n_sel=n_sel)


def _moba_decode_kernel(pt_ref, q_ref, kpage_ref, vpage_ref, knew_ref, vnew_ref, o_ref, kbuf_ref, vbuf_ref,
                        kmean_ref, rank_ref, m_ref, l_ref, acc_ref, *, tq, n_sel, n_pages, page):
    p = pl.program_id(2)
    per_block = MOBA_BLOCK // page

    @pl.when(p == 0)
    def _():
        kmean_ref[...] = jnp.zeros(kmean_ref.shape, F32)

    @pl.when(p < n_pages)
    def _():
        at = pl.ds(pl.multiple_of(p * page, page), page)
        rows = kpage_ref[0, 0]
        kbuf_ref[at, :] = rows.astype(BF16)
        vbuf_ref[at, :] = vpage_ref[0, 0].astype(BF16)
        kmean_ref[pl.ds(p // per_block, 1), :] += jnp.sum(rows, axis=0, keepdims=True) * (1.0 / MOBA_BLOCK)

    @pl.when(p == n_pages)
    def _():
        past = n_pages * page
        kbuf_ref[pl.ds(past, MOBA_BLOCK), :] = knew_ref[0].astype(BF16)
        vbuf_ref[pl.ds(past, MOBA_BLOCK), :] = vnew_ref[0].astype(BF16)
        o_ref[0] = _moba_core(q_ref[0], kmean_ref[...], _chunk_loader(kbuf_ref, MOBA_BLOCK),
                              _chunk_loader(vbuf_ref, MOBA_BLOCK), rank_ref, m_ref, l_ref, acc_ref,
                              tq=tq, c0=past // MOBA_BLOCK, n_sel=n_sel)


def _moba_scratch(nbp, tq):
    return [pltpu.VMEM((nbp, LANES), F32), pltpu.VMEM((nbp, max(2 * tq, LANES)), F32)] + _softmax_scratch(2 * tq)


def _moba_decode_call(zq, cache, layer, page_table, new_rows):
    b, tq, _ = zq.shape
    n_pages = page_table.shape[1]
    page = cache.shape[2]
    past = n_pages * page
    n_blocks = past // MOBA_BLOCK + 1
    nbp = -(-n_blocks // LANES) * LANES
    kern = functools.partial(_moba_decode_kernel, tq=tq, n_sel=min(MOBA_TOPK, n_blocks), n_pages=n_pages, page=page)
    q_blk0 = _NEW["q_c"] // LANES
    last = n_pages - 1
    grid_spec = pltpu.PrefetchScalarGridSpec(
        num_scalar_prefetch=1,
        grid=(b, 2, n_pages + 1),
        in_specs=[pl.BlockSpec((1, tq, LANES), lambda bi, c, p, pt: (bi, 0, q_blk0 + c)),
                  pl.BlockSpec((1, 1, page, LANES), lambda bi, c, p, pt: (layer, pt[bi, jnp.minimum(p, last)], 0, c)),
                  pl.BlockSpec((1, 1, page, LANES),
                               lambda bi, c, p, pt: (layer, pt[bi, jnp.minimum(p, last)], 0, 2 + c)),
                  pl.BlockSpec((1, MOBA_BLOCK, LANES), lambda bi, c, p, pt: (bi, 0, c)),
                  pl.BlockSpec((1, MOBA_BLOCK, LANES), lambda bi, c, p, pt: (bi, 0, 2 + c))],
        out_specs=pl.BlockSpec((1, tq, LANES), lambda bi, c, p, pt: (bi, 0, c)),
        scratch_shapes=[pltpu.VMEM((past + MOBA_BLOCK, LANES), BF16), pltpu.VMEM((past + MOBA_BLOCK, LANES), BF16)]
        + _moba_scratch(nbp, tq))
    return pl.pallas_call(
        kern, grid_spec=grid_spec,
        out_shape=jax.ShapeDtypeStruct((b, tq, 2 * LANES), F32),
        compiler_params=_params(("parallel", "parallel", "arbitrary")),
        name="moba_decode",
    )(page_table, zq, cache, cache, new_rows, new_rows)


def _moba_call(zq, kv, *, tq):
    b, t_q, _ = zq.shape
    l_rows = kv.shape[1]
    assert tq == MOBA_BLOCK
    n_blocks = l_rows // MOBA_BLOCK
    nbp = -(-n_blocks // LANES) * LANES
    kern = functools.partial(_moba_kernel, tq=tq, n_blocks=n_blocks, n_sel=min(MOBA_TOPK, n_blocks))
    q_blk0 = _NEW["q_c"] // LANES
    return pl.pallas_call(
        kern,
        grid=(b, 2, t_q // tq),
        in_specs=[pl.BlockSpec((1, tq, LANES), lambda bi, c, i: (bi, i, q_blk0 + c)),
                  pl.BlockSpec((1, l_rows, LANES), lambda bi, c, i: (bi, 0, BLK_KC + c)),
                  pl.BlockSpec((1, l_rows, LANES), lambda bi, c, i: (bi, 0, BLK_VC + c))],
        out_specs=pl.BlockSpec((1, tq, LANES), lambda bi, c, i: (bi, i, c)),
        out_shape=jax.ShapeDtypeStruct((b, t_q, 2 * LANES), F32),
        scratch_shapes=_moba_scratch(nbp, tq),
        compiler_params=_params(("parallel", "parallel", "arbitrary")),
        name="moba_attention",
    )(zq, kv, kv)


def _memory_kv(mem, g_ln, w_kv, g_k):
    b, m_rows, d = mem.shape
    n = w_kv.shape[1]
    half = n // 2
    mask = np.zeros((n,), np.float32)
    mask[:half] = 1
    zero = jnp.zeros((n,), F32)
    gain = jnp.concatenate([jnp.tile(g_k.astype(F32), half // HEAD_DIM), jnp.ones((half,), F32)])
    cfg = jnp.stack([jnp.asarray(mask), gain, zero, zero, zero, zero, zero, zero])
    masks = dict(norm=mask, rope=np.zeros_like(mask), sig=np.zeros_like(mask))
    rope = jnp.zeros((m_rows, 3 * LANES), F32)
    out = _project(mem.reshape(b * m_rows, d), g_ln.reshape(1, d), w_kv.astype(BF16), cfg, rope, masks,
                   tm=m_rows, tn=n)
    return out.reshape(b, m_rows, n)


def _mem_kernel(q_ref, kv_ref, o_ref, *, tq):
    lane = _lane_iota()
    n_kv = kv_ref.shape[2] // 2
    chunks = []
    for c in range(2):
        qst = _pair_heads(q_ref[0, :, c * LANES:(c + 1) * LANES])
        kb = kv_ref[0, :, c * LANES:(c + 1) * LANES].astype(BF16)
        vb = kv_ref[0, :, n_kv + c * LANES:n_kv + (c + 1) * LANES].astype(BF16)
        s = _dot_nt((qst * SCALE).astype(BF16), kb)
        e = jnp.exp(s - jnp.max(s, axis=-1, keepdims=True))
        p = e / jnp.sum(e, axis=-1, keepdims=True)
        o = _dot(p.astype(BF16), vb)
        chunks.append(jnp.where(lane < HEAD_DIM, o[0:tq], o[tq:2 * tq]))
    o_ref[0] = jnp.concatenate(chunks, axis=1)


def _mem_call(zq, mkv, *, tq):
    b, t_q, _ = zq.shape
    tq = min(tq, t_q)
    return pl.pallas_call(
        functools.partial(_mem_kernel, tq=tq),
        grid=(b, t_q // tq),
        in_specs=[pl.BlockSpec((1, tq, 2 * LANES), lambda bi, i: (bi, i, _NEW["q_m"] // 256)),
                  pl.BlockSpec((1,) + mkv.shape[1:], lambda bi, i: (bi, 0, 0))],
        out_specs=pl.BlockSpec((1, tq, 2 * LANES), lambda bi, i: (bi, i, 0)),
        out_shape=jax.ShapeDtypeStruct((b, t_q, 2 * LANES), F32),
        compiler_params=_params(("parallel", "parallel")),
        name="mem_attention",
    )(zq, mkv)


def _combine_kernel(x_ref, oa_ref, ob_ref, oc_ref, om_ref, gate_ref, wb_ref, wo_ref, y_ref):
    d = x_ref.shape[-1]
    h = None
    for bi, o_ref in enumerate((oa_ref, ob_ref, oc_ref, om_ref)):
        t = gate_ref[:, bi * d:(bi + 1) * d] * _dot(o_ref[...].astype(BF16), wb_ref[bi])
        h = t if h is None else h + t
    y_ref[...] = x_ref[...] + _dot(h.astype(BF16), wo_ref[...])


def _combine(x2d, outs, gate, w_branch, w_out, *, tm):
    m, d = x2d.shape
    tm = min(tm, m)
    bw = outs[0].shape[-1]
    o_spec = pl.BlockSpec((tm, bw), lambda i: (i, 0))
    return pl.pallas_call(
        _combine_kernel,
        grid=(m // tm,),
        in_specs=[pl.BlockSpec((tm, d), lambda i: (i, 0)), o_spec, o_spec, o_spec, o_spec,
                  pl.BlockSpec((tm, N_BRANCH * d), lambda i: (i, 0)),
                  pl.BlockSpec((N_BRANCH, bw, d), lambda i: (0, 0, 0)),
                  pl.BlockSpec((d, d), lambda i: (0, 0))],
        out_specs=pl.BlockSpec((tm, d), lambda i: (i, 0)),
        out_shape=jax.ShapeDtypeStruct((m, d), F32),
        compiler_params=_params(("parallel",)),
        name="branch_mix",
    )(x2d, *outs, gate, w_branch, w_out)


FF_CHUNK = 256
HALO = 16


def _rms(x, g):
    return x * lax.rsqrt(jnp.mean(x * x, axis=-1, keepdims=True) + RMS_EPS) * g


def _conv3(cw, u2, u1, u0):
    return cw[3:4] + cw[0:1] * u2 + cw[1:2] * u1 + cw[2:3] * u0


def _ffn_kernel(x_ref, xh_ref, ha_ref, hb_ref, g_ref, wa_ref, wb_ref, cwa_ref, cwb_ref, wdn_ref,
                y_ref, sta_ref, stb_ref, xn_ref, xhn_ref, acc_ref, *, tm):
    i = pl.program_id(1)
    j = pl.program_id(2)

    @pl.when(j == 0)
    def _():
        xn_ref[...] = _rms(x_ref[0], g_ref[...]).astype(BF16)
        xhn_ref[...] = _rms(xh_ref[0], g_ref[...]).astype(BF16)
        acc_ref[...] = jnp.zeros(acc_ref.shape, F32)

    def half(w_ref, hist_ref, cw_ref, st_ref):
        u = _dot(xn_ref[...], w_ref[...])
        u_prev = _dot(xhn_ref[...], w_ref[...])[HALO - SUBLANES:HALO]
        prev = jnp.where(i == 0, hist_ref[0], u_prev)
        ext = jnp.concatenate([prev, u], axis=0)
        st_ref[0, 0] = ext[tm:tm + SUBLANES]
        return _conv3(cw_ref[...], pltpu.roll(ext, 2, 0)[SUBLANES:], pltpu.roll(ext, 1, 0)[SUBLANES:], u)

    a = half(wa_ref, ha_ref, cwa_ref, sta_ref)
    b = half(wb_ref, hb_ref, cwb_ref, stb_ref)
    acc_ref[...] += _dot((a * jax.nn.sigmoid(a) * b).astype(BF16), wdn_ref[...])

    @pl.when(j == pl.num_programs(2) - 1)
    def _():
        y_ref[0] = x_ref[0] + acc_ref[...]


def _conv_table(conv_w, conv_b):
    return jnp.concatenate([conv_w, conv_b[None, :], jnp.zeros((SUBLANES - CONV_WIDTH - 1, conv_b.shape[0]), F32)])


def _conv_ffn(x, hist, g, w_up, conv_w, conv_b, w_down, *, tm):
    b, t, d = x.shape
    d_ff = w_down.shape[0]
    n_j = d_ff // FF_CHUNK
    tm = min(tm, t)
    cw = _conv_table(conv_w, conv_b)
    hist8 = jnp.concatenate([jnp.zeros((b, SUBLANES - 2, 2 * d_ff), F32), hist], axis=1)
    a_col = lambda bi, i, j: (0, j)
    b_col = lambda bi, i, j: (0, n_j + j)
    st_spec = pl.BlockSpec((1, 1, SUBLANES, FF_CHUNK), lambda bi, i, j: (bi, i, 0, j))
    st_shape = jax.ShapeDtypeStruct((b, t // tm, SUBLANES, d_ff), F32)
    y, st_a, st_b = pl.pallas_call(
        functools.partial(_ffn_kernel, tm=tm),
        grid=(b, t // tm, n_j),
        in_specs=[pl.BlockSpec((1, tm, d), lambda bi, i, j: (bi, i, 0)),
                  pl.BlockSpec((1, HALO, d), lambda bi, i, j: (bi, jnp.maximum(i * (tm // HALO) - 1, 0), 0)),
                  pl.BlockSpec((1, SUBLANES, FF_CHUNK), lambda bi, i, j: (bi, 0, j)),
                  pl.BlockSpec((1, SUBLANES, FF_CHUNK), lambda bi, i, j: (bi, 0, n_j + j)),
                  pl.BlockSpec((1, d), lambda bi, i, j: (0, 0)),
                  pl.BlockSpec((d, FF_CHUNK), a_col), pl.BlockSpec((d, FF_CHUNK), b_col),
                  pl.BlockSpec((SUBLANES, FF_CHUNK), a_col), pl.BlockSpec((SUBLANES, FF_CHUNK), b_col),
                  pl.BlockSpec((FF_CHUNK, d), lambda bi, i, j: (j, 0))],
        out_specs=[pl.BlockSpec((1, tm, d), lambda bi, i, j: (bi, i, 0)), st_spec, st_spec],
        out_shape=[jax.ShapeDtypeStruct((b, t, d), F32), st_shape, st_shape],
        scratch_shapes=[pltpu.VMEM((tm, d), BF16), pltpu.VMEM((HALO, d), BF16), pltpu.VMEM((tm, d), F32)],
        compiler_params=_params(("parallel", "arbitrary", "arbitrary")),
        name="conv_ffn",
    )(x, x, hist8, hist8, g.reshape(1, d), w_up, w_up, cw, cw, w_down)
    return y, jnp.concatenate([st_a[:, -1, SUBLANES - 2:], st_b[:, -1, SUBLANES - 2:]], axis=-1)


def _ffn_row_kernel(x_ref, h0a_ref, h0b_ref, h1a_ref, h1b_ref, g_ref, wa_ref, wb_ref, cwa_ref, cwb_ref, wdn_ref,
                    y_ref, ua_ref, ub_ref, xn_ref, acc_ref):
    j = pl.program_id(0)

    @pl.when(j == 0)
    def _():
        xn_ref[...] = _rms(x_ref[...], g_ref[...]).astype(BF16)
        acc_ref[...] = jnp.zeros(acc_ref.shape, F32)

    ua = _dot(xn_ref[...], wa_ref[...])
    ub = _dot(xn_ref[...], wb_ref[...])
    ua_ref[...] = ua
    ub_ref[...] = ub
    a = _conv3(cwa_ref[...], h0a_ref[...], h1a_ref[...], ua)
    b = _conv3(cwb_ref[...], h0b_ref[...], h1b_ref[...], ub)
    acc_ref[...] += _dot((a * jax.nn.sigmoid(a) * b).astype(BF16), wdn_ref[...])

    @pl.when(j == pl.num_programs(0) - 1)
    def _():
        y_ref[...] = x_ref[...] + acc_ref[...]


def _conv_ffn_rows(x2d, hist, g, w_up, conv_w, conv_b, w_down):
    b, d = x2d.shape
    d_ff = w_down.shape[0]
    n_j = d_ff // FF_CHUNK
    cw = _conv_table(conv_w, conv_b)
    h0, h1 = hist[:, 0], hist[:, 1]
    a_col = lambda j: (0, j)
    b_col = lambda j: (0, n_j + j)
    row_a, row_b = pl.BlockSpec((b, FF_CHUNK), a_col), pl.BlockSpec((b, FF_CHUNK), b_col)
    y, ua, ub = pl.pallas_call(
        _ffn_row_kernel,
        grid=(n_j,),
        in_specs=[pl.BlockSpec((b, d), lambda j: (0, 0)), row_a, row_b, row_a, row_b,
                  pl.BlockSpec((1, d), lambda j: (0, 0)),
                  pl.BlockSpec((d, FF_CHUNK), a_col), pl.BlockSpec((d, FF_CHUNK), b_col),
                  pl.BlockSpec((SUBLANES, FF_CHUNK), a_col), pl.BlockSpec((SUBLANES, FF_CHUNK), b_col),
                  pl.BlockSpec((FF_CHUNK, d), lambda j: (j, 0))],
        out_specs=[pl.BlockSpec((b, d), lambda j: (0, 0)), row_a, row_a],
        out_shape=[jax.ShapeDtypeStruct((b, d), F32), jax.ShapeDtypeStruct((b, d_ff), F32),
                   jax.ShapeDtypeStruct((b, d_ff), F32)],
        scratch_shapes=[pltpu.VMEM((b, d), BF16), pltpu.VMEM((b, d), F32)],
        compiler_params=_params(("arbitrary",)),
        name="conv_ffn_rows",
    )(x2d, h0, h0, h1, h1, g.reshape(1, d), w_up, w_up, cw, cw, w_down)
    return y, jnp.stack([h1, jnp.concatenate([ua, ub], axis=-1)], axis=1)


def _cols(zh, name, width):
    return zh[..., _NEW[name]:_NEW[name] + width]


def _new_rows(zh):
    b, t, _ = zh.shape
    dsa = jnp.concatenate([_cols(zh, "k_a", 2 * HEAD_DIM), _cols(zh, "ik", HEAD_DIM)], axis=-1)
    nsa = _cols(zh, "kc", 4 * HEAD_DIM)
    moba = _cols(zh, "k_c", 2 * N_HEADS * HEAD_DIM)
    win = _cols(zh, "kw", 2 * HEAD_DIM)
    return (dsa.reshape(b, t, 3, HEAD_DIM), nsa.reshape(b, t, 4, HEAD_DIM),
            moba.reshape(b, t, 2, N_HEADS, HEAD_DIM), win.reshape(b, t, 2, HEAD_DIM))


def _prompt_layer(x, mem, p):
    b, t, d = x.shape
    zh2d, gate = _projections(x.reshape(b * t, d), jnp.arange(t, dtype=jnp.int32), p["ln"][0], p["w_heads"],
                              p["cfg"], p["w_gate"], tm=256)
    zh = zh2d.reshape(b, t, N_HEADCOLS)
    o_a = _dsa_call(zh, zh, zh, kv_blk=BLK_KV_A, ik_blk=BLK_IK, tq=128, n_keys=t)
    cmp = _nsa_compress(zh, p["cmp_w"], blk=BLK_CMP)
    o_b = _nsa_call(zh, cmp, zh, zh, sel_blk=BLK_SEL, win_blk=BLK_WIN, tq=128, n_keys=t)
    o_c = _moba_call(zh, zh, tq=MOBA_BLOCK)
    mkv = _memory_kv(mem, p["ln"][2], p["w_mem_kv"], p["g_mem"][1])
    o_m = _mem_call(zh, mkv, tq=256)
    outs = [o.reshape(b * t, o.shape[-1]) for o in (o_a, o_b, o_c, o_m)]
    x1 = _combine(x.reshape(b * t, d), outs, gate, p["w_branch"], p["w_out"], tm=512).reshape(b, t, d)
    hist = jnp.zeros((b, CONV_WIDTH - 1, p["w_up"].shape[1]), F32)
    y, conv = _conv_ffn(x1, hist, p["ln"][1], p["w_up"], p["conv_w"], p["conv_b"], p["w_down"], tm=1024)
    dsa, nsa, moba, win = _new_rows(zh)
    keep = min(NSA_WINDOW, t)
    return y, dsa, nsa, moba, win[:, t - keep:], mkv.reshape(b, mkv.shape[1], 2, N_HEADS, HEAD_DIM), conv


def _pad_rows(rows, n):
    return jnp.pad(rows[:, None, :], ((0, 0), (0, n - 1), (0, 0)))


def _sample_layer(x, layer, caches, page_table, win_state, mem_kv, conv_hist, p, *, kc=512):
    b, _, d = x.shape
    cache_dsa, cache_nsa, cache_moba = caches
    past = page_table.shape[1] * cache_dsa.shape[2]
    x2d = x.reshape(b, d)
    zh, gate = _projections(x2d, jnp.full((b,), past, jnp.int32), p["ln"][0], p["w_heads"], p["cfg"],
                            p["w_gate"], tm=b)
    zq = _pad_rows(zh, DEC_ROWS)
    dsa, nsa, moba, win = _new_rows(zh[:, None, :])
    dsa_row = jnp.concatenate([dsa.reshape(b, 3 * HEAD_DIM), jnp.zeros((b, HEAD_DIM), F32)], axis=-1)
    o_a = _dsa_decode_call(zq, cache_dsa, layer, page_table, _pad_rows(dsa_row, kc), kc=kc)
    win_all = jnp.concatenate([win_state.reshape(b, -1, 2 * HEAD_DIM), win.reshape(b, 1, 2 * HEAD_DIM)], axis=1)
    w_pad = -(-win_all.shape[1] // LANES) * LANES
    win_rows = jnp.pad(win_all, ((0, 0), (0, w_pad - win_all.shape[1]), (0, 0)))
    o_b = _nsa_decode_call(zq, cache_nsa, layer, page_table, _pad_rows(nsa.reshape(b, 4 * HEAD_DIM), kc),
                           win_rows, p["cmp_w"], kc=kc)
    o_c = _moba_decode_call(zq, cache_moba, layer, page_table,
                            _pad_rows(moba.reshape(b, 2 * N_HEADS * HEAD_DIM), MOBA_BLOCK))
    o_m = _mem_call(zq, mem_kv.reshape(b, mem_kv.shape[1], -1), tq=DEC_ROWS)
    outs = [o[:, 0, :] for o in (o_a, o_b, o_c, o_m)]
    x1 = _combine(x2d, outs, gate, p["w_branch"], p["w_out"], tm=b)
    y, conv = _conv_ffn_rows(x1, conv_hist, p["ln"][1], p["w_up"], p["conv_w"], p["conv_b"], p["w_down"])
    keep = win_state.shape[1]
    win_new = win_all[:, win_all.shape[1] - keep:].reshape(b, keep, 2, HEAD_DIM)
    return y.reshape(b, 1, d), dsa, nsa, moba, win_new, conv


def kernel(x_prompt, x_sample, cache_dsa, cache_nsa, cache_moba, state_nsa_win, cache_mem, state_ffn_conv,
           page_table, mem_prompt, ln, w_in, g_dsa, g_nsa, g_moba, g_mem, w_mem_kv, w_cmp1, w_cmp2, pe_cmp,
           w_branch, w_out, w_up, conv_w, conv_b, w_down):
    depth = ln.shape[0]
    caches = (cache_dsa.reshape(*cache_dsa.shape[:3], -1), cache_nsa.reshape(*cache_nsa.shape[:3], -1),
              cache_moba.reshape(*cache_moba.shape[:3], -1))
    xp, xs = x_prompt, x_sample
    outs_p = [[] for _ in range(6)]
    outs_s = [[] for _ in range(5)]
    for l in range(depth):
        w_heads, cfg = _head_weights(w_in[l], g_dsa[l], g_nsa[l], g_moba[l], g_mem[l])
        p = dict(ln=ln[l], w_heads=w_heads, cfg=cfg, w_gate=w_in[l][:, GATE_ORIG:].astype(BF16), g_mem=g_mem[l],
                 w_mem_kv=w_mem_kv[l], cmp_w=_compress_weights(w_cmp1[l], w_cmp2[l], pe_cmp[l]),
                 w_branch=w_branch[l].astype(BF16), w_out=w_out[l].astype(BF16), w_up=w_up[l].astype(BF16),
                 conv_w=conv_w[l], conv_b=conv_b[l], w_down=w_down[l].astype(BF16))
        xp, *rest = _prompt_layer(xp, mem_prompt, p)
        for acc, r in zip(outs_p, rest):
            acc.append(r)
        xs, *rest = _sample_layer(xs, l, caches, page_table, state_nsa_win[l], cache_mem[l], state_ffn_conv[l], p)
        for acc, r in zip(outs_s, rest):
            acc.append(r)
    dsa_p, nsa_p, moba_p, win_p, memkv_p, conv_p = [jnp.stack(a) for a in outs_p]
    dsa_s, nsa_s, moba_s, win_s, conv_s = [jnp.stack(a) for a in outs_s]
    return (xp, xs, dsa_p, dsa_s, nsa_p, nsa_s, moba_p, moba_s, win_p, win_s, memkv_p, conv_p, conv_s)
```

```python
import functools
import math

import numpy as np
import jax
import jax.numpy as jnp
from jax import lax
from jax.experimental import pallas as pl
from jax.experimental.pallas import tpu as pltpu

HEAD_DIM = 64
ROPE_DIM = HEAD_DIM // 4
ROPE_THETA = 500000.0
N_HEADS = 4
DSA_TOPK = 256
NSA_CMP_LEN = 32
NSA_CMP_STRIDE = 16
NSA_SEL_BLOCK = 64
NSA_SEL_TOPN = 16
NSA_WINDOW = 512
MOBA_BLOCK = 256
MOBA_TOPK = 3
N_BRANCH = 4
CONV_WIDTH = 3
RMS_EPS = 1e-6

LANES = 128
SUBLANES = 8
VMEM_LIMIT = 56 * 1024 * 1024
DEC_ROWS = SUBLANES
KEY_CHUNK = 512
DEC_KEY_CHUNK = 2048
PAGES_PER_STEP = 8

F32 = jnp.float32
BF16 = jnp.bfloat16
NEG_INF = float("-inf")
SCALE = HEAD_DIM ** -0.5

KEY_NEG_INF = int(np.uint32(0xFF800000) ^ np.uint32(0x7FFFFFFF)) - 2 ** 32
KEY_POS_INF = 0x7F800000
INT_MIN = -2 ** 31

_NT = (((1,), (1,)), ((), ()))

_ORIG = dict(q_a=0, k_a=256, v_a=320, iq=384, ik=640, iw=704, q_b=708, kc=964, vc=1028, ks=1092, vs=1156,
             kw=1220, vw=1284, g_b=1348, q_c=1360, k_c=1616, v_c=1872, q_m=2128)
GATE_ORIG = 2384
_NEW = dict(q_a=0, iq=256, q_b=512, q_c=768, q_m=1024, k_c=1280, v_c=1536, k_a=1792, v_a=1856, ik=1920,
            iw=1984, kc=2048, vc=2112, ks=2176, vs=2240, kw=2304, vw=2368, g_b=2432)
_WIDTH = dict(q_a=256, iq=256, q_b=256, q_c=256, q_m=256, k_c=256, v_c=256, k_a=64, v_a=64, ik=64, iw=4,
              kc=64, vc=64, ks=64, vs=64, kw=64, vw=64, g_b=12)
N_HEADCOLS = 2560
_NORMED = ("q_a", "q_b", "q_c", "q_m", "k_c", "k_a", "kc", "ks", "kw")
_ROPED = ("q_a", "iq", "q_b", "q_c", "k_c", "k_a", "ik", "kc", "ks", "kw")
_SIGMOID = ("g_b",)

BLK_KV_A, BLK_IK, BLK_CMP, BLK_SEL, BLK_WIN, BLK_GB = 14, 15, 16, 17, 18, 19
BLK_KC, BLK_VC = 10, 12


def _head_layout():
    masks = {k: np.zeros((N_HEADCOLS,), np.float32) for k in ("norm", "rope", "sig")}
    for name, new in _NEW.items():
        w = _WIDTH[name]
        if name in _NORMED:
            masks["norm"][new:new + w] = 1
        if name in _ROPED:
            masks["rope"][new:new + w] = 1
        if name in _SIGMOID:
            masks["sig"][new:new + w] = 1
    return masks


_COL_MASKS = _head_layout()


def _chunk_any(mask):
    return tuple(bool(mask[c * LANES:(c + 1) * LANES].any()) for c in range(mask.shape[0] // LANES))


def _params(sem):
    return pltpu.CompilerParams(dimension_semantics=sem, vmem_limit_bytes=VMEM_LIMIT)


def _lane_iota(n=LANES):
    return lax.broadcasted_iota(jnp.int32, (1, n), 1)


def _row_iota(n):
    return lax.broadcasted_iota(jnp.int32, (n, 1), 0)


def _split_bf16(x):
    hi = x.astype(BF16)
    lo = (x - hi.astype(F32)).astype(BF16)
    return hi, lo


def _dot(a, b):
    return jnp.dot(a, b, preferred_element_type=F32)


def _dot_nt(a, b):
    return lax.dot_general(a, b, _NT, preferred_element_type=F32)


def _proj_kernel(x_ref, g_ref, w_ref, cfg_ref, rope_ref, o_ref, *, norm_chunks, rope_chunks, sig_chunks):
    x = x_ref[...]
    xn = x * lax.rsqrt(jnp.mean(x * x, axis=-1, keepdims=True) + RMS_EPS) * g_ref[...]
    z = _dot(xn.astype(BF16), w_ref[...])
    rr = lax.broadcasted_iota(jnp.int32, (LANES, LANES), 0) // HEAD_DIM
    cc = lax.broadcasted_iota(jnp.int32, (LANES, LANES), 1) // HEAD_DIM
    seg = (rr == cc).astype(BF16)
    for c in range(z.shape[1] // LANES):
        sl = slice(c * LANES, (c + 1) * LANES)
        zc = z[:, sl]
        if norm_chunks[c]:
            hi, lo = _split_bf16(zc * zc)
            ss = _dot(hi, seg) + _dot(lo, seg)
            r = lax.rsqrt(ss * (1.0 / HEAD_DIM) + RMS_EPS)
            zc = jnp.where(cfg_ref[0:1, sl] > 0, zc * r * cfg_ref[1:2, sl], zc)
        if rope_chunks[c]:
            rot = (zc * rope_ref[:, 0:LANES]
                   + pltpu.roll(zc, LANES - ROPE_DIM // 2, 1) * rope_ref[:, LANES:2 * LANES]
                   + pltpu.roll(zc, ROPE_DIM // 2, 1) * rope_ref[:, 2 * LANES:3 * LANES])
            zc = jnp.where(cfg_ref[2:3, sl] > 0, rot, zc)
        if sig_chunks[c]:
            zc = jnp.where(cfg_ref[3:4, sl] > 0, jax.nn.sigmoid(zc), zc)
        o_ref[:, sl] = zc


def _project(x2d, g, w_bf16, cfg, rope, masks, *, tm, tn):
    m, d = x2d.shape
    n = w_bf16.shape[1]
    tn = min(tn, n)
    tm = min(tm, m)
    flags = {k: _chunk_any(v) for k, v in masks.items()}
    per_tile = tn // LANES
    for k, v in flags.items():
        assert all(v[t * per_tile:(t + 1) * per_tile] == v[:per_tile] for t in range(n // tn)), k
    rope_tiles = rope.shape[0] // tm
    kern = functools.partial(_proj_kernel, norm_chunks=flags["norm"][:per_tile],
                             rope_chunks=flags["rope"][:per_tile], sig_chunks=flags["sig"][:per_tile])
    return pl.pallas_call(
        kern,
        grid=(m // tm, n // tn),
        in_specs=[pl.BlockSpec((tm, d), lambda i, j: (i, 0)),
                  pl.BlockSpec((1, d), lambda i, j: (0, 0)),
                  pl.BlockSpec((d, tn), lambda i, j: (0, j)),
                  pl.BlockSpec((SUBLANES, tn), lambda i, j: (0, j)),
                  pl.BlockSpec((tm, 3 * LANES), lambda i, j: (i % rope_tiles, 0))],
        out_specs=pl.BlockSpec((tm, tn), lambda i, j: (i, j)),
        out_shape=jax.ShapeDtypeStruct((m, n), F32),
        compiler_params=_params(("parallel", "arbitrary")),
        name="project",
    )(x2d, g, w_bf16, cfg, rope)


def _rope_table(pos):
    half = ROPE_DIM // 2
    inv_freq = ROPE_THETA ** (-jnp.arange(half, dtype=F32) / half)
    ang = pos.astype(F32)[:, None] * inv_freq[None, :]
    cos, sin = jnp.cos(ang), jnp.sin(ang)
    t = pos.shape[0]
    ones = jnp.ones((t, HEAD_DIM - ROPE_DIM), F32)
    zeros = jnp.zeros((t, HEAD_DIM - ROPE_DIM), F32)
    zh = jnp.zeros((t, half), F32)
    c64 = jnp.concatenate([cos, cos, ones], axis=1)
    s1 = jnp.concatenate([-sin, zh, zeros], axis=1)
    s2 = jnp.concatenate([zh, sin, zeros], axis=1)
    return jnp.concatenate([c64, c64, s1, s1, s2, s2], axis=1)


def _head_weights(w_in_l, g_dsa, g_nsa, g_moba, g_mem):
    d = w_in_l.shape[0]
    pieces, at = [], 0
    for name, new in sorted(_NEW.items(), key=lambda kv: kv[1]):
        if new > at:
            pieces.append(jnp.zeros((d, new - at), w_in_l.dtype))
        pieces.append(w_in_l[:, _ORIG[name]:_ORIG[name] + _WIDTH[name]])
        at = new + _WIDTH[name]
    pieces.append(jnp.zeros((d, N_HEADCOLS - at), w_in_l.dtype))
    w = jnp.concatenate(pieces, axis=1).astype(BF16)
    gain = jnp.ones((N_HEADCOLS,), F32)
    for name, gvec in (("q_a", g_dsa[0]), ("k_a", g_dsa[1]), ("q_b", g_nsa[0]), ("kc", g_nsa[1]),
                       ("ks", g_nsa[2]), ("kw", g_nsa[3]), ("q_c", g_moba[0]), ("k_c", g_moba[1]),
                       ("q_m", g_mem[0])):
        reps = _WIDTH[name] // HEAD_DIM
        gain = lax.dynamic_update_slice(gain, jnp.tile(gvec.astype(F32), reps), (_NEW[name],))
    zero = jnp.zeros((N_HEADCOLS,), F32)
    cfg = jnp.stack([jnp.asarray(_COL_MASKS["norm"]), gain, jnp.asarray(_COL_MASKS["rope"]),
                     jnp.asarray(_COL_MASKS["sig"]), zero, zero, zero, zero])
    return w, cfg


def _gate_cfg(n):
    z = jnp.zeros((n,), F32)
    o = jnp.ones((n,), F32)
    return jnp.stack([z, o, z, o, z, z, z, z])


def _projections(x2d, pos, ln0, w_heads, cfg, w_gate, *, tm):
    d = x2d.shape[1]
    rope = _rope_table(pos)
    g = ln0.reshape(1, d)
    zh = _project(x2d, g, w_heads, cfg, rope, _COL_MASKS, tm=tm, tn=N_HEADCOLS)
    n_g = w_gate.shape[1]
    gmask = dict(norm=np.zeros((n_g,), np.float32), rope=np.zeros((n_g,), np.float32),
                 sig=np.ones((n_g,), np.float32))
    gate = _project(x2d, g, w_gate, _gate_cfg(n_g), rope, gmask, tm=tm, tn=1024)
    return zh, gate


def _stack_heads(x256):
    lane = _lane_iota()
    parts = []
    for c in range(2):
        ch = x256[:, c * LANES:(c + 1) * LANES]
        parts.append(jnp.where(lane < HEAD_DIM, ch, 0.0))
        parts.append(jnp.where(lane < HEAD_DIM, pltpu.roll(ch, HEAD_DIM, 1), 0.0))
    return jnp.concatenate(parts, axis=0)


def _unstack_heads_t(o_t, tq):
    o = o_t.T
    lane = _lane_iota()
    chunks = []
    for c in range(2):
        even, odd = o[2 * c * tq:(2 * c + 1) * tq], o[(2 * c + 1) * tq:(2 * c + 2) * tq]
        chunks.append(jnp.where(lane < HEAD_DIM, pltpu.roll(even, HEAD_DIM, 1), odd))
    return jnp.concatenate(chunks, axis=1)


def _pair_heads(q128):
    lane = _lane_iota()
    return jnp.concatenate([jnp.where(lane < HEAD_DIM, q128, 0.0), jnp.where(lane >= HEAD_DIM, q128, 0.0)], axis=0)


def _rows_t(x):
    tq = x.shape[0]
    if tq < LANES:
        x = jnp.concatenate([x, jnp.zeros((LANES - tq, x.shape[1]), x.dtype)], axis=0)
    return x.T[:, 0:tq]


def _tile_lanes(x, n):
    return jnp.concatenate([x] * n, axis=1)


def _tsoftmax_init(m_ref, l_ref, acc_ref):
    m_ref[...] = jnp.full(m_ref.shape, NEG_INF, F32)
    l_ref[...] = jnp.zeros(l_ref.shape, F32)
    acc_ref[...] = jnp.zeros(acc_ref.shape, F32)


def _tsoftmax_step(s_t, v_t, m_ref, l_ref, acc_ref):
    m_old = m_ref[...]
    m_new = jnp.maximum(m_old, jnp.max(s_t, axis=0, keepdims=True))
    m_safe = jnp.where(m_new == NEG_INF, 0.0, m_new)
    p = jnp.exp(s_t - m_safe)
    alpha = jnp.exp(m_old - m_safe)
    l_ref[...] = alpha * l_ref[...] + jnp.sum(p, axis=0, keepdims=True)
    acc_ref[...] = alpha * acc_ref[...] + _dot(v_t, p.astype(BF16))
    m_ref[...] = m_new


def _tsoftmax_result(l_ref, acc_ref):
    return acc_ref[...] / jnp.maximum(l_ref[...], 1e-30)


def _tmasked_softmax(s_t, valid_t):
    s = jnp.where(valid_t, s_t, NEG_INF)
    m = jnp.max(s, axis=0, keepdims=True)
    m = jnp.where(m == NEG_INF, 0.0, m)
    e = jnp.exp(s - m)
    return e / jnp.maximum(jnp.sum(e, axis=0, keepdims=True), 1e-30)


def _tsoftmax_scratch(d, r):
    return [pltpu.VMEM((1, r), F32), pltpu.VMEM((1, r), F32), pltpu.VMEM((d, r), F32)]


def _chunk_loader(ref, kc):
    def load(c):
        rows = pl.ds(pl.multiple_of(c * kc, kc), kc)
        x = ref[0, rows, :] if len(ref.shape) == 3 else ref[rows, :]
        return x.astype(BF16)
    return load


def _fill_transposed(src_ref, dst_ref, kc):
    def body(c, carry):
        dst_ref[c] = src_ref[0, pl.ds(pl.multiple_of(c * kc, kc), kc), :].T.astype(BF16)
        return carry
    lax.fori_loop(0, dst_ref.shape[0], body, 0)


def _rank_rows(val_t, src_ref, n_rows):
    src_ref[...] = val_t
    j = _row_iota(val_t.shape[0])

    def body(i, rank):
        row = src_ref[pl.ds(i, 1), :]
        beats = (row > val_t) | ((row == val_t) & (i < j))
        return rank + jnp.where(beats, 1, 0)

    return lax.fori_loop(0, n_rows, body, jnp.zeros(val_t.shape, jnp.int32))


def _select_top_t(val_t, rank_ref, n_valid, n_sel):
    rank = _rank_rows(val_t, rank_ref, n_valid)
    return jnp.where((rank < n_sel) & (val_t > NEG_INF), 1.0, 0.0)


def _dsa_core(q, iq, iw, load_kv, load_ik, load_kvt, key_ref, m_ref, l_ref, acc_ref,
              *, tq, kc, n_top, p0, idx_bits):
    n_chunks = (p0 + tq - 1) // kc + 1
    pos = p0 + _lane_iota(tq)
    iqs_t = _stack_heads(iq).T.astype(BF16)
    qs_t = (_stack_heads(q) * SCALE).T.astype(BF16)
    iw_t = _rows_t(iw)
    w_rows = [iw_t[HEAD_DIM + h:HEAD_DIM + h + 1, :] for h in range(N_HEADS)]

    def chunk_pos(c):
        return c * kc + _row_iota(kc)

    def score_chunk(c, carry):
        lg = _dot(load_ik(c), iqs_t)
        sc = w_rows[0] * jnp.maximum(lg[:, 0:tq], 0.0)
        for h in range(1, N_HEADS):
            sc = sc + w_rows[h] * jnp.maximum(lg[:, h * tq:(h + 1) * tq], 0.0)
        sc = jnp.where(chunk_pos(c) <= pos, sc, NEG_INF)
        bits = pltpu.bitcast(sc, jnp.int32)
        key_ref[c] = bits ^ ((bits >> 31) & 0x7FFFFFFF)
        return carry

    lax.fori_loop(0, n_chunks, score_chunk, 0)

    def count(pred):
        def body(c, acc):
            hit = jnp.where(pred(key_ref[c], chunk_pos(c)), 1, 0)
            parts = [hit[r * SUBLANES:(r + 1) * SUBLANES] for r in range(kc // SUBLANES)]
            while len(parts) > 1:
                parts = [a + b for a, b in zip(parts[0::2], parts[1::2])]
            return acc + parts[0]
        acc = lax.fori_loop(0, n_chunks, body, jnp.zeros((SUBLANES, tq), jnp.int32))
        return jnp.sum(acc, axis=0, keepdims=True)

    def thr_bit(b, thr):
        cand = thr + jnp.left_shift(jnp.int32(1), 31 - b)
        cnt = count(lambda key, kpos: key >= cand)
        return jnp.where(cnt >= n_top, cand, thr)

    thr = lax.fori_loop(0, 32, thr_bit, jnp.full((1, tq), INT_MIN, jnp.int32))

    def last_tied():
        need = n_top - count(lambda key, kpos: key > thr)

        def idx_bit(b, last):
            cand = last + jnp.left_shift(jnp.int32(1), idx_bits - 1 - b)
            cnt = count(lambda key, kpos: (key == thr) & (kpos < cand))
            return jnp.where(cnt < need, cand, last)

        return lax.fori_loop(0, idx_bits, idx_bit, jnp.zeros((1, tq), jnp.int32))

    over = (count(lambda key, kpos: key >= thr) > n_top) & (thr > KEY_NEG_INF)
    last = lax.cond(jnp.max(jnp.where(over, 1, 0)) > 0, last_tied,
                    lambda: jnp.full((1, tq), 2 ** idx_bits, jnp.int32))

    _tsoftmax_init(m_ref, l_ref, acc_ref)

    def attend(c, carry):
        s = _dot(load_kv(c), qs_t)
        key = key_ref[c]
        sel = (key > thr) | ((key == thr) & (chunk_pos(c) <= last))
        sel = sel & (key > KEY_NEG_INF) & (key < KEY_POS_INF)
        s = jnp.concatenate([jnp.where(sel, s[:, h * tq:(h + 1) * tq], NEG_INF) for h in range(N_HEADS)], axis=1)
        _tsoftmax_step(s, load_kvt(c), m_ref, l_ref, acc_ref)
        return carry

    lax.fori_loop(0, n_chunks, attend, 0)
    return _unstack_heads_t(_tsoftmax_result(l_ref, acc_ref), tq)


def _dsa_kernel(q_ref, iq_ref, iw_ref, kv_ref, ik_ref, o_ref, kvt_ref, key_ref, m_ref, l_ref, acc_ref,
                *, tq, kc, n_top, idx_bits):
    i = pl.program_id(1)

    @pl.when(i == 0)
    def _():
        _fill_transposed(kv_ref, kvt_ref, kc)

    o_ref[0] = _dsa_core(q_ref[0], iq_ref[0], iw_ref[0], _chunk_loader(kv_ref, kc), _chunk_loader(ik_ref, kc),
                         lambda c: kvt_ref[c], key_ref, m_ref, l_ref, acc_ref,
                         tq=tq, kc=kc, n_top=n_top, p0=i * tq, idx_bits=idx_bits)


def _row_softmax_step(rows, s, valid, v_bf16, m_ref, l_ref, acc_ref):
    s = jnp.where(valid, s, NEG_INF)
    m_old = m_ref[rows, :]
    m_new = jnp.maximum(m_old, jnp.max(s, axis=-1, keepdims=True))
    m_safe = jnp.where(m_new == NEG_INF, 0.0, m_new)
    p = jnp.exp(s - m_safe)
    alpha = jnp.exp(m_old - m_safe)
    l_ref[rows, :] = alpha * l_ref[rows, :] + jnp.sum(p, axis=-1, keepdims=True)
    acc_ref[rows, :] = alpha * acc_ref[rows, :] + _dot(p.astype(BF16), v_bf16)
    m_ref[rows, :] = m_new


def _row_softmax_init(m_ref, l_ref, acc_ref):
    m_ref[...] = jnp.full(m_ref.shape, NEG_INF, F32)
    l_ref[...] = jnp.zeros(l_ref.shape, F32)
    acc_ref[...] = jnp.zeros(acc_ref.shape, F32)


def _row_softmax_result(rows, l_ref, acc_ref):
    return acc_ref[rows, :] / jnp.maximum(l_ref[rows, :], 1e-30)


def _row_masked_softmax(s, valid):
    s = jnp.where(valid, s, NEG_INF)
    m = jnp.max(s, axis=-1, keepdims=True)
    m = jnp.where(m == NEG_INF, 0.0, m)
    e = jnp.exp(s - m)
    return e / jnp.maximum(jnp.sum(e, axis=-1, keepdims=True), 1e-30)


def _row_softmax_scratch(rows, d):
    return [pltpu.VMEM((rows, 1), F32), pltpu.VMEM((rows, 1), F32), pltpu.VMEM((rows, d), F32)]


def _unstack_heads(o_heads):
    lane = _lane_iota()
    chunks = []
    for c in range(2):
        chunks.append(jnp.where(lane < HEAD_DIM, pltpu.roll(o_heads[2 * c], HEAD_DIM, 1), o_heads[2 * c + 1]))
    return jnp.concatenate(chunks, axis=1)


def _dsa_rows_core(q, iq, iw, load_kv, load_ik, key_ref, m_ref, l_ref, acc_ref, *, tq, kc, n_top, p0, idx_bits):
    n_chunks = (p0 + tq - 1) // kc + 1
    pos = p0 + _row_iota(tq)
    iqs = _stack_heads(iq).astype(BF16)
    qs = (_stack_heads(q) * SCALE).astype(BF16)
    w_cols = [iw[:, HEAD_DIM + h:HEAD_DIM + h + 1] for h in range(N_HEADS)]

    def chunk_pos(c):
        return c * kc + _lane_iota(kc)

    def score_chunk(c, carry):
        lg = _dot_nt(iqs, load_ik(c))
        sc = w_cols[0] * jnp.maximum(lg[0:tq], 0.0)
        for h in range(1, N_HEADS):
            sc = sc + w_cols[h] * jnp.maximum(lg[h * tq:(h + 1) * tq], 0.0)
        sc = jnp.where(chunk_pos(c) <= pos, sc, NEG_INF)
        bits = pltpu.bitcast(sc, jnp.int32)
        key_ref[c] = bits ^ ((bits >> 31) & 0x7FFFFFFF)
        return carry

    lax.fori_loop(0, n_chunks, score_chunk, 0)

    def count(pred):
        def body(c, acc):
            hit = jnp.where(pred(key_ref[c], chunk_pos(c)), 1, 0)
            parts = [hit[:, t * LANES:(t + 1) * LANES] for t in range(kc // LANES)]
            while len(parts) > 1:
                parts = [a + b for a, b in zip(parts[0::2], parts[1::2])]
            return acc + parts[0]
        acc = lax.fori_loop(0, n_chunks, body, jnp.zeros((tq, LANES), jnp.int32))
        return jnp.sum(acc, axis=1, keepdims=True)

    def thr_bit(b, thr):
        cand = thr + jnp.left_shift(jnp.int32(1), 31 - b)
        cnt = count(lambda key, kpos: key >= cand)
        return jnp.where(cnt >= n_top, cand, thr)

    thr = lax.fori_loop(0, 32, thr_bit, jnp.full((tq, 1), INT_MIN, jnp.int32))

    def last_tied():
        need = n_top - count(lambda key, kpos: key > thr)

        def idx_bit(b, last):
            cand = last + jnp.left_shift(jnp.int32(1), idx_bits - 1 - b)
            cnt = count(lambda key, kpos: (key == thr) & (kpos < cand))
            return jnp.where(cnt < need, cand, last)

        return lax.fori_loop(0, idx_bits, idx_bit, jnp.zeros((tq, 1), jnp.int32))

    over = (count(lambda key, kpos: key >= thr) > n_top) & (thr > KEY_NEG_INF)
    last = lax.cond(jnp.max(jnp.where(over, 1, 0)) > 0, last_tied,
                    lambda: jnp.full((tq, 1), 2 ** idx_bits, jnp.int32))

    _row_softmax_init(m_ref, l_ref, acc_ref)

    def attend(c, carry):
        kvc = load_kv(c)
        s = _dot_nt(qs, kvc)
        key = key_ref[c]
        sel = (key > thr) | ((key == thr) & (chunk_pos(c) <= last))
        sel = sel & (key > KEY_NEG_INF) & (key < KEY_POS_INF)
        for h in range(N_HEADS):
            _row_softmax_step(pl.ds(h * tq, tq), s[h * tq:(h + 1) * tq], sel, kvc, m_ref, l_ref, acc_ref)
        return carry

    lax.fori_loop(0, n_chunks, attend, 0)
    return _unstack_heads([_row_softmax_result(pl.ds(h * tq, tq), l_ref, acc_ref) for h in range(N_HEADS)])


def _dsa_decode_kernel(pt_ref, q_ref, iq_ref, iw_ref, *rest, tq, kc, n_top, idx_bits, n_pages, page, pps):
    page_refs, (new_ref, o_ref, kv_ref, ik_ref, key_ref, m_ref, l_ref, acc_ref) = rest[:pps], rest[pps:]
    s = pl.program_id(1)
    n_steps = n_pages // pps

    def put(at, rows):
        kv_ref[at, :] = rows[:, 0:LANES].astype(BF16)
        ik = rows[:, LANES:]
        ik_ref[at, :] = jnp.concatenate([ik, jnp.zeros((rows.shape[0], 2 * LANES - rows.shape[1]), F32)],
                                        axis=1).astype(BF16)

    @pl.when(s < n_steps)
    def _():
        for r in range(pps):
            put(pl.ds(pl.multiple_of((s * pps + r) * page, page), page), page_refs[r][0, 0])

    @pl.when(s == n_steps)
    def _():
        put(pl.ds(n_pages * page, kc), new_ref[0])
        o_ref[0] = _dsa_rows_core(q_ref[0], iq_ref[0], iw_ref[0], _chunk_loader(kv_ref, kc), _chunk_loader(ik_ref, kc),
                                  key_ref, m_ref, l_ref, acc_ref, tq=tq, kc=kc, n_top=n_top, p0=n_pages * page,
                                  idx_bits=idx_bits)


def _dsa_scratch(n_kc, tq, kc):
    return ([pltpu.VMEM((n_kc, LANES, kc), BF16), pltpu.VMEM((n_kc, kc, tq), jnp.int32)]
            + _tsoftmax_scratch(LANES, N_HEADS * tq))


def _page_specs(cache, layer, n_pages, pps):
    block = (1, 1) + cache.shape[2:]
    zeros = (0,) * (len(cache.shape) - 2)

    def spec(r):
        return pl.BlockSpec(block, lambda bi, s, pt: (layer, pt[bi, jnp.minimum(s * pps + r, n_pages - 1)]) + zeros)

    return [spec(r) for r in range(pps)]


def _dsa_decode_call(zq, cache, layer, page_table, new_rows, *, kc=DEC_KEY_CHUNK, pps=PAGES_PER_STEP):
    b, tq, _ = zq.shape
    n_pages = page_table.shape[1]
    page = cache.shape[2]
    past = n_pages * page
    assert past % kc == 0 and n_pages % pps == 0
    l_rows = past + kc
    n_kc = l_rows // kc
    kern = functools.partial(_dsa_decode_kernel, tq=tq, kc=kc, n_top=min(DSA_TOPK, (past + 1) // 4),
                             idx_bits=max(1, math.ceil(math.log2(l_rows))), n_pages=n_pages, page=page, pps=pps)
    grid_spec = pltpu.PrefetchScalarGridSpec(
        num_scalar_prefetch=1,
        grid=(b, n_pages // pps + 1),
        in_specs=[pl.BlockSpec((1, tq, 2 * LANES), lambda bi, s, pt: (bi, 0, _NEW["q_a"] // 256)),
                  pl.BlockSpec((1, tq, 2 * LANES), lambda bi, s, pt: (bi, 0, _NEW["iq"] // 256)),
                  pl.BlockSpec((1, tq, LANES), lambda bi, s, pt: (bi, 0, BLK_IK))]
        + _page_specs(cache, layer, n_pages, pps)
        + [pl.BlockSpec((1,) + new_rows.shape[1:], lambda bi, s, pt: (bi, 0, 0))],
        out_specs=pl.BlockSpec((1, tq, 2 * LANES), lambda bi, s, pt: (bi, 0, 0)),
        scratch_shapes=[pltpu.VMEM((l_rows, LANES), BF16), pltpu.VMEM((l_rows, LANES), BF16),
                        pltpu.VMEM((n_kc, tq, kc), jnp.int32)] + _row_softmax_scratch(N_HEADS * tq, LANES))
    return pl.pallas_call(
        kern, grid_spec=grid_spec,
        out_shape=jax.ShapeDtypeStruct((b, tq, 2 * LANES), F32),
        compiler_params=_params(("parallel", "arbitrary")),
        name="dsa_decode",
    )(page_table, zq, zq, zq, *([cache] * pps), new_rows)


def _dsa_call(zq, kv, ik, *, kv_blk, ik_blk, tq, n_keys, kc=KEY_CHUNK):
    b, t_q, _ = zq.shape
    l_rows = kv.shape[1]
    n_kc = l_rows // kc
    kern = functools.partial(_dsa_kernel, tq=tq, kc=kc, n_top=min(DSA_TOPK, n_keys // 4),
                             idx_bits=max(1, math.ceil(math.log2(l_rows))))
    return pl.pallas_call(
        kern,
        grid=(b, t_q // tq),
        in_specs=[pl.BlockSpec((1, tq, 2 * LANES), lambda bi, i: (bi, i, _NEW["q_a"] // 256)),
                  pl.BlockSpec((1, tq, 2 * LANES), lambda bi, i: (bi, i, _NEW["iq"] // 256)),
                  pl.BlockSpec((1, tq, LANES), lambda bi, i: (bi, i, BLK_IK)),
                  pl.BlockSpec((1, l_rows, LANES), lambda bi, i: (bi, 0, kv_blk)),
                  pl.BlockSpec((1, l_rows, LANES), lambda bi, i: (bi, 0, ik_blk))],
        out_specs=pl.BlockSpec((1, tq, 2 * LANES), lambda bi, i: (bi, i, 0)),
        out_shape=jax.ShapeDtypeStruct((b, t_q, 2 * LANES), F32),
        scratch_shapes=_dsa_scratch(n_kc, tq, kc),
        compiler_params=_params(("parallel", "arbitrary")),
        name="dsa_attention",
    )(zq, zq, zq, kv, ik)


def _compress(load_rows, wlo_ref, whi_ref, pe_ref, w2_ref, n_blocks):
    a = jnp.zeros((n_blocks, LANES), F32)
    b = jnp.zeros((n_blocks, LANES), F32)
    for p in range(NSA_CMP_STRIDE):
        xp = load_rows(p)
        a = a + _dot((xp + pe_ref[p:p + 1, :]).astype(BF16), wlo_ref[p])
        b = b + _dot((xp + pe_ref[NSA_CMP_STRIDE + p:NSA_CMP_STRIDE + p + 1, :]).astype(BF16), whi_ref[p])
    h = jax.nn.gelu(a + pltpu.roll(b, n_blocks - 1, 0))
    return _dot(h.astype(BF16), w2_ref[...])


def _cmp_kernel(rows_ref, wlo_ref, whi_ref, pe_ref, w2_ref, o_ref, *, n_blocks):
    o_ref[0] = _compress(lambda p: rows_ref[0, pl.ds(p, n_blocks, stride=NSA_CMP_STRIDE), :],
                         wlo_ref, whi_ref, pe_ref, w2_ref, n_blocks)


def _block_diag2(m0, m1):
    z = jnp.zeros_like(m0)
    return jnp.concatenate([jnp.concatenate([m0, z], axis=-1), jnp.concatenate([z, m1], axis=-1)], axis=-2)


def _compress_weights(w_cmp1, w_cmp2, pe_cmp):
    w1 = w_cmp1.reshape(2, NSA_CMP_LEN, HEAD_DIM, HEAD_DIM)
    wlo = _block_diag2(w1[0, :NSA_CMP_STRIDE], w1[1, :NSA_CMP_STRIDE]).astype(BF16)
    whi = _block_diag2(w1[0, NSA_CMP_STRIDE:], w1[1, NSA_CMP_STRIDE:]).astype(BF16)
    pe = jnp.concatenate([pe_cmp[0], pe_cmp[1]], axis=-1)
    w2 = _block_diag2(w_cmp2[0], w_cmp2[1]).astype(BF16)
    return wlo, whi, pe, w2


def _compress_specs(index_map3, index_map2):
    return [pl.BlockSpec((NSA_CMP_STRIDE, LANES, LANES), index_map3),
            pl.BlockSpec((NSA_CMP_STRIDE, LANES, LANES), index_map3),
            pl.BlockSpec((NSA_CMP_LEN, LANES), index_map2),
            pl.BlockSpec((LANES, LANES), index_map2)]


def _nsa_compress(rows, cmp_w, *, blk):
    b, l_rows, _ = rows.shape
    n_blocks = l_rows // NSA_CMP_STRIDE
    return pl.pallas_call(
        functools.partial(_cmp_kernel, n_blocks=n_blocks),
        grid=(b,),
        in_specs=[pl.BlockSpec((1, l_rows, LANES), lambda bi: (bi, 0, blk))]
        + _compress_specs(lambda bi: (0, 0, 0), lambda bi: (0, 0)),
        out_specs=pl.BlockSpec((1, n_blocks, LANES), lambda bi: (bi, 0, 0)),
        out_shape=jax.ShapeDtypeStruct((b, n_blocks, LANES), F32),
        compiler_params=_params(("parallel",)),
        name="nsa_compress",
    )(rows, *cmp_w)


def _nsa_core(q, g, cmp_rows, load_sel, load_selt, wrows, wpos, rank_ref, m_ref, l_ref, acc_ref,
              *, tq, kc, p0, n_cmp, n_blk, n_sel):
    pos = p0 + _lane_iota(tq)
    qs_t = (_stack_heads(q) * SCALE).T.astype(BF16)

    ncp = cmp_rows.shape[0]
    cmp_t = cmp_rows.T.astype(BF16)
    s_c = _dot(cmp_rows.astype(BF16), qs_t)
    n_idx = _row_iota(ncp)
    valid_c = (n_idx * NSA_CMP_STRIDE + (NSA_CMP_LEN - 1) <= pos) & (n_idx < n_cmp)
    p_c = [_tmasked_softmax(s_c[:, h * tq:(h + 1) * tq], valid_c) for h in range(N_HEADS)]
    o_c = _dot(cmp_t, jnp.concatenate(p_c, axis=1).astype(BF16))
    p_sum = p_c[0] + p_c[1] + p_c[2] + p_c[3]

    nbp = rank_ref.shape[0]
    jb = lax.broadcasted_iota(jnp.int32, (nbp, ncp), 0)
    nn = lax.broadcasted_iota(jnp.int32, (nbp, ncp), 1)
    cover_t = ((nn * NSA_CMP_STRIDE < (jb + 1) * NSA_SEL_BLOCK)
               & (nn * NSA_CMP_STRIDE + (NSA_CMP_LEN - 1) >= jb * NSA_SEL_BLOCK) & (nn < n_cmp)).astype(BF16)
    hi, lo = _split_bf16(p_sum)
    imp = _dot(cover_t, hi) + _dot(cover_t, lo)
    j = _row_iota(nbp)
    cur = pos // NSA_SEL_BLOCK
    forced = (j == 0) | (j == cur) | (j == cur - 1)
    imp = jnp.where(forced, jnp.inf, jnp.where(j <= cur, imp, NEG_INF))
    n_vis = jnp.minimum((p0 + tq - 1) // NSA_SEL_BLOCK + 1, n_blk)
    sel = _select_top_t(imp, rank_ref, n_vis, n_sel).astype(BF16)

    _tsoftmax_init(m_ref, l_ref, acc_ref)
    n_chunks = (p0 + tq - 1) // kc + 1
    ek = lax.broadcasted_iota(jnp.int32, (kc, nbp), 0)
    eb = lax.broadcasted_iota(jnp.int32, (kc, nbp), 1)

    def attend(c, carry):
        s = _dot(load_sel(c), qs_t)
        expand = (eb == (c * kc + ek) // NSA_SEL_BLOCK).astype(BF16)
        picked = _dot(expand, sel)
        valid = (picked > 0.5) & (c * kc + _row_iota(kc) <= pos)
        s = jnp.concatenate([jnp.where(valid, s[:, h * tq:(h + 1) * tq], NEG_INF) for h in range(N_HEADS)], axis=1)
        _tsoftmax_step(s, load_selt(c), m_ref, l_ref, acc_ref)
        return carry

    lax.fori_loop(0, n_chunks, attend, 0)
    o_s = _tsoftmax_result(l_ref, acc_ref)

    s_w = _dot(wrows.astype(BF16), qs_t)
    valid_w = (wpos <= pos) & (pos - wpos < NSA_WINDOW)
    p_w = [_tmasked_softmax(s_w[:, h * tq:(h + 1) * tq], valid_w) for h in range(N_HEADS)]
    o_w = _dot(wrows.T.astype(BF16), jnp.concatenate(p_w, axis=1).astype(BF16))

    g_t = _rows_t(g)
    mixed = []
    for h in range(N_HEADS):
        cols = slice(h * tq, (h + 1) * tq)
        mixed.append(g_t[3 * h:3 * h + 1] * o_c[:, cols] + g_t[3 * h + 1:3 * h + 2] * o_s[:, cols]
                     + g_t[3 * h + 2:3 * h + 3] * o_w[:, cols])
    return _unstack_heads_t(jnp.concatenate(mixed, axis=1), tq)


def _nsa_kernel(q_ref, g_ref, cmp_ref, sel_ref, win_ref, o_ref, selt_ref, rank_ref, m_ref, l_ref, acc_ref,
                *, tq, kc, n_cmp, n_blk, n_sel, win_rows):
    i = pl.program_id(1)

    @pl.when(i == 0)
    def _():
        _fill_transposed(sel_ref, selt_ref, kc)

    p0 = i * tq
    start = pl.multiple_of(jnp.clip(p0 - NSA_WINDOW, 0, win_ref.shape[1] - win_rows), SUBLANES)
    o_ref[0] = _nsa_core(q_ref[0], g_ref[0], cmp_ref[0], _chunk_loader(sel_ref, kc), lambda c: selt_ref[c],
                         win_ref[0, pl.ds(start, win_rows), :], start + _row_iota(win_rows),
                         rank_ref, m_ref, l_ref, acc_ref,
                         tq=tq, kc=kc, p0=p0, n_cmp=n_cmp, n_blk=n_blk, n_sel=n_sel)


def _top_mask_row(v, n_sel):
    n = v.shape[1]
    i = lax.broadcasted_iota(jnp.int32, (n, n), 0)
    j = lax.broadcasted_iota(jnp.int32, (n, n), 1)
    v_col = jnp.sum(jnp.where(i == j, v, 0.0), axis=1, keepdims=True)
    beats = (v_col > v) | ((v_col == v) & (i < j))
    rank = jnp.sum(jnp.where(beats, 1, 0), axis=0, keepdims=True)
    return jnp.where((rank < n_sel) & (v > NEG_INF), 1.0, 0.0)


def _nsa_rows_core(q, g, cmpv, load_sel, wrows, wpos, m_ref, l_ref, acc_ref, *, tq, kc, p0, n_cmp, n_blk, n_sel):
    pos = p0 + _row_iota(tq)
    qs = (_stack_heads(q) * SCALE).astype(BF16)

    ncp = cmpv.shape[0]
    s_c = _dot_nt(qs, cmpv)
    n_idx = _lane_iota(ncp)
    valid_c = (n_idx * NSA_CMP_STRIDE + (NSA_CMP_LEN - 1) <= pos) & (n_idx < n_cmp)
    o_c = []
    p_sum = jnp.zeros((tq, ncp), F32)
    for h in range(N_HEADS):
        p = _row_masked_softmax(s_c[h * tq:(h + 1) * tq], valid_c)
        p_sum = p_sum + p
        o_c.append(_dot(p.astype(BF16), cmpv))

    nbp = -(-n_blk // LANES) * LANES
    nn = lax.broadcasted_iota(jnp.int32, (ncp, nbp), 0)
    jb = lax.broadcasted_iota(jnp.int32, (ncp, nbp), 1)
    cover = ((nn * NSA_CMP_STRIDE < (jb + 1) * NSA_SEL_BLOCK)
             & (nn * NSA_CMP_STRIDE + (NSA_CMP_LEN - 1) >= jb * NSA_SEL_BLOCK) & (nn < n_cmp)).astype(BF16)
    hi, lo = _split_bf16(p_sum)
    imp = _dot(hi, cover) + _dot(lo, cover)
    j = _lane_iota(nbp)
    cur = pos // NSA_SEL_BLOCK
    forced = (j == 0) | (j == cur) | (j == cur - 1)
    imp = jnp.where(forced, jnp.inf, jnp.where(j <= cur, imp, NEG_INF))
    sel = jnp.broadcast_to(_top_mask_row(imp[0:1], n_sel), (tq, nbp)).astype(BF16)

    _row_softmax_init(m_ref, l_ref, acc_ref)
    n_chunks = (p0 + tq - 1) // kc + 1
    eb = lax.broadcasted_iota(jnp.int32, (nbp, kc), 0)
    ek = lax.broadcasted_iota(jnp.int32, (nbp, kc), 1)

    def attend(c, carry):
        rows = load_sel(c)
        s = _dot_nt(qs, rows)
        expand = (eb == (c * kc + ek) // NSA_SEL_BLOCK).astype(BF16)
        picked = _dot(sel, expand)
        valid = (picked > 0.5) & (c * kc + _lane_iota(kc) <= pos)
        for h in range(N_HEADS):
            _row_softmax_step(pl.ds(h * tq, tq), s[h * tq:(h + 1) * tq], valid, rows, m_ref, l_ref, acc_ref)
        return carry

    lax.fori_loop(0, n_chunks, attend, 0)

    s_w = _dot_nt(qs, wrows)
    valid_w = (wpos <= pos) & (pos - wpos < NSA_WINDOW)
    heads = []
    for h in range(N_HEADS):
        p_w = _row_masked_softmax(s_w[h * tq:(h + 1) * tq], valid_w)
        o_w = _dot(p_w.astype(BF16), wrows)
        o_s = _row_softmax_result(pl.ds(h * tq, tq), l_ref, acc_ref)
        heads.append(g[:, 3 * h:3 * h + 1] * o_c[h] + g[:, 3 * h + 1:3 * h + 2] * o_s
                     + g[:, 3 * h + 2:3 * h + 3] * o_w)
    return _unstack_heads(heads)


def _nsa_decode_kernel(pt_ref, q_ref, g_ref, *rest, tq, kc, n_cmp, n_blk, n_sel, n_pages, page, pps):
    page_refs = rest[:pps]
    (new_ref, win_ref, wlo_ref, whi_ref, pe_ref, w2_ref, o_ref, cbuf_ref, sbuf_ref, m_ref, l_ref, acc_ref) = rest[pps:]
    s = pl.program_id(1)
    n_steps = n_pages // pps

    @pl.when(s < n_steps)
    def _():
        for r in range(pps):
            at = pl.ds(pl.multiple_of((s * pps + r) * page, page), page)
            cbuf_ref[at, :] = page_refs[r][0, 0, :, 0:LANES]
            sbuf_ref[at, :] = page_refs[r][0, 0, :, LANES:2 * LANES].astype(BF16)

    @pl.when(s == n_steps)
    def _():
        past = n_pages * page
        sbuf_ref[pl.ds(past, kc), :] = new_ref[0, :, LANES:2 * LANES].astype(BF16)
        n_blocks = past // NSA_CMP_STRIDE
        cmpv = _compress(lambda r: cbuf_ref[pl.ds(r, n_blocks, stride=NSA_CMP_STRIDE), :],
                         wlo_ref, whi_ref, pe_ref, w2_ref, n_blocks).astype(BF16)
        wrows = win_ref[0].astype(BF16)
        o_ref[0] = _nsa_rows_core(q_ref[0], g_ref[0], cmpv, _chunk_loader(sbuf_ref, kc), wrows,
                                  past - NSA_WINDOW + _lane_iota(wrows.shape[0]), m_ref, l_ref, acc_ref,
                                  tq=tq, kc=kc, p0=past, n_cmp=n_cmp, n_blk=n_blk, n_sel=n_sel)


def _nsa_sizes(n_keys):
    n_cmp = (n_keys - NSA_CMP_LEN) // NSA_CMP_STRIDE + 1
    n_blk = -(-n_keys // NSA_SEL_BLOCK)
    return n_cmp, n_blk, min(NSA_SEL_TOPN, n_blk), -(-n_blk // LANES) * LANES


def _nsa_scratch(n_kc, nbp, tq, kc):
    return ([pltpu.VMEM((n_kc, LANES, kc), BF16), pltpu.VMEM((nbp, tq), F32)]
            + _tsoftmax_scratch(LANES, N_HEADS * tq))


def _nsa_decode_call(zq, cache, layer, page_table, new_rows, win_rows, cmp_w, *, kc=DEC_KEY_CHUNK,
                     pps=PAGES_PER_STEP):
    b, tq, _ = zq.shape
    n_pages = page_table.shape[1]
    page = cache.shape[2]
    past = n_pages * page
    assert past % kc == 0 and n_pages % pps == 0
    n_cmp, n_blk, n_sel, _ = _nsa_sizes(past + 1)
    kern = functools.partial(_nsa_decode_kernel, tq=tq, kc=kc, n_cmp=n_cmp, n_blk=n_blk, n_sel=n_sel,
                             n_pages=n_pages, page=page, pps=pps)
    grid_spec = pltpu.PrefetchScalarGridSpec(
        num_scalar_prefetch=1,
        grid=(b, n_pages // pps + 1),
        in_specs=[pl.BlockSpec((1, tq, 2 * LANES), lambda bi, s, pt: (bi, 0, _NEW["q_b"] // 256)),
                  pl.BlockSpec((1, tq, LANES), lambda bi, s, pt: (bi, 0, BLK_GB))]
        + _page_specs(cache, layer, n_pages, pps)
        + [pl.BlockSpec((1,) + new_rows.shape[1:], lambda bi, s, pt: (bi, 0, 0)),
           pl.BlockSpec((1,) + win_rows.shape[1:], lambda bi, s, pt: (bi, 0, 0))]
        + _compress_specs(lambda bi, s, pt: (0, 0, 0), lambda bi, s, pt: (0, 0)),
        out_specs=pl.BlockSpec((1, tq, 2 * LANES), lambda bi, s, pt: (bi, 0, 0)),
        scratch_shapes=[pltpu.VMEM((past, LANES), F32), pltpu.VMEM((past + kc, LANES), BF16)]
        + _row_softmax_scratch(N_HEADS * tq, LANES))
    return pl.pallas_call(
        kern, grid_spec=grid_spec,
        out_shape=jax.ShapeDtypeStruct((b, tq, 2 * LANES), F32),
        compiler_params=_params(("parallel", "arbitrary")),
        name="nsa_decode",
    )(page_table, zq, zq, *([cache] * pps), new_rows, win_rows, *cmp_w)


def _nsa_call(zq, cmp, sel, win, *, sel_blk, win_blk, tq, n_keys, kc=KEY_CHUNK):
    b, t_q, _ = zq.shape
    l_rows = sel.shape[1]
    n_cmp, n_blk, n_sel, nbp = _nsa_sizes(n_keys)
    kern = functools.partial(_nsa_kernel, tq=tq, kc=kc, n_cmp=n_cmp, n_blk=n_blk, n_sel=n_sel,
                             win_rows=min(NSA_WINDOW + tq, l_rows))
    return pl.pallas_call(
        kern,
        grid=(b, t_q // tq),
        in_specs=[pl.BlockSpec((1, tq, 2 * LANES), lambda bi, i: (bi, i, _NEW["q_b"] // 256)),
                  pl.BlockSpec((1, tq, LANES), lambda bi, i: (bi, i, BLK_GB)),
                  pl.BlockSpec((1, cmp.shape[1], LANES), lambda bi, i: (bi, 0, 0)),
                  pl.BlockSpec((1, l_rows, LANES), lambda bi, i: (bi, 0, sel_blk)),
                  pl.BlockSpec((1, l_rows, LANES), lambda bi, i: (bi, 0, win_blk))],
        out_specs=pl.BlockSpec((1, tq, 2 * LANES), lambda bi, i: (bi, i, 0)),
        out_shape=jax.ShapeDtypeStruct((b, t_q, 2 * LANES), F32),
        scratch_shapes=_nsa_scratch(l_rows // kc, nbp, tq, kc),
        compiler_params=_params(("parallel", "arbitrary")),
        name="nsa_attention",
    )(zq, zq, cmp, sel, win)


def _moba_kernel(q_ref, k_ref, v_ref, o_ref, kmean_ref, vt_ref, rank_ref, sel_ref, m_ref, l_ref, acc_ref,
                 *, tq, n_blocks, n_sel):
    c0 = pl.program_id(2)

    @pl.when(c0 == 0)
    def _():
        kmean_ref[...] = jnp.zeros(kmean_ref.shape, F32)

        def mean_block(blk, carry):
            rows = k_ref[0, pl.ds(pl.multiple_of(blk * MOBA_BLOCK, MOBA_BLOCK), MOBA_BLOCK), :]
            kmean_ref[pl.ds(blk, 1), :] = jnp.sum(rows, axis=0, keepdims=True) * (1.0 / MOBA_BLOCK)
            return carry

        lax.fori_loop(0, n_blocks, mean_block, 0)
        _fill_transposed(v_ref, vt_ref, MOBA_BLOCK)

    nbp = kmean_ref.shape[0]
    qst = _pair_heads(q_ref[0])
    qst_t = qst.T
    q_hi, q_lo = _split_bf16(qst_t)
    k_hi, k_lo = _split_bf16(kmean_ref[...])
    gate = _dot(k_hi, q_hi) + _dot(k_hi, q_lo) + _dot(k_lo, q_hi)
    gate = jnp.where(_row_iota(nbp) < c0, gate, NEG_INF)
    sel_ref[...] = _select_top_t(gate, rank_ref, c0, n_sel)
    qs_t = (qst_t * SCALE).astype(BF16)
    load_k = _chunk_loader(k_ref, MOBA_BLOCK)

    _tsoftmax_init(m_ref, l_ref, acc_ref)

    def attend(blk, carry):
        s = jnp.where(sel_ref[pl.ds(blk, 1), :] > 0.5, _dot(load_k(blk), qs_t), NEG_INF)
        _tsoftmax_step(s, vt_ref[blk], m_ref, l_ref, acc_ref)
        return carry

    lax.fori_loop(0, c0, attend, 0)

    ki = lax.broadcasted_iota(jnp.int32, (MOBA_BLOCK, tq), 0)
    qi = lax.broadcasted_iota(jnp.int32, (MOBA_BLOCK, tq), 1)
    s_own = _dot(load_k(c0), qs_t)
    causal = ki <= qi
    s_own = jnp.concatenate([jnp.where(causal, s_own[:, 0:tq], NEG_INF),
                             jnp.where(causal, s_own[:, tq:2 * tq], NEG_INF)], axis=1)
    _tsoftmax_step(s_own, vt_ref[c0], m_ref, l_ref, acc_ref)
    o = _tsoftmax_result(l_ref, acc_ref).T
    o_ref[0] = jnp.where(_lane_iota() < HEAD_DIM, o[0:tq], o[tq:2 * tq])


def _moba_call(zq, kv, *, tq):
    b, t_q, _ = zq.shape
    l_rows = kv.shape[1]
    assert tq == MOBA_BLOCK
    n_blocks = l_rows // MOBA_BLOCK
    nbp = -(-n_blocks // LANES) * LANES
    kern = functools.partial(_moba_kernel, tq=tq, n_blocks=n_blocks, n_sel=min(MOBA_TOPK, n_blocks))
    q_blk0 = _NEW["q_c"] // LANES
    return pl.pallas_call(
        kern,
        grid=(b, 2, t_q // tq),
        in_specs=[pl.BlockSpec((1, tq, LANES), lambda bi, c, i: (bi, i, q_blk0 + c)),
                  pl.BlockSpec((1, l_rows, LANES), lambda bi, c, i: (bi, 0, BLK_KC + c)),
                  pl.BlockSpec((1, l_rows, LANES), lambda bi, c, i: (bi, 0, BLK_VC + c))],
        out_specs=pl.BlockSpec((1, tq, LANES), lambda bi, c, i: (bi, i, c)),
        out_shape=jax.ShapeDtypeStruct((b, t_q, 2 * LANES), F32),
        scratch_shapes=[pltpu.VMEM((nbp, LANES), F32), pltpu.VMEM((n_blocks, LANES, MOBA_BLOCK), BF16),
                        pltpu.VMEM((nbp, 2 * tq), F32), pltpu.VMEM((nbp, 2 * tq), F32)]
        + _tsoftmax_scratch(LANES, 2 * tq),
        compiler_params=_params(("parallel", "parallel", "arbitrary")),
        name="moba_attention",
    )(zq, kv, kv)


def _moba_pick_kernel(pt_ref, q_ref, *rest, n_blocks, n_sel, n_pages, page, pps):
    page_refs, (o_ref, kmean_ref, rank_ref) = rest[:pps], rest[pps:]
    s = pl.program_id(1)
    per_block = MOBA_BLOCK // page
    n_steps = n_pages // pps
    nbp = kmean_ref.shape[0]

    @pl.when(s == 0)
    def _():
        kmean_ref[...] = jnp.zeros(kmean_ref.shape, F32)

    @pl.when(s < n_steps)
    def _():
        for j in range(pps // per_block):
            tot = None
            for r in range(per_block):
                part = jnp.sum(page_refs[j * per_block + r][0, 0, :, 0], axis=0)
                tot = part if tot is None else tot + part
            row = jnp.concatenate([tot[h:h + 1, :] for h in range(N_HEADS)], axis=1)
            kmean_ref[pl.ds(s * (pps // per_block) + j, 1), :] = row * (1.0 / MOBA_BLOCK)

    @pl.when(s == n_steps)
    def _():
        prod = kmean_ref[...] * q_ref[0, 0:1, :]
        seg = (lax.broadcasted_iota(jnp.int32, (N_HEADS * HEAD_DIM, LANES), 0) // HEAD_DIM
               == lax.broadcasted_iota(jnp.int32, (N_HEADS * HEAD_DIM, LANES), 1)).astype(BF16)
        hi, lo = _split_bf16(prod)
        lo2 = (prod - hi.astype(F32) - lo.astype(F32)).astype(BF16)
        gate = _dot(hi, seg) + _dot(lo, seg) + _dot(lo2, seg)
        j = _row_iota(nbp)
        gate = jnp.where(j < n_blocks, gate, NEG_INF)
        rank = _rank_rows(gate, rank_ref, n_blocks)
        rows = []
        for r in range(n_sel):
            hit = (rank == r) & (gate > NEG_INF)
            rows.append(jnp.sum(jnp.where(hit, j, 0), axis=0, keepdims=True))
        for r in range(n_sel):
            hit = (rank == r) & (gate > NEG_INF)
            rows.append(jnp.sum(jnp.where(hit, 1, 0), axis=0, keepdims=True))
        rows.append(jnp.zeros((SUBLANES - 2 * n_sel, LANES), jnp.int32))
        o_ref[0] = jnp.concatenate(rows, axis=0)


def _moba_pick_call(zq, cache, layer, page_table, *, pps=PAGES_PER_STEP):
    b, tq, _ = zq.shape
    n_pages = page_table.shape[1]
    page = cache.shape[2]
    n_blocks = n_pages * page // MOBA_BLOCK
    n_sel = min(MOBA_TOPK, n_blocks + 1)
    assert MOBA_BLOCK % page == 0 and n_pages % pps == 0 and pps % (MOBA_BLOCK // page) == 0
    assert 2 * n_sel <= SUBLANES
    nbp = -(-n_blocks // LANES) * LANES
    block = (1, 1, page, 1) + cache.shape[4:]

    def spec(r):
        return pl.BlockSpec(block, lambda bi, s, pt: (layer, pt[bi, jnp.minimum(s * pps + r, n_pages - 1)],
                                                      0, 0, 0, 0))

    grid_spec = pltpu.PrefetchScalarGridSpec(
        num_scalar_prefetch=1,
        grid=(b, n_pages // pps + 1),
        in_specs=[pl.BlockSpec((1, tq, 2 * LANES), lambda bi, s, pt: (bi, 0, _NEW["q_c"] // 256))]
        + [spec(r) for r in range(pps)],
        out_specs=pl.BlockSpec((1, SUBLANES, LANES), lambda bi, s, pt: (bi, 0, 0)),
        scratch_shapes=[pltpu.VMEM((nbp, N_HEADS * HEAD_DIM), F32), pltpu.VMEM((nbp, LANES), F32)])
    kern = functools.partial(_moba_pick_kernel, n_blocks=n_blocks, n_sel=n_sel, n_pages=n_pages, page=page, pps=pps)
    return pl.pallas_call(
        kern, grid_spec=grid_spec,
        out_shape=jax.ShapeDtypeStruct((b, SUBLANES, LANES), jnp.int32),
        compiler_params=_params(("parallel", "arbitrary")),
        name="moba_pick",
    )(page_table, zq, *([cache] * pps)), n_sel


def _moba_gather_kernel(pid_ref, ok_ref, q_ref, *rest, tq, n_sel, per_block, page):
    n_pg = N_HEADS * per_block
    k_refs, v_refs = rest[:n_pg], rest[n_pg:2 * n_pg]
    new_ref, o_ref, m_ref, l_ref, acc_ref = rest[2 * n_pg:]
    bi, r = pl.program_id(0), pl.program_id(1)

    @pl.when(r == 0)
    def _():
        m_ref[...] = jnp.full(m_ref.shape, NEG_INF, F32)
        l_ref[...] = jnp.zeros(l_ref.shape, F32)
        acc_ref[...] = jnp.zeros(acc_ref.shape, F32)

    qs = _stack_heads(q_ref[0]) * SCALE
    for h in range(N_HEADS):
        rows = pl.ds(h * tq, tq)
        qh = qs[h * tq:(h + 1) * tq, 0:HEAD_DIM].astype(BF16)
        kb = jnp.concatenate([k_refs[h * per_block + j][0, 0, :, 0, h, :] for j in range(per_block)], axis=0)
        vb = jnp.concatenate([v_refs[h * per_block + j][0, 0, :, 0, h, :] for j in range(per_block)], axis=0)
        ok = ok_ref[bi, r * N_HEADS + h] > 0
        _row_softmax_step(rows, _dot_nt(qh, kb.astype(BF16)), ok, vb.astype(BF16), m_ref, l_ref, acc_ref)

    @pl.when(r == n_sel - 1)
    def _():
        outs = []
        for h in range(N_HEADS):
            rows = pl.ds(h * tq, tq)
            qh = qs[h * tq:(h + 1) * tq, 0:HEAD_DIM].astype(BF16)
            kb = new_ref[0, 0, :, 0, h, :].astype(BF16)
            vb = new_ref[0, 0, :, 1, h, :].astype(BF16)
            ki = lax.broadcasted_iota(jnp.int32, (tq, page), 1)
            qi = lax.broadcasted_iota(jnp.int32, (tq, page), 0)
            _row_softmax_step(rows, _dot_nt(qh, kb), ki <= qi, vb, m_ref, l_ref, acc_ref)
            outs.append(acc_ref[rows, :] / jnp.maximum(l_ref[rows, :], 1e-30))
        o_ref[0] = jnp.concatenate(outs, axis=1)


def _moba_decode_call(zq, cache, layer, page_table, new_page):
    b, tq, _ = zq.shape
    page = cache.shape[2]
    per_block = MOBA_BLOCK // page
    picks, n_sel = _moba_pick_call(zq, cache, layer, page_table)
    blk = picks[:, :n_sel, :N_HEADS]
    ok = picks[:, n_sel:2 * n_sel, :N_HEADS].reshape(b, n_sel * N_HEADS)
    logical = blk[..., None] * per_block + jnp.arange(per_block, dtype=jnp.int32)
    pid = jnp.take_along_axis(page_table, logical.reshape(b, -1), axis=1)
    block = (1, 1, page, 1) + cache.shape[4:]

    def spec(h, j, field):
        return pl.BlockSpec(block, lambda bi, r, pid_ref, ok_ref:
                            (layer, pid_ref[bi, (r * N_HEADS + h) * per_block + j], 0, field, 0, 0))

    hj = [(h, j) for h in range(N_HEADS) for j in range(per_block)]
    grid_spec = pltpu.PrefetchScalarGridSpec(
        num_scalar_prefetch=2,
        grid=(b, n_sel),
        in_specs=[pl.BlockSpec((1, tq, 2 * LANES), lambda bi, r, pid_ref, ok_ref: (bi, 0, _NEW["q_c"] // 256))]
        + [spec(h, j, 0) for h, j in hj] + [spec(h, j, 1) for h, j in hj]
        + [pl.BlockSpec((1, 1) + new_page.shape[1:], lambda bi, r, pid_ref, ok_ref: (0, bi, 0, 0, 0, 0))],
        out_specs=pl.BlockSpec((1, tq, 2 * LANES), lambda bi, r, pid_ref, ok_ref: (bi, 0, 0)),
        scratch_shapes=[pltpu.VMEM((N_HEADS * tq, 1), F32), pltpu.VMEM((N_HEADS * tq, 1), F32),
                        pltpu.VMEM((N_HEADS * tq, HEAD_DIM), F32)])
    kern = functools.partial(_moba_gather_kernel, tq=tq, n_sel=n_sel, per_block=per_block, page=page)
    return pl.pallas_call(
        kern, grid_spec=grid_spec,
        out_shape=jax.ShapeDtypeStruct((b, tq, 2 * LANES), F32),
        compiler_params=_params(("parallel", "arbitrary")),
        name="moba_decode",
    )(pid, ok, zq, *([cache] * (2 * len(hj))), new_page[None])


def _memory_kv(mem, g_ln, w_kv, g_k):
    b, m_rows, d = mem.shape
    n = w_kv.shape[1]
    half = n // 2
    mask = np.zeros((n,), np.float32)
    mask[:half] = 1
    zero = jnp.zeros((n,), F32)
    gain = jnp.concatenate([jnp.tile(g_k.astype(F32), half // HEAD_DIM), jnp.ones((half,), F32)])
    cfg = jnp.stack([jnp.asarray(mask), gain, zero, zero, zero, zero, zero, zero])
    masks = dict(norm=mask, rope=np.zeros_like(mask), sig=np.zeros_like(mask))
    rope = jnp.zeros((m_rows, 3 * LANES), F32)
    out = _project(mem.reshape(b * m_rows, d), g_ln.reshape(1, d), w_kv.astype(BF16), cfg, rope, masks,
                   tm=m_rows, tn=n)
    return out.reshape(b, m_rows, n)


def _mem_kernel(q_ref, kv_ref, o_ref, *, tq):
    lane = _lane_iota()
    n_kv = kv_ref.shape[2] // 2
    chunks = []
    for c in range(2):
        qst = _pair_heads(q_ref[0, :, c * LANES:(c + 1) * LANES])
        kb = kv_ref[0, :, c * LANES:(c + 1) * LANES].astype(BF16)
        vb = kv_ref[0, :, n_kv + c * LANES:n_kv + (c + 1) * LANES].astype(BF16)
        s = _dot_nt((qst * SCALE).astype(BF16), kb)
        e = jnp.exp(s - jnp.max(s, axis=-1, keepdims=True))
        p = e / jnp.sum(e, axis=-1, keepdims=True)
        o = _dot(p.astype(BF16), vb)
        chunks.append(jnp.where(lane < HEAD_DIM, o[0:tq], o[tq:2 * tq]))
    o_ref[0] = jnp.concatenate(chunks, axis=1)


def _mem_call(zq, mkv, *, tq):
    b, t_q, _ = zq.shape
    tq = min(tq, t_q)
    return pl.pallas_call(
        functools.partial(_mem_kernel, tq=tq),
        grid=(b, t_q // tq),
        in_specs=[pl.BlockSpec((1, tq, 2 * LANES), lambda bi, i: (bi, i, _NEW["q_m"] // 256)),
                  pl.BlockSpec((1,) + mkv.shape[1:], lambda bi, i: (bi, 0, 0))],
        out_specs=pl.BlockSpec((1, tq, 2 * LANES), lambda bi, i: (bi, i, 0)),
        out_shape=jax.ShapeDtypeStruct((b, t_q, 2 * LANES), F32),
        compiler_params=_params(("parallel", "parallel")),
        name="mem_attention",
    )(zq, mkv)


def _combine_kernel(x_ref, oa_ref, ob_ref, oc_ref, om_ref, gate_ref, wb_ref, wo_ref, y_ref):
    d = x_ref.shape[-1]
    h = None
    for bi, o_ref in enumerate((oa_ref, ob_ref, oc_ref, om_ref)):
        t = gate_ref[:, bi * d:(bi + 1) * d] * _dot(o_ref[...].astype(BF16), wb_ref[bi])
        h = t if h is None else h + t
    y_ref[...] = x_ref[...] + _dot(h.astype(BF16), wo_ref[...])


def _combine(x2d, outs, gate, w_branch, w_out, *, tm):
    m, d = x2d.shape
    tm = min(tm, m)
    bw = outs[0].shape[-1]
    o_spec = pl.BlockSpec((tm, bw), lambda i: (i, 0))
    return pl.pallas_call(
        _combine_kernel,
        grid=(m // tm,),
        in_specs=[pl.BlockSpec((tm, d), lambda i: (i, 0)), o_spec, o_spec, o_spec, o_spec,
                  pl.BlockSpec((tm, N_BRANCH * d), lambda i: (i, 0)),
                  pl.BlockSpec((N_BRANCH, bw, d), lambda i: (0, 0, 0)),
                  pl.BlockSpec((d, d), lambda i: (0, 0))],
        out_specs=pl.BlockSpec((tm, d), lambda i: (i, 0)),
        out_shape=jax.ShapeDtypeStruct((m, d), F32),
        compiler_params=_params(("parallel",)),
        name="branch_mix",
    )(x2d, *outs, gate, w_branch, w_out)


FF_CHUNK = 256
HALO = 16


def _rms(x, g):
    return x * lax.rsqrt(jnp.mean(x * x, axis=-1, keepdims=True) + RMS_EPS) * g


def _conv3(cw, u2, u1, u0):
    return cw[3:4] + cw[0:1] * u2 + cw[1:2] * u1 + cw[2:3] * u0


def _ffn_kernel(x_ref, xh_ref, ha_ref, hb_ref, g_ref, wa_ref, wb_ref, cwa_ref, cwb_ref, wdn_ref,
                y_ref, sta_ref, stb_ref, xn_ref, xhn_ref, acc_ref, *, tm):
    i = pl.program_id(1)
    j = pl.program_id(2)

    @pl.when(j == 0)
    def _():
        xn_ref[...] = _rms(x_ref[0], g_ref[...]).astype(BF16)
        xhn_ref[...] = _rms(xh_ref[0], g_ref[...]).astype(BF16)
        acc_ref[...] = jnp.zeros(acc_ref.shape, F32)

    def half(w_ref, hist_ref, cw_ref, st_ref):
        u = _dot(xn_ref[...], w_ref[...])
        u_prev = _dot(xhn_ref[...], w_ref[...])[HALO - SUBLANES:HALO]
        prev = jnp.where(i == 0, hist_ref[0], u_prev)
        ext = jnp.concatenate([prev, u], axis=0)
        st_ref[0, 0] = ext[tm:tm + SUBLANES]
        return _conv3(cw_ref[...], pltpu.roll(ext, 2, 0)[SUBLANES:], pltpu.roll(ext, 1, 0)[SUBLANES:], u)

    a = half(wa_ref, ha_ref, cwa_ref, sta_ref)
    b = half(wb_ref, hb_ref, cwb_ref, stb_ref)
    acc_ref[...] += _dot((a * jax.nn.sigmoid(a) * b).astype(BF16), wdn_ref[...])

    @pl.when(j == pl.num_programs(2) - 1)
    def _():
        y_ref[0] = x_ref[0] + acc_ref[...]


def _conv_table(conv_w, conv_b):
    return jnp.concatenate([conv_w, conv_b[None, :], jnp.zeros((SUBLANES - CONV_WIDTH - 1, conv_b.shape[0]), F32)])


def _conv_ffn(x, hist, g, w_up, conv_w, conv_b, w_down, *, tm):
    b, t, d = x.shape
    d_ff = w_down.shape[0]
    n_j = d_ff // FF_CHUNK
    tm = min(tm, t)
    cw = _conv_table(conv_w, conv_b)
    hist8 = jnp.concatenate([jnp.zeros((b, SUBLANES - 2, 2 * d_ff), F32), hist], axis=1)
    a_col = lambda bi, i, j: (0, j)
    b_col = lambda bi, i, j: (0, n_j + j)
    st_spec = pl.BlockSpec((1, 1, SUBLANES, FF_CHUNK), lambda bi, i, j: (bi, i, 0, j))
    st_shape = jax.ShapeDtypeStruct((b, t // tm, SUBLANES, d_ff), F32)
    y, st_a, st_b = pl.pallas_call(
        functools.partial(_ffn_kernel, tm=tm),
        grid=(b, t // tm, n_j),
        in_specs=[pl.BlockSpec((1, tm, d), lambda bi, i, j: (bi, i, 0)),
                  pl.BlockSpec((1, HALO, d), lambda bi, i, j: (bi, jnp.maximum(i * (tm // HALO) - 1, 0), 0)),
                  pl.BlockSpec((1, SUBLANES, FF_CHUNK), lambda bi, i, j: (bi, 0, j)),
                  pl.BlockSpec((1, SUBLANES, FF_CHUNK), lambda bi, i, j: (bi, 0, n_j + j)),
                  pl.BlockSpec((1, d), lambda bi, i, j: (0, 0)),
                  pl.BlockSpec((d, FF_CHUNK), a_col), pl.BlockSpec((d, FF_CHUNK), b_col),
                  pl.BlockSpec((SUBLANES, FF_CHUNK), a_col), pl.BlockSpec((SUBLANES, FF_CHUNK), b_col),
                  pl.BlockSpec((FF_CHUNK, d), lambda bi, i, j: (j, 0))],
        out_specs=[pl.BlockSpec((1, tm, d), lambda bi, i, j: (bi, i, 0)), st_spec, st_spec],
        out_shape=[jax.ShapeDtypeStruct((b, t, d), F32), st_shape, st_shape],
        scratch_shapes=[pltpu.VMEM((tm, d), BF16), pltpu.VMEM((HALO, d), BF16), pltpu.VMEM((tm, d), F32)],
        compiler_params=_params(("parallel", "arbitrary", "arbitrary")),
        name="conv_ffn",
    )(x, x, hist8, hist8, g.reshape(1, d), w_up, w_up, cw, cw, w_down)
    return y, jnp.concatenate([st_a[:, -1, SUBLANES - 2:], st_b[:, -1, SUBLANES - 2:]], axis=-1)


def _ffn_row_kernel(x_ref, h0a_ref, h0b_ref, h1a_ref, h1b_ref, g_ref, wa_ref, wb_ref, cwa_ref, cwb_ref, wdn_ref,
                    y_ref, ua_ref, ub_ref, xn_ref, acc_ref):
    j = pl.program_id(0)

    @pl.when(j == 0)
    def _():
        xn_ref[...] = _rms(x_ref[...], g_ref[...]).astype(BF16)
        acc_ref[...] = jnp.zeros(acc_ref.shape, F32)

    ua = _dot(xn_ref[...], wa_ref[...])
    ub = _dot(xn_ref[...], wb_ref[...])
    ua_ref[...] = ua
    ub_ref[...] = ub
    a = _conv3(cwa_ref[...], h0a_ref[...], h1a_ref[...], ua)
    b = _conv3(cwb_ref[...], h0b_ref[...], h1b_ref[...], ub)
    acc_ref[...] += _dot((a * jax.nn.sigmoid(a) * b).astype(BF16), wdn_ref[...])

    @pl.when(j == pl.num_programs(0) - 1)
    def _():
        y_ref[...] = x_ref[...] + acc_ref[...]


def _conv_ffn_rows(x2d, hist, g, w_up, conv_w, conv_b, w_down):
    b, d = x2d.shape
    d_ff = w_down.shape[0]
    n_j = d_ff // FF_CHUNK
    cw = _conv_table(conv_w, conv_b)
    h0, h1 = hist[:, 0], hist[:, 1]
    a_col = lambda j: (0, j)
    b_col = lambda j: (0, n_j + j)
    row_a, row_b = pl.BlockSpec((b, FF_CHUNK), a_col), pl.BlockSpec((b, FF_CHUNK), b_col)
    y, ua, ub = pl.pallas_call(
        _ffn_row_kernel,
        grid=(n_j,),
        in_specs=[pl.BlockSpec((b, d), lambda j: (0, 0)), row_a, row_b, row_a, row_b,
                  pl.BlockSpec((1, d), lambda j: (0, 0)),
                  pl.BlockSpec((d, FF_CHUNK), a_col), pl.BlockSpec((d, FF_CHUNK), b_col),
                  pl.BlockSpec((SUBLANES, FF_CHUNK), a_col), pl.BlockSpec((SUBLANES, FF_CHUNK), b_col),
                  pl.BlockSpec((FF_CHUNK, d), lambda j: (j, 0))],
        out_specs=[pl.BlockSpec((b, d), lambda j: (0, 0)), row_a, row_a],
        out_shape=[jax.ShapeDtypeStruct((b, d), F32), jax.ShapeDtypeStruct((b, d_ff), F32),
                   jax.ShapeDtypeStruct((b, d_ff), F32)],
        scratch_shapes=[pltpu.VMEM((b, d), BF16), pltpu.VMEM((b, d), F32)],
        compiler_params=_params(("arbitrary",)),
        name="conv_ffn_rows",
    )(x2d, h0, h0, h1, h1, g.reshape(1, d), w_up, w_up, cw, cw, w_down)
    return y, jnp.stack([h1, jnp.concatenate([ua, ub], axis=-1)], axis=1)


def _cols(zh, name, width):
    return zh[..., _NEW[name]:_NEW[name] + width]


def _new_rows(zh):
    b, t, _ = zh.shape
    dsa = jnp.concatenate([_cols(zh, "k_a", 2 * HEAD_DIM), _cols(zh, "ik", HEAD_DIM)], axis=-1)
    nsa = _cols(zh, "kc", 4 * HEAD_DIM)
    moba = _cols(zh, "k_c", 2 * N_HEADS * HEAD_DIM)
    win = _cols(zh, "kw", 2 * HEAD_DIM)
    return (dsa.reshape(b, t, 3, HEAD_DIM), nsa.reshape(b, t, 4, HEAD_DIM),
            moba.reshape(b, t, 2, N_HEADS, HEAD_DIM), win.reshape(b, t, 2, HEAD_DIM))


def _prompt_layer(x, mem, p):
    b, t, d = x.shape
    zh2d, gate = _projections(x.reshape(b * t, d), jnp.arange(t, dtype=jnp.int32), p["ln"][0], p["w_heads"],
                              p["cfg"], p["w_gate"], tm=256)
    zh = zh2d.reshape(b, t, N_HEADCOLS)
    o_a = _dsa_call(zh, zh, zh, kv_blk=BLK_KV_A, ik_blk=BLK_IK, tq=128, n_keys=t)
    cmp = _nsa_compress(zh, p["cmp_w"], blk=BLK_CMP)
    o_b = _nsa_call(zh, cmp, zh, zh, sel_blk=BLK_SEL, win_blk=BLK_WIN, tq=128, n_keys=t)
    o_c = _moba_call(zh, zh, tq=MOBA_BLOCK)
    mkv = _memory_kv(mem, p["ln"][2], p["w_mem_kv"], p["g_mem"][1])
    o_m = _mem_call(zh, mkv, tq=256)
    outs = [o.reshape(b * t, o.shape[-1]) for o in (o_a, o_b, o_c, o_m)]
    x1 = _combine(x.reshape(b * t, d), outs, gate, p["w_branch"], p["w_out"], tm=512).reshape(b, t, d)
    hist = jnp.zeros((b, CONV_WIDTH - 1, p["w_up"].shape[1]), F32)
    y, conv = _conv_ffn(x1, hist, p["ln"][1], p["w_up"], p["conv_w"], p["conv_b"], p["w_down"], tm=1024)
    dsa, nsa, moba, win = _new_rows(zh)
    keep = min(NSA_WINDOW, t)
    return y, dsa, nsa, moba, win[:, t - keep:], mkv.reshape(b, mkv.shape[1], 2, N_HEADS, HEAD_DIM), conv


def _first_row(x, n):
    return jnp.pad(x, ((0, 0), (0, n - 1)) + ((0, 0),) * (x.ndim - 2))


def _sample_layer(x, layer, caches, page_table, win_state, mem_kv, conv_hist, p):
    b, _, d = x.shape
    cache_dsa, cache_nsa, cache_moba = caches
    page = cache_dsa.shape[2]
    past = page_table.shape[1] * page
    x2d = x.reshape(b, d)
    zh, gate = _projections(x2d, jnp.full((b,), past, jnp.int32), p["ln"][0], p["w_heads"], p["cfg"],
                            p["w_gate"], tm=b)
    zq = _first_row(zh[:, None, :], DEC_ROWS)
    dsa, nsa, moba, win = _new_rows(zh[:, None, :])
    o_a = _dsa_decode_call(zq, cache_dsa, layer, page_table, _first_row(dsa.reshape(b, 1, -1), DEC_KEY_CHUNK))
    win_all = jnp.concatenate([win_state.reshape(b, -1, 2 * HEAD_DIM), win.reshape(b, 1, 2 * HEAD_DIM)], axis=1)
    w_pad = -(-win_all.shape[1] // LANES) * LANES
    win_rows = jnp.pad(win_all, ((0, 0), (0, w_pad - win_all.shape[1]), (0, 0)))
    o_b = _nsa_decode_call(zq, cache_nsa, layer, page_table, _first_row(nsa.reshape(b, 1, -1), DEC_KEY_CHUNK),
                           win_rows, p["cmp_w"])
    o_c = _moba_decode_call(zq, cache_moba, layer, page_table, _first_row(moba, page))
    o_m = _mem_call(zq, mem_kv.reshape(b, mem_kv.shape[1], -1), tq=DEC_ROWS)
    outs = [o[:, 0, :] for o in (o_a, o_b, o_c, o_m)]
    x1 = _combine(x2d, outs, gate, p["w_branch"], p["w_out"], tm=b)
    y, conv = _conv_ffn_rows(x1, conv_hist, p["ln"][1], p["w_up"], p["conv_w"], p["conv_b"], p["w_down"])
    keep = win_state.shape[1]
    win_new = win_all[:, win_all.shape[1] - keep:].reshape(b, keep, 2, HEAD_DIM)
    return y.reshape(b, 1, d), dsa, nsa, moba, win_new, conv


def kernel(x_prompt, x_sample, cache_dsa, cache_nsa, cache_moba, state_nsa_win, cache_mem, state_ffn_conv,
           page_table, mem_prompt, ln, w_in, g_dsa, g_nsa, g_moba, g_mem, w_mem_kv, w_cmp1, w_cmp2, pe_cmp,
           w_branch, w_out, w_up, conv_w, conv_b, w_down):
    depth = ln.shape[0]
    caches = (cache_dsa.reshape(*cache_dsa.shape[:3], -1), cache_nsa.reshape(*cache_nsa.shape[:3], -1), cache_moba)
    xp, xs = x_prompt, x_sample
    outs_p = [[] for _ in range(6)]
    outs_s = [[] for _ in range(5)]
    for l in range(depth):
        w_heads, cfg = _head_weights(w_in[l], g_dsa[l], g_nsa[l], g_moba[l], g_mem[l])
        p = dict(ln=ln[l], w_heads=w_heads, cfg=cfg, w_gate=w_in[l][:, GATE_ORIG:].astype(BF16), g_mem=g_mem[l],
                 w_mem_kv=w_mem_kv[l], cmp_w=_compress_weights(w_cmp1[l], w_cmp2[l], pe_cmp[l]),
                 w_branch=w_branch[l].astype(BF16), w_out=w_out[l].astype(BF16), w_up=w_up[l].astype(BF16),
                 conv_w=conv_w[l], conv_b=conv_b[l], w_down=w_down[l].astype(BF16))
        xp, *rest = _prompt_layer(xp, mem_prompt, p)
        for acc, r in zip(outs_p, rest):
            acc.append(r)
        xs, *rest = _sample_layer(xs, l, caches, page_table, state_nsa_win[l], cache_mem[l], state_ffn_conv[l], p)
        for acc, r in zip(outs_s, rest):
            acc.append(r)
    dsa_p, nsa_p, moba_p, win_p, memkv_p, conv_p = [jnp.stack(a) for a in outs_p]
    dsa_s, nsa_s, moba_s, win_s, conv_s = [jnp.stack(a) for a in outs_s]
    return (xp, xs, dsa_p, dsa_s, nsa_p, nsa_s, moba_p, moba_s, win_p, win_s, memkv_p, conv_p, conv_s)
```

```python
import functools
import math

import numpy as np
import jax
import jax.numpy as jnp
from jax import lax
from jax.experimental import pallas as pl
from jax.experimental.pallas import tpu as pltpu

HEAD_DIM = 64
ROPE_DIM = HEAD_DIM // 4
ROPE_THETA = 500000.0
N_HEADS = 4
DSA_TOPK = 256
NSA_CMP_LEN = 32
NSA_CMP_STRIDE = 16
NSA_SEL_BLOCK = 64
NSA_SEL_TOPN = 16
NSA_WINDOW = 512
MOBA_BLOCK = 256
MOBA_TOPK = 3
N_BRANCH = 4
CONV_WIDTH = 3
RMS_EPS = 1e-6

LANES = 128
SUBLANES = 8
VMEM_LIMIT = 56 * 1024 * 1024
DEC_ROWS = SUBLANES
KEY_CHUNK = 512
DEC_KEY_CHUNK = 2048
PAGES_PER_STEP = 8

F32 = jnp.float32
BF16 = jnp.bfloat16
NEG_INF = float("-inf")
SCALE = HEAD_DIM ** -0.5
SCALE_LOG2 = SCALE * math.log2(math.e)

KEY_NEG_INF = int(np.uint32(0xFF800000) ^ np.uint32(0x7FFFFFFF)) - 2 ** 32
KEY_POS_INF = 0x7F800000
INT_MIN = -2 ** 31
BF16_TINY = 2 ** 7

_NT = (((1,), (1,)), ((), ()))

_ORIG = dict(q_a=0, k_a=256, v_a=320, iq=384, ik=640, iw=704, q_b=708, kc=964, vc=1028, ks=1092, vs=1156,
             kw=1220, vw=1284, g_b=1348, q_c=1360, k_c=1616, v_c=1872, q_m=2128)
GATE_ORIG = 2384
_NEW = dict(q_a=0, iq=256, q_b=512, q_c=768, q_m=1024, k_c=1280, v_c=1536, k_a=1792, v_a=1856, ik=1920,
            iw=1984, kc=2048, vc=2112, ks=2176, vs=2240, kw=2304, vw=2368, g_b=2432)
_WIDTH = dict(q_a=256, iq=256, q_b=256, q_c=256, q_m=256, k_c=256, v_c=256, k_a=64, v_a=64, ik=64, iw=4,
              kc=64, vc=64, ks=64, vs=64, kw=64, vw=64, g_b=12)
N_HEADCOLS = 2560
_NORMED = ("q_a", "q_b", "q_c", "q_m", "k_c", "k_a", "kc", "ks", "kw")
_ROPED = ("q_a", "iq", "q_b", "q_c", "k_c", "k_a", "ik", "kc", "ks", "kw")
_SIGMOID = ("g_b",)

BLK_KV_A, BLK_IK, BLK_CMP, BLK_SEL, BLK_WIN, BLK_GB = 14, 15, 16, 17, 18, 19
BLK_KC, BLK_VC = 10, 12


def _head_layout():
    masks = {k: np.zeros((N_HEADCOLS,), np.float32) for k in ("norm", "rope", "sig")}
    for name, new in _NEW.items():
        w = _WIDTH[name]
        if name in _NORMED:
            masks["norm"][new:new + w] = 1
        if name in _ROPED:
            masks["rope"][new:new + w] = 1
        if name in _SIGMOID:
            masks["sig"][new:new + w] = 1
    return masks


_COL_MASKS = _head_layout()


def _chunk_any(mask):
    return tuple(bool(mask[c * LANES:(c + 1) * LANES].any()) for c in range(mask.shape[0] // LANES))


def _params(sem):
    return pltpu.CompilerParams(dimension_semantics=sem, vmem_limit_bytes=VMEM_LIMIT)


def _lane_iota(n=LANES):
    return lax.broadcasted_iota(jnp.int32, (1, n), 1)


def _row_iota(n):
    return lax.broadcasted_iota(jnp.int32, (n, 1), 0)


def _split_bf16(x):
    hi = x.astype(BF16)
    lo = (x - hi.astype(F32)).astype(BF16)
    return hi, lo


def _dot(a, b):
    return jnp.dot(a, b, preferred_element_type=F32)


def _dot_nt(a, b):
    return lax.dot_general(a, b, _NT, preferred_element_type=F32)


def _proj_kernel(x_ref, g_ref, w_ref, cfg_ref, rope_ref, o_ref, *, norm_chunks, rope_chunks, sig_chunks):
    x = x_ref[...]
    xn = x * lax.rsqrt(jnp.mean(x * x, axis=-1, keepdims=True) + RMS_EPS) * g_ref[...]
    z = _dot(xn.astype(BF16), w_ref[...])
    rr = lax.broadcasted_iota(jnp.int32, (LANES, LANES), 0) // HEAD_DIM
    cc = lax.broadcasted_iota(jnp.int32, (LANES, LANES), 1) // HEAD_DIM
    seg = (rr == cc).astype(BF16)
    for c in range(z.shape[1] // LANES):
        sl = slice(c * LANES, (c + 1) * LANES)
        zc = z[:, sl]
        if norm_chunks[c]:
            hi, lo = _split_bf16(zc * zc)
            ss = _dot(hi, seg) + _dot(lo, seg)
            r = lax.rsqrt(ss * (1.0 / HEAD_DIM) + RMS_EPS)
            zc = jnp.where(cfg_ref[0:1, sl] > 0, zc * r * cfg_ref[1:2, sl], zc)
        if rope_chunks[c]:
            rot = (zc * rope_ref[:, 0:LANES]
                   + pltpu.roll(zc, LANES - ROPE_DIM // 2, 1) * rope_ref[:, LANES:2 * LANES]
                   + pltpu.roll(zc, ROPE_DIM // 2, 1) * rope_ref[:, 2 * LANES:3 * LANES])
            zc = jnp.where(cfg_ref[2:3, sl] > 0, rot, zc)
        if sig_chunks[c]:
            zc = jnp.where(cfg_ref[3:4, sl] > 0, jax.nn.sigmoid(zc), zc)
        o_ref[:, sl] = zc


def _project(x2d, g, w_bf16, cfg, rope, masks, *, tm, tn):
    m, d = x2d.shape
    n = w_bf16.shape[1]
    tn = min(tn, n)
    tm = min(tm, m)
    flags = {k: _chunk_any(v) for k, v in masks.items()}
    per_tile = tn // LANES
    for k, v in flags.items():
        assert all(v[t * per_tile:(t + 1) * per_tile] == v[:per_tile] for t in range(n // tn)), k
    rope_tiles = rope.shape[0] // tm
    kern = functools.partial(_proj_kernel, norm_chunks=flags["norm"][:per_tile],
                             rope_chunks=flags["rope"][:per_tile], sig_chunks=flags["sig"][:per_tile])
    return pl.pallas_call(
        kern,
        grid=(m // tm, n // tn),
        in_specs=[pl.BlockSpec((tm, d), lambda i, j: (i, 0)),
                  pl.BlockSpec((1, d), lambda i, j: (0, 0)),
                  pl.BlockSpec((d, tn), lambda i, j: (0, j)),
                  pl.BlockSpec((SUBLANES, tn), lambda i, j: (0, j)),
                  pl.BlockSpec((tm, 3 * LANES), lambda i, j: (i % rope_tiles, 0))],
        out_specs=pl.BlockSpec((tm, tn), lambda i, j: (i, j)),
        out_shape=jax.ShapeDtypeStruct((m, n), F32),
        compiler_params=_params(("parallel", "arbitrary")),
        name="project",
    )(x2d, g, w_bf16, cfg, rope)


def _rope_table(pos):
    half = ROPE_DIM // 2
    inv_freq = ROPE_THETA ** (-jnp.arange(half, dtype=F32) / half)
    ang = pos.astype(F32)[:, None] * inv_freq[None, :]
    cos, sin = jnp.cos(ang), jnp.sin(ang)
    t = pos.shape[0]
    ones = jnp.ones((t, HEAD_DIM - ROPE_DIM), F32)
    zeros = jnp.zeros((t, HEAD_DIM - ROPE_DIM), F32)
    zh = jnp.zeros((t, half), F32)
    c64 = jnp.concatenate([cos, cos, ones], axis=1)
    s1 = jnp.concatenate([-sin, zh, zeros], axis=1)
    s2 = jnp.concatenate([zh, sin, zeros], axis=1)
    return jnp.concatenate([c64, c64, s1, s1, s2, s2], axis=1)


def _head_weights(w_in_l, g_dsa, g_nsa, g_moba, g_mem):
    d = w_in_l.shape[0]
    pieces, at = [], 0
    for name, new in sorted(_NEW.items(), key=lambda kv: kv[1]):
        if new > at:
            pieces.append(jnp.zeros((d, new - at), w_in_l.dtype))
        pieces.append(w_in_l[:, _ORIG[name]:_ORIG[name] + _WIDTH[name]])
        at = new + _WIDTH[name]
    pieces.append(jnp.zeros((d, N_HEADCOLS - at), w_in_l.dtype))
    w = jnp.concatenate(pieces, axis=1).astype(BF16)
    gain = jnp.ones((N_HEADCOLS,), F32)
    for name, gvec in (("q_a", g_dsa[0]), ("k_a", g_dsa[1]), ("q_b", g_nsa[0]), ("kc", g_nsa[1]),
                       ("ks", g_nsa[2]), ("kw", g_nsa[3]), ("q_c", g_moba[0]), ("k_c", g_moba[1]),
                       ("q_m", g_mem[0])):
        reps = _WIDTH[name] // HEAD_DIM
        gain = lax.dynamic_update_slice(gain, jnp.tile(gvec.astype(F32), reps), (_NEW[name],))
    zero = jnp.zeros((N_HEADCOLS,), F32)
    cfg = jnp.stack([jnp.asarray(_COL_MASKS["norm"]), gain, jnp.asarray(_COL_MASKS["rope"]),
                     jnp.asarray(_COL_MASKS["sig"]), zero, zero, zero, zero])
    return w, cfg


def _gate_cfg(n):
    z = jnp.zeros((n,), F32)
    o = jnp.ones((n,), F32)
    return jnp.stack([z, o, z, o, z, z, z, z])


def _projections(x2d, pos, ln0, w_heads, cfg, w_gate, *, tm):
    d = x2d.shape[1]
    rope = _rope_table(pos)
    g = ln0.reshape(1, d)
    zh = _project(x2d, g, w_heads, cfg, rope, _COL_MASKS, tm=tm, tn=N_HEADCOLS)
    n_g = w_gate.shape[1]
    gmask = dict(norm=np.zeros((n_g,), np.float32), rope=np.zeros((n_g,), np.float32),
                 sig=np.ones((n_g,), np.float32))
    gate = _project(x2d, g, w_gate, _gate_cfg(n_g), rope, gmask, tm=tm, tn=1024)
    return zh, gate


def _stack_heads(x256):
    lane = _lane_iota()
    parts = []
    for c in range(2):
        ch = x256[:, c * LANES:(c + 1) * LANES]
        parts.append(jnp.where(lane < HEAD_DIM, ch, 0.0))
        parts.append(jnp.where(lane < HEAD_DIM, pltpu.roll(ch, HEAD_DIM, 1), 0.0))
    return jnp.concatenate(parts, axis=0)


def _unstack_heads_t(o_t, tq):
    o = o_t.T
    lane = _lane_iota()
    chunks = []
    for c in range(2):
        even, odd = o[2 * c * tq:(2 * c + 1) * tq], o[(2 * c + 1) * tq:(2 * c + 2) * tq]
        chunks.append(jnp.where(lane < HEAD_DIM, pltpu.roll(even, HEAD_DIM, 1), odd))
    return jnp.concatenate(chunks, axis=1)


def _pair_heads(q128):
    lane = _lane_iota()
    return jnp.concatenate([jnp.where(lane < HEAD_DIM, q128, 0.0), jnp.where(lane >= HEAD_DIM, q128, 0.0)], axis=0)


def _rows_t(x):
    tq = x.shape[0]
    if tq < LANES:
        x = jnp.concatenate([x, jnp.zeros((LANES - tq, x.shape[1]), x.dtype)], axis=0)
    return x.T[:, 0:tq]


def _tile_lanes(x, n):
    return jnp.concatenate([x] * n, axis=1)


def _tsoftmax_init(m_ref, l_ref, acc_ref):
    m_ref[...] = jnp.full(m_ref.shape, NEG_INF, F32)
    l_ref[...] = jnp.zeros(l_ref.shape, F32)
    acc_ref[...] = jnp.zeros(acc_ref.shape, F32)


def _tsoftmax_step(s_t, v_t, m_ref, l_ref, acc_ref):
    m_old = m_ref[...]
    m_new = jnp.maximum(m_old, jnp.max(s_t, axis=0, keepdims=True))
    m_safe = jnp.where(m_new == NEG_INF, 0.0, m_new)
    p = jnp.exp2(s_t - m_safe).astype(BF16)
    alpha = jnp.exp2(m_old - m_safe)
    ones = jnp.ones((SUBLANES, p.shape[0]), BF16)
    l_ref[...] = alpha * l_ref[...] + _dot(ones, p)[0:1]
    acc_ref[...] = alpha * acc_ref[...] + _dot(v_t, p)
    m_ref[...] = m_new


def _tsoftmax_result(l_ref, acc_ref):
    return acc_ref[...] / jnp.maximum(l_ref[...], 1e-30)


def _tmasked_softmax(s_t, valid_t):
    s = jnp.where(valid_t, s_t, NEG_INF)
    m = jnp.max(s, axis=0, keepdims=True)
    m = jnp.where(m == NEG_INF, 0.0, m)
    e = jnp.exp2(s - m)
    return e / jnp.maximum(jnp.sum(e, axis=0, keepdims=True), 1e-30)


def _tsoftmax_scratch(d, r):
    return [pltpu.VMEM((1, r), F32), pltpu.VMEM((1, r), F32), pltpu.VMEM((d, r), F32)]


def _chunk_loader(ref, kc):
    def load(c):
        rows = pl.ds(pl.multiple_of(c * kc, kc), kc)
        x = ref[0, rows, :] if len(ref.shape) == 3 else ref[rows, :]
        return x.astype(BF16)
    return load


def _fill_transposed(src_ref, dst_ref, kc):
    def body(c, carry):
        dst_ref[c] = src_ref[0, pl.ds(pl.multiple_of(c * kc, kc), kc), :].T.astype(BF16)
        return carry
    lax.fori_loop(0, dst_ref.shape[0], body, 0)


def _rank_rows(val_t, src_ref, n_rows):
    src_ref[...] = val_t
    j = _row_iota(val_t.shape[0])

    def body(i, rank):
        row = src_ref[pl.ds(i, 1), :]
        beats = (row > val_t) | ((row == val_t) & (i < j))
        return rank + jnp.where(beats, 1, 0)

    return lax.fori_loop(0, n_rows, body, jnp.zeros(val_t.shape, jnp.int32))


def _select_top_t(val_t, rank_ref, n_valid, n_sel):
    rank = _rank_rows(val_t, rank_ref, n_valid)
    return jnp.where((rank < n_sel) & (val_t > NEG_INF), 1.0, 0.0)


def _key16_value(k16):
    return pltpu.bitcast((k16 ^ ((k16 >> 15) & 0x7FFF)) << 16, F32)


def _dsa_core(q, iq, iw, load_kv, load_ik, load_kvt, key_ref, hi_ref, m_ref, l_ref, acc_ref,
              *, tq, kc, n_top, p0, idx_bits):
    n_chunks = (p0 + tq - 1) // kc + 1
    pos = p0 + _lane_iota(tq)
    iqs_t = _stack_heads(iq).T.astype(BF16)
    qs_t = (_stack_heads(q) * SCALE_LOG2).T.astype(BF16)
    iw_t = _rows_t(iw)
    w_rows = [iw_t[HEAD_DIM + h:HEAD_DIM + h + 1, :] for h in range(N_HEADS)]
    packed = 2 * SUBLANES

    def chunk_pos(c):
        return c * kc + _row_iota(kc)

    def score_chunk(c, carry):
        lg = _dot(load_ik(c), iqs_t)
        sc = w_rows[0] * jnp.maximum(lg[:, 0:tq], 0.0)
        for h in range(1, N_HEADS):
            sc = sc + w_rows[h] * jnp.maximum(lg[:, h * tq:(h + 1) * tq], 0.0)
        sc = jnp.where(chunk_pos(c) <= pos, sc, NEG_INF)
        bits = pltpu.bitcast(sc, jnp.int32)
        key = bits ^ ((bits >> 31) & 0x7FFFFFFF)
        key_ref[c] = key
        hi_ref[c] = _key16_value(key >> 16).astype(BF16)
        return carry

    lax.fori_loop(0, n_chunks, score_chunk, 0)

    def count_hi(cand):
        cb = jnp.broadcast_to(cand, (packed, tq)).astype(BF16)
        one, zero = jnp.ones((packed, tq), BF16), jnp.zeros((packed, tq), BF16)

        def body(c, acc):
            v = hi_ref[c]
            parts = [jnp.where(v[r * packed:(r + 1) * packed] >= cb, one, zero) for r in range(kc // packed)]
            while len(parts) > 1:
                parts = [a + b for a, b in zip(parts[0::2], parts[1::2])]
            return acc + parts[0].astype(F32)

        acc = lax.fori_loop(0, n_chunks, body, jnp.zeros((packed, tq), F32))
        return jnp.sum(acc, axis=0, keepdims=True)

    def count(pred):
        def body(c, acc):
            hit = jnp.where(pred(key_ref[c], chunk_pos(c)), 1, 0)
            parts = [hit[r * SUBLANES:(r + 1) * SUBLANES] for r in range(kc // SUBLANES)]
            while len(parts) > 1:
                parts = [a + b for a, b in zip(parts[0::2], parts[1::2])]
            return acc + parts[0]
        acc = lax.fori_loop(0, n_chunks, body, jnp.zeros((SUBLANES, tq), jnp.int32))
        return jnp.sum(acc, axis=0, keepdims=True)

    def hi_bit(b, t16):
        cand = t16 + jnp.left_shift(jnp.int32(1), 15 - b)
        cnt = count_hi(_key16_value(cand))
        return jnp.where((cnt >= n_top) | (cand <= KEY_NEG_INF >> 16), cand, t16)

    t16 = lax.fori_loop(0, 16, hi_bit, jnp.full((1, tq), -2 ** 15, jnp.int32))
    in_band = (t16 >= -BF16_TINY) & (t16 < BF16_TINY)
    first_bit = jnp.where(jnp.max(jnp.where(in_band, 1, 0)) > 0, 0, 8)

    def thr_bit(b, thr):
        cand = thr + jnp.left_shift(jnp.int32(1), 23 - b)
        cnt = count(lambda key, kpos: key >= cand)
        return jnp.where((cnt >= n_top) & (cand > thr), cand, thr)

    thr = lax.fori_loop(first_bit, 24, thr_bit, jnp.where(in_band, -BF16_TINY, t16) << 16)

    def last_tied():
        need = n_top - count(lambda key, kpos: key > thr)

        def idx_bit(b, last):
            cand = last + jnp.left_shift(jnp.int32(1), idx_bits - 1 - b)
            cnt = count(lambda key, kpos: (key == thr) & (kpos < cand))
            return jnp.where(cnt < need, cand, last)

        return lax.fori_loop(0, idx_bits, idx_bit, jnp.zeros((1, tq), jnp.int32))

    over = (count(lambda key, kpos: key >= thr) > n_top) & (thr > KEY_NEG_INF)
    last = lax.cond(jnp.max(jnp.where(over, 1, 0)) > 0, last_tied,
                    lambda: jnp.full((1, tq), 2 ** idx_bits, jnp.int32))
    last = jnp.where(thr > KEY_NEG_INF, last, -1)

    _tsoftmax_init(m_ref, l_ref, acc_ref)

    def attend(c, carry):
        s = _dot(load_kv(c), qs_t)
        key = key_ref[c]
        sel = ((key > thr) | ((key == thr) & (chunk_pos(c) <= last))) & (key < KEY_POS_INF)
        s = jnp.concatenate([jnp.where(sel, s[:, h * tq:(h + 1) * tq], NEG_INF) for h in range(N_HEADS)], axis=1)
        _tsoftmax_step(s, load_kvt(c), m_ref, l_ref, acc_ref)
        return carry

    lax.fori_loop(0, n_chunks, attend, 0)
    return _unstack_heads_t(_tsoftmax_result(l_ref, acc_ref), tq)


def _dsa_kernel(q_ref, iq_ref, iw_ref, kv_ref, ik_ref, o_ref, kvt_ref, key_ref, hi_ref, m_ref, l_ref, acc_ref,
                *, tq, kc, n_top, idx_bits):
    i = pl.program_id(1)

    @pl.when(i == 0)
    def _():
        _fill_transposed(kv_ref, kvt_ref, kc)

    o_ref[0] = _dsa_core(q_ref[0], iq_ref[0], iw_ref[0], _chunk_loader(kv_ref, kc), _chunk_loader(ik_ref, kc),
                         lambda c: kvt_ref[c], key_ref, hi_ref, m_ref, l_ref, acc_ref,
                         tq=tq, kc=kc, n_top=n_top, p0=i * tq, idx_bits=idx_bits)


def _row_softmax_step(rows, s, valid, v_bf16, m_ref, l_ref, acc_ref):
    s = jnp.where(valid, s, NEG_INF)
    m_old = m_ref[rows, :]
    m_new = jnp.maximum(m_old, jnp.max(s, axis=-1, keepdims=True))
    m_safe = jnp.where(m_new == NEG_INF, 0.0, m_new)
    p = jnp.exp(s - m_safe)
    alpha = jnp.exp(m_old - m_safe)
    l_ref[rows, :] = alpha * l_ref[rows, :] + jnp.sum(p, axis=-1, keepdims=True)
    acc_ref[rows, :] = alpha * acc_ref[rows, :] + _dot(p.astype(BF16), v_bf16)
    m_ref[rows, :] = m_new


def _row_softmax_init(m_ref, l_ref, acc_ref):
    m_ref[...] = jnp.full(m_ref.shape, NEG_INF, F32)
    l_ref[...] = jnp.zeros(l_ref.shape, F32)
    acc_ref[...] = jnp.zeros(acc_ref.shape, F32)


def _row_softmax_result(rows, l_ref, acc_ref):
    return acc_ref[rows, :] / jnp.maximum(l_ref[rows, :], 1e-30)


def _row_masked_softmax(s, valid):
    s = jnp.where(valid, s, NEG_INF)
    m = jnp.max(s, axis=-1, keepdims=True)
    m = jnp.where(m == NEG_INF, 0.0, m)
    e = jnp.exp(s - m)
    return e / jnp.maximum(jnp.sum(e, axis=-1, keepdims=True), 1e-30)


def _row_softmax_scratch(rows, d):
    return [pltpu.VMEM((rows, 1), F32), pltpu.VMEM((rows, 1), F32), pltpu.VMEM((rows, d), F32)]


def _unstack_heads(o_heads):
    lane = _lane_iota()
    chunks = []
    for c in range(2):
        chunks.append(jnp.where(lane < HEAD_DIM, pltpu.roll(o_heads[2 * c], HEAD_DIM, 1), o_heads[2 * c + 1]))
    return jnp.concatenate(chunks, axis=1)


def _dsa_rows_core(q, iq, iw, load_kv, load_ik, key_ref, m_ref, l_ref, acc_ref, *, tq, kc, n_top, p0, idx_bits):
    n_chunks = (p0 + tq - 1) // kc + 1
    pos = p0 + _row_iota(tq)
    iqs = _stack_heads(iq).astype(BF16)
    qs = (_stack_heads(q) * SCALE).astype(BF16)
    w_cols = [iw[:, HEAD_DIM + h:HEAD_DIM + h + 1] for h in range(N_HEADS)]

    def chunk_pos(c):
        return c * kc + _lane_iota(kc)

    def score_chunk(c, carry):
        lg = _dot_nt(iqs, load_ik(c))
        sc = w_cols[0] * jnp.maximum(lg[0:tq], 0.0)
        for h in range(1, N_HEADS):
            sc = sc + w_cols[h] * jnp.maximum(lg[h * tq:(h + 1) * tq], 0.0)
        sc = jnp.where(chunk_pos(c) <= pos, sc, NEG_INF)
        bits = pltpu.bitcast(sc, jnp.int32)
        key_ref[c] = bits ^ ((bits >> 31) & 0x7FFFFFFF)
        return carry

    lax.fori_loop(0, n_chunks, score_chunk, 0)

    def count(pred):
        def body(c, acc):
            hit = jnp.where(pred(key_ref[c], chunk_pos(c)), 1, 0)
            parts = [hit[:, t * LANES:(t + 1) * LANES] for t in range(kc // LANES)]
            while len(parts) > 1:
                parts = [a + b for a, b in zip(parts[0::2], parts[1::2])]
            return acc + parts[0]
        acc = lax.fori_loop(0, n_chunks, body, jnp.zeros((tq, LANES), jnp.int32))
        return jnp.sum(acc, axis=1, keepdims=True)

    def thr_bit(b, thr):
        cand = thr + jnp.left_shift(jnp.int32(1), 31 - b)
        cnt = count(lambda key, kpos: key >= cand)
        return jnp.where(cnt >= n_top, cand, thr)

    thr = lax.fori_loop(0, 32, thr_bit, jnp.full((tq, 1), INT_MIN, jnp.int32))

    def last_tied():
        need = n_top - count(lambda key, kpos: key > thr)

        def idx_bit(b, last):
            cand = last + jnp.left_shift(jnp.int32(1), idx_bits - 1 - b)
            cnt = count(lambda key, kpos: (key == thr) & (kpos < cand))
            return jnp.where(cnt < need, cand, last)

        return lax.fori_loop(0, idx_bits, idx_bit, jnp.zeros((tq, 1), jnp.int32))

    over = (count(lambda key, kpos: key >= thr) > n_top) & (thr > KEY_NEG_INF)
    last = lax.cond(jnp.max(jnp.where(over, 1, 0)) > 0, last_tied,
                    lambda: jnp.full((tq, 1), 2 ** idx_bits, jnp.int32))

    _row_softmax_init(m_ref, l_ref, acc_ref)

    def attend(c, carry):
        kvc = load_kv(c)
        s = _dot_nt(qs, kvc)
        key = key_ref[c]
        sel = (key > thr) | ((key == thr) & (chunk_pos(c) <= last))
        sel = sel & (key > KEY_NEG_INF) & (key < KEY_POS_INF)
        for h in range(N_HEADS):
            _row_softmax_step(pl.ds(h * tq, tq), s[h * tq:(h + 1) * tq], sel, kvc, m_ref, l_ref, acc_ref)
        return carry

    lax.fori_loop(0, n_chunks, attend, 0)
    return _unstack_heads([_row_softmax_result(pl.ds(h * tq, tq), l_ref, acc_ref) for h in range(N_HEADS)])


def _dsa_decode_kernel(pt_ref, q_ref, iq_ref, iw_ref, *rest, tq, kc, n_top, idx_bits, n_pages, page, pps):
    page_refs, (new_ref, o_ref, kv_ref, ik_ref, key_ref, m_ref, l_ref, acc_ref) = rest[:pps], rest[pps:]
    s = pl.program_id(1)
    n_steps = n_pages // pps

    def put(at, rows):
        kv_ref[at, :] = rows[:, 0:LANES].astype(BF16)
        ik = rows[:, LANES:]
        ik_ref[at, :] = jnp.concatenate([ik, jnp.zeros((rows.shape[0], 2 * LANES - rows.shape[1]), F32)],
                                        axis=1).astype(BF16)

    @pl.when(s < n_steps)
    def _():
        for r in range(pps):
            put(pl.ds(pl.multiple_of((s * pps + r) * page, page), page), page_refs[r][0, 0])

    @pl.when(s == n_steps)
    def _():
        put(pl.ds(n_pages * page, kc), new_ref[0])
        o_ref[0] = _dsa_rows_core(q_ref[0], iq_ref[0], iw_ref[0], _chunk_loader(kv_ref, kc), _chunk_loader(ik_ref, kc),
                                  key_ref, m_ref, l_ref, acc_ref, tq=tq, kc=kc, n_top=n_top, p0=n_pages * page,
                                  idx_bits=idx_bits)


def _dsa_scratch(n_kc, tq, kc):
    return ([pltpu.VMEM((n_kc, LANES, kc), BF16), pltpu.VMEM((n_kc, kc, tq), jnp.int32),
             pltpu.VMEM((n_kc, kc, tq), BF16)] + _tsoftmax_scratch(LANES, N_HEADS * tq))


def _page_specs(cache, layer, n_pages, pps):
    block = (1, 1) + cache.shape[2:]
    zeros = (0,) * (len(cache.shape) - 2)

    def spec(r):
        return pl.BlockSpec(block, lambda bi, s, pt: (layer, pt[bi, jnp.minimum(s * pps + r, n_pages - 1)]) + zeros)

    return [spec(r) for r in range(pps)]


def _dsa_decode_call(zq, cache, layer, page_table, new_rows, *, kc=DEC_KEY_CHUNK, pps=PAGES_PER_STEP):
    b, tq, _ = zq.shape
    n_pages = page_table.shape[1]
    page = cache.shape[2]
    past = n_pages * page
    assert past % kc == 0 and n_pages % pps == 0
    l_rows = past + kc
    n_kc = l_rows // kc
    kern = functools.partial(_dsa_decode_kernel, tq=tq, kc=kc, n_top=min(DSA_TOPK, (past + 1) // 4),
                             idx_bits=max(1, math.ceil(math.log2(l_rows))), n_pages=n_pages, page=page, pps=pps)
    grid_spec = pltpu.PrefetchScalarGridSpec(
        num_scalar_prefetch=1,
        grid=(b, n_pages // pps + 1),
        in_specs=[pl.BlockSpec((1, tq, 2 * LANES), lambda bi, s, pt: (bi, 0, _NEW["q_a"] // 256)),
                  pl.BlockSpec((1, tq, 2 * LANES), lambda bi, s, pt: (bi, 0, _NEW["iq"] // 256)),
                  pl.BlockSpec((1, tq, LANES), lambda bi, s, pt: (bi, 0, BLK_IK))]
        + _page_specs(cache, layer, n_pages, pps)
        + [pl.BlockSpec((1,) + new_rows.shape[1:], lambda bi, s, pt: (bi, 0, 0))],
        out_specs=pl.BlockSpec((1, tq, 2 * LANES), lambda bi, s, pt: (bi, 0, 0)),
        scratch_shapes=[pltpu.VMEM((l_rows, LANES), BF16), pltpu.VMEM((l_rows, LANES), BF16),
                        pltpu.VMEM((n_kc, tq, kc), jnp.int32)] + _row_softmax_scratch(N_HEADS * tq, LANES))
    return pl.pallas_call(
        kern, grid_spec=grid_spec,
        out_shape=jax.ShapeDtypeStruct((b, tq, 2 * LANES), F32),
        compiler_params=_params(("parallel", "arbitrary")),
        name="dsa_decode",
    )(page_table, zq, zq, zq, *([cache] * pps), new_rows)


def _dsa_call(zq, kv, ik, *, kv_blk, ik_blk, tq, n_keys, kc=KEY_CHUNK):
    b, t_q, _ = zq.shape
    l_rows = kv.shape[1]
    n_kc = l_rows // kc
    kern = functools.partial(_dsa_kernel, tq=tq, kc=kc, n_top=min(DSA_TOPK, n_keys // 4),
                             idx_bits=max(1, math.ceil(math.log2(l_rows))))
    return pl.pallas_call(
        kern,
        grid=(b, t_q // tq),
        in_specs=[pl.BlockSpec((1, tq, 2 * LANES), lambda bi, i: (bi, i, _NEW["q_a"] // 256)),
                  pl.BlockSpec((1, tq, 2 * LANES), lambda bi, i: (bi, i, _NEW["iq"] // 256)),
                  pl.BlockSpec((1, tq, LANES), lambda bi, i: (bi, i, BLK_IK)),
                  pl.BlockSpec((1, l_rows, LANES), lambda bi, i: (bi, 0, kv_blk)),
                  pl.BlockSpec((1, l_rows, LANES), lambda bi, i: (bi, 0, ik_blk))],
        out_specs=pl.BlockSpec((1, tq, 2 * LANES), lambda bi, i: (bi, i, 0)),
        out_shape=jax.ShapeDtypeStruct((b, t_q, 2 * LANES), F32),
        scratch_shapes=_dsa_scratch(n_kc, tq, kc),
        compiler_params=_params(("parallel", "arbitrary")),
        name="dsa_attention",
    )(zq, zq, zq, kv, ik)


def _compress(load_rows, wlo_ref, whi_ref, pe_ref, w2_ref, n_blocks):
    a = jnp.zeros((n_blocks, LANES), F32)
    b = jnp.zeros((n_blocks, LANES), F32)
    for p in range(NSA_CMP_STRIDE):
        xp = load_rows(p)
        a = a + _dot((xp + pe_ref[p:p + 1, :]).astype(BF16), wlo_ref[p])
        b = b + _dot((xp + pe_ref[NSA_CMP_STRIDE + p:NSA_CMP_STRIDE + p + 1, :]).astype(BF16), whi_ref[p])
    h = jax.nn.gelu(a + pltpu.roll(b, n_blocks - 1, 0))
    return _dot(h.astype(BF16), w2_ref[...])


def _cmp_kernel(rows_ref, wlo_ref, whi_ref, pe_ref, w2_ref, o_ref, *, n_blocks):
    o_ref[0] = _compress(lambda p: rows_ref[0, pl.ds(p, n_blocks, stride=NSA_CMP_STRIDE), :],
                         wlo_ref, whi_ref, pe_ref, w2_ref, n_blocks)


def _block_diag2(m0, m1):
    z = jnp.zeros_like(m0)
    return jnp.concatenate([jnp.concatenate([m0, z], axis=-1), jnp.concatenate([z, m1], axis=-1)], axis=-2)


def _compress_weights(w_cmp1, w_cmp2, pe_cmp):
    w1 = w_cmp1.reshape(2, NSA_CMP_LEN, HEAD_DIM, HEAD_DIM)
    wlo = _block_diag2(w1[0, :NSA_CMP_STRIDE], w1[1, :NSA_CMP_STRIDE]).astype(BF16)
    whi = _block_diag2(w1[0, NSA_CMP_STRIDE:], w1[1, NSA_CMP_STRIDE:]).astype(BF16)
    pe = jnp.concatenate([pe_cmp[0], pe_cmp[1]], axis=-1)
    w2 = _block_diag2(w_cmp2[0], w_cmp2[1]).astype(BF16)
    return wlo, whi, pe, w2


def _compress_specs(index_map3, index_map2):
    return [pl.BlockSpec((NSA_CMP_STRIDE, LANES, LANES), index_map3),
            pl.BlockSpec((NSA_CMP_STRIDE, LANES, LANES), index_map3),
            pl.BlockSpec((NSA_CMP_LEN, LANES), index_map2),
            pl.BlockSpec((LANES, LANES), index_map2)]


def _nsa_compress(rows, cmp_w, *, blk):
    b, l_rows, _ = rows.shape
    n_blocks = l_rows // NSA_CMP_STRIDE
    return pl.pallas_call(
        functools.partial(_cmp_kernel, n_blocks=n_blocks),
        grid=(b,),
        in_specs=[pl.BlockSpec((1, l_rows, LANES), lambda bi: (bi, 0, blk))]
        + _compress_specs(lambda bi: (0, 0, 0), lambda bi: (0, 0)),
        out_specs=pl.BlockSpec((1, n_blocks, LANES), lambda bi: (bi, 0, 0)),
        out_shape=jax.ShapeDtypeStruct((b, n_blocks, LANES), F32),
        compiler_params=_params(("parallel",)),
        name="nsa_compress",
    )(rows, *cmp_w)


def _nsa_core(q, g, cmp_rows, load_sel, load_selt, wrows, wpos, rank_ref, m_ref, l_ref, acc_ref,
              *, tq, kc, p0, n_cmp, n_blk, n_sel):
    pos = p0 + _lane_iota(tq)
    qs_t = (_stack_heads(q) * SCALE_LOG2).T.astype(BF16)

    ncp = cmp_rows.shape[0]
    cmp_t = cmp_rows.T.astype(BF16)
    s_c = _dot(cmp_rows.astype(BF16), qs_t)
    n_idx = _row_iota(ncp)
    valid_c = (n_idx * NSA_CMP_STRIDE + (NSA_CMP_LEN - 1) <= pos) & (n_idx < n_cmp)
    p_c = [_tmasked_softmax(s_c[:, h * tq:(h + 1) * tq], valid_c) for h in range(N_HEADS)]
    o_c = _dot(cmp_t, jnp.concatenate(p_c, axis=1).astype(BF16))
    p_sum = p_c[0] + p_c[1] + p_c[2] + p_c[3]

    nbp = rank_ref.shape[0]
    jb = lax.broadcasted_iota(jnp.int32, (nbp, ncp), 0)
    nn = lax.broadcasted_iota(jnp.int32, (nbp, ncp), 1)
    cover_t = ((nn * NSA_CMP_STRIDE < (jb + 1) * NSA_SEL_BLOCK)
               & (nn * NSA_CMP_STRIDE + (NSA_CMP_LEN - 1) >= jb * NSA_SEL_BLOCK) & (nn < n_cmp)).astype(BF16)
    hi, lo = _split_bf16(p_sum)
    imp = _dot(cover_t, hi) + _dot(cover_t, lo)
    j = _row_iota(nbp)
    cur = pos // NSA_SEL_BLOCK
    forced = (j == 0) | (j == cur) | (j == cur - 1)
    imp = jnp.where(forced, jnp.inf, jnp.where(j <= cur, imp, NEG_INF))
    n_vis = jnp.minimum((p0 + tq - 1) // NSA_SEL_BLOCK + 1, n_blk)
    sel = _select_top_t(imp, rank_ref, n_vis, n_sel).astype(BF16)

    _tsoftmax_init(m_ref, l_ref, acc_ref)
    n_chunks = (p0 + tq - 1) // kc + 1
    ek = lax.broadcasted_iota(jnp.int32, (kc, nbp), 0)
    eb = lax.broadcasted_iota(jnp.int32, (kc, nbp), 1)

    def attend(c, carry):
        s = _dot(load_sel(c), qs_t)
        expand = (eb == (c * kc + ek) // NSA_SEL_BLOCK).astype(BF16)
        picked = _dot(expand, sel)
        valid = (picked > 0.5) & (c * kc + _row_iota(kc) <= pos)
        s = jnp.concatenate([jnp.where(valid, s[:, h * tq:(h + 1) * tq], NEG_INF) for h in range(N_HEADS)], axis=1)
        _tsoftmax_step(s, load_selt(c), m_ref, l_ref, acc_ref)
        return carry

    lax.fori_loop(0, n_chunks, attend, 0)
    o_s = _tsoftmax_result(l_ref, acc_ref)

    s_w = _dot(wrows.astype(BF16), qs_t)
    valid_w = (wpos <= pos) & (pos - wpos < NSA_WINDOW)
    p_w = [_tmasked_softmax(s_w[:, h * tq:(h + 1) * tq], valid_w) for h in range(N_HEADS)]
    o_w = _dot(wrows.T.astype(BF16), jnp.concatenate(p_w, axis=1).astype(BF16))

    g_t = _rows_t(g)
    mixed = []
    for h in range(N_HEADS):
        cols = slice(h * tq, (h + 1) * tq)
        mixed.append(g_t[3 * h:3 * h + 1] * o_c[:, cols] + g_t[3 * h + 1:3 * h + 2] * o_s[:, cols]
                     + g_t[3 * h + 2:3 * h + 3] * o_w[:, cols])
    return _unstack_heads_t(jnp.concatenate(mixed, axis=1), tq)


def _nsa_kernel(q_ref, g_ref, cmp_ref, sel_ref, win_ref, o_ref, selt_ref, rank_ref, m_ref, l_ref, acc_ref,
                *, tq, kc, n_cmp, n_blk, n_sel, win_rows):
    i = pl.program_id(1)

    @pl.when(i == 0)
    def _():
        _fill_transposed(sel_ref, selt_ref, kc)

    p0 = i * tq
    start = pl.multiple_of(jnp.clip(p0 - NSA_WINDOW, 0, win_ref.shape[1] - win_rows), SUBLANES)
    o_ref[0] = _nsa_core(q_ref[0], g_ref[0], cmp_ref[0], _chunk_loader(sel_ref, kc), lambda c: selt_ref[c],
                         win_ref[0, pl.ds(start, win_rows), :], start + _row_iota(win_rows),
                         rank_ref, m_ref, l_ref, acc_ref,
                         tq=tq, kc=kc, p0=p0, n_cmp=n_cmp, n_blk=n_blk, n_sel=n_sel)


def _top_mask_row(v, n_sel):
    n = v.shape[1]
    i = lax.broadcasted_iota(jnp.int32, (n, n), 0)
    j = lax.broadcasted_iota(jnp.int32, (n, n), 1)
    v_col = jnp.sum(jnp.where(i == j, v, 0.0), axis=1, keepdims=True)
    beats = (v_col > v) | ((v_col == v) & (i < j))
    rank = jnp.sum(jnp.where(beats, 1, 0), axis=0, keepdims=True)
    return jnp.where((rank < n_sel) & (v > NEG_INF), 1.0, 0.0)


def _nsa_rows_core(q, g, cmpv, load_sel, wrows, wpos, m_ref, l_ref, acc_ref, *, tq, kc, p0, n_cmp, n_blk, n_sel):
    pos = p0 + _row_iota(tq)
    qs = (_stack_heads(q) * SCALE).astype(BF16)

    ncp = cmpv.shape[0]
    s_c = _dot_nt(qs, cmpv)
    n_idx = _lane_iota(ncp)
    valid_c = (n_idx * NSA_CMP_STRIDE + (NSA_CMP_LEN - 1) <= pos) & (n_idx < n_cmp)
    o_c = []
    p_sum = jnp.zeros((tq, ncp), F32)
    for h in range(N_HEADS):
        p = _row_masked_softmax(s_c[h * tq:(h + 1) * tq], valid_c)
        p_sum = p_sum + p
        o_c.append(_dot(p.astype(BF16), cmpv))

    nbp = -(-n_blk // LANES) * LANES
    nn = lax.broadcasted_iota(jnp.int32, (ncp, nbp), 0)
    jb = lax.broadcasted_iota(jnp.int32, (ncp, nbp), 1)
    cover = ((nn * NSA_CMP_STRIDE < (jb + 1) * NSA_SEL_BLOCK)
             & (nn * NSA_CMP_STRIDE + (NSA_CMP_LEN - 1) >= jb * NSA_SEL_BLOCK) & (nn < n_cmp)).astype(BF16)
    hi, lo = _split_bf16(p_sum)
    imp = _dot(hi, cover) + _dot(lo, cover)
    j = _lane_iota(nbp)
    cur = pos // NSA_SEL_BLOCK
    forced = (j == 0) | (j == cur) | (j == cur - 1)
    imp = jnp.where(forced, jnp.inf, jnp.where(j <= cur, imp, NEG_INF))
    sel = jnp.broadcast_to(_top_mask_row(imp[0:1], n_sel), (tq, nbp)).astype(BF16)

    _row_softmax_init(m_ref, l_ref, acc_ref)
    n_chunks = (p0 + tq - 1) // kc + 1
    eb = lax.broadcasted_iota(jnp.int32, (nbp, kc), 0)
    ek = lax.broadcasted_iota(jnp.int32, (nbp, kc), 1)

    def attend(c, carry):
        rows = load_sel(c)
        s = _dot_nt(qs, rows)
        expand = (eb == (c * kc + ek) // NSA_SEL_BLOCK).astype(BF16)
        picked = _dot(sel, expand)
        valid = (picked > 0.5) & (c * kc + _lane_iota(kc) <= pos)
        for h in range(N_HEADS):
            _row_softmax_step(pl.ds(h * tq, tq), s[h * tq:(h + 1) * tq], valid, rows, m_ref, l_ref, acc_ref)
        return carry

    lax.fori_loop(0, n_chunks, attend, 0)

    s_w = _dot_nt(qs, wrows)
    valid_w = (wpos <= pos) & (pos - wpos < NSA_WINDOW)
    heads = []
    for h in range(N_HEADS):
        p_w = _row_masked_softmax(s_w[h * tq:(h + 1) * tq], valid_w)
        o_w = _dot(p_w.astype(BF16), wrows)
        o_s = _row_softmax_result(pl.ds(h * tq, tq), l_ref, acc_ref)
        heads.append(g[:, 3 * h:3 * h + 1] * o_c[h] + g[:, 3 * h + 1:3 * h + 2] * o_s
                     + g[:, 3 * h + 2:3 * h + 3] * o_w)
    return _unstack_heads(heads)


def _nsa_decode_kernel(pt_ref, q_ref, g_ref, *rest, tq, kc, n_cmp, n_blk, n_sel, n_pages, page, pps):
    page_refs = rest[:pps]
    (new_ref, win_ref, wlo_ref, whi_ref, pe_ref, w2_ref, o_ref, cbuf_ref, sbuf_ref, m_ref, l_ref, acc_ref) = rest[pps:]
    s = pl.program_id(1)
    n_steps = n_pages // pps

    @pl.when(s < n_steps)
    def _():
        for r in range(pps):
            at = pl.ds(pl.multiple_of((s * pps + r) * page, page), page)
            cbuf_ref[at, :] = page_refs[r][0, 0, :, 0:LANES]
            sbuf_ref[at, :] = page_refs[r][0, 0, :, LANES:2 * LANES].astype(BF16)

    @pl.when(s == n_steps)
    def _():
        past = n_pages * page
        sbuf_ref[pl.ds(past, kc), :] = new_ref[0, :, LANES:2 * LANES].astype(BF16)
        n_blocks = past // NSA_CMP_STRIDE
        cmpv = _compress(lambda r: cbuf_ref[pl.ds(r, n_blocks, stride=NSA_CMP_STRIDE), :],
                         wlo_ref, whi_ref, pe_ref, w2_ref, n_blocks).astype(BF16)
        wrows = win_ref[0].astype(BF16)
        o_ref[0] = _nsa_rows_core(q_ref[0], g_ref[0], cmpv, _chunk_loader(sbuf_ref, kc), wrows,
                                  past - NSA_WINDOW + _lane_iota(wrows.shape[0]), m_ref, l_ref, acc_ref,
                                  tq=tq, kc=kc, p0=past, n_cmp=n_cmp, n_blk=n_blk, n_sel=n_sel)


def _nsa_sizes(n_keys):
    n_cmp = (n_keys - NSA_CMP_LEN) // NSA_CMP_STRIDE + 1
    n_blk = -(-n_keys // NSA_SEL_BLOCK)
    return n_cmp, n_blk, min(NSA_SEL_TOPN, n_blk), -(-n_blk // LANES) * LANES


def _nsa_scratch(n_kc, nbp, tq, kc):
    return ([pltpu.VMEM((n_kc, LANES, kc), BF16), pltpu.VMEM((nbp, tq), F32)]
            + _tsoftmax_scratch(LANES, N_HEADS * tq))


def _nsa_decode_call(zq, cache, layer, page_table, new_rows, win_rows, cmp_w, *, kc=DEC_KEY_CHUNK,
                     pps=PAGES_PER_STEP):
    b, tq, _ = zq.shape
    n_pages = page_table.shape[1]
    page = cache.shape[2]
    past = n_pages * page
    assert past % kc == 0 and n_pages % pps == 0
    n_cmp, n_blk, n_sel, _ = _nsa_sizes(past + 1)
    kern = functools.partial(_nsa_decode_kernel, tq=tq, kc=kc, n_cmp=n_cmp, n_blk=n_blk, n_sel=n_sel,
                             n_pages=n_pages, page=page, pps=pps)
    grid_spec = pltpu.PrefetchScalarGridSpec(
        num_scalar_prefetch=1,
        grid=(b, n_pages // pps + 1),
        in_specs=[pl.BlockSpec((1, tq, 2 * LANES), lambda bi, s, pt: (bi, 0, _NEW["q_b"] // 256)),
                  pl.BlockSpec((1, tq, LANES), lambda bi, s, pt: (bi, 0, BLK_GB))]
        + _page_specs(cache, layer, n_pages, pps)
        + [pl.BlockSpec((1,) + new_rows.shape[1:], lambda bi, s, pt: (bi, 0, 0)),
           pl.BlockSpec((1,) + win_rows.shape[1:], lambda bi, s, pt: (bi, 0, 0))]
        + _compress_specs(lambda bi, s, pt: (0, 0, 0), lambda bi, s, pt: (0, 0)),
        out_specs=pl.BlockSpec((1, tq, 2 * LANES), lambda bi, s, pt: (bi, 0, 0)),
        scratch_shapes=[pltpu.VMEM((past, LANES), F32), pltpu.VMEM((past + kc, LANES), BF16)]
        + _row_softmax_scratch(N_HEADS * tq, LANES))
    return pl.pallas_call(
        kern, grid_spec=grid_spec,
        out_shape=jax.ShapeDtypeStruct((b, tq, 2 * LANES), F32),
        compiler_params=_params(("parallel", "arbitrary")),
        name="nsa_decode",
    )(page_table, zq, zq, *([cache] * pps), new_rows, win_rows, *cmp_w)


def _nsa_call(zq, cmp, sel, win, *, sel_blk, win_blk, tq, n_keys, kc=KEY_CHUNK):
    b, t_q, _ = zq.shape
    l_rows = sel.shape[1]
    n_cmp, n_blk, n_sel, nbp = _nsa_sizes(n_keys)
    kern = functools.partial(_nsa_kernel, tq=tq, kc=kc, n_cmp=n_cmp, n_blk=n_blk, n_sel=n_sel,
                             win_rows=min(NSA_WINDOW + tq, l_rows))
    return pl.pallas_call(
        kern,
        grid=(b, t_q // tq),
        in_specs=[pl.BlockSpec((1, tq, 2 * LANES), lambda bi, i: (bi, i, _NEW["q_b"] // 256)),
                  pl.BlockSpec((1, tq, LANES), lambda bi, i: (bi, i, BLK_GB)),
                  pl.BlockSpec((1, cmp.shape[1], LANES), lambda bi, i: (bi, 0, 0)),
                  pl.BlockSpec((1, l_rows, LANES), lambda bi, i: (bi, 0, sel_blk)),
                  pl.BlockSpec((1, l_rows, LANES), lambda bi, i: (bi, 0, win_blk))],
        out_specs=pl.BlockSpec((1, tq, 2 * LANES), lambda bi, i: (bi, i, 0)),
        out_shape=jax.ShapeDtypeStruct((b, t_q, 2 * LANES), F32),
        scratch_shapes=_nsa_scratch(l_rows // kc, nbp, tq, kc),
        compiler_params=_params(("parallel", "arbitrary")),
        name="nsa_attention",
    )(zq, zq, cmp, sel, win)


def _moba_kernel(q_ref, k_ref, v_ref, o_ref, kmean_ref, vt_ref, rank_ref, sel_ref, m_ref, l_ref, acc_ref,
                 *, tq, n_blocks, n_sel):
    c0 = pl.program_id(2)

    @pl.when(c0 == 0)
    def _():
        kmean_ref[...] = jnp.zeros(kmean_ref.shape, F32)

        def mean_block(blk, carry):
            rows = k_ref[0, pl.ds(pl.multiple_of(blk * MOBA_BLOCK, MOBA_BLOCK), MOBA_BLOCK), :]
            kmean_ref[pl.ds(blk, 1), :] = jnp.sum(rows, axis=0, keepdims=True) * (1.0 / MOBA_BLOCK)
            return carry

        lax.fori_loop(0, n_blocks, mean_block, 0)
        _fill_transposed(v_ref, vt_ref, MOBA_BLOCK)

    nbp = kmean_ref.shape[0]
    qst = _pair_heads(q_ref[0])
    qst_t = qst.T
    q_hi, q_lo = _split_bf16(qst_t)
    k_hi, k_lo = _split_bf16(kmean_ref[...])
    gate = _dot(k_hi, q_hi) + _dot(k_hi, q_lo) + _dot(k_lo, q_hi)
    gate = jnp.where(_row_iota(nbp) < c0, gate, NEG_INF)
    sel_ref[...] = _select_top_t(gate, rank_ref, c0, n_sel)
    qs_t = (qst_t * SCALE_LOG2).astype(BF16)
    load_k = _chunk_loader(k_ref, MOBA_BLOCK)

    _tsoftmax_init(m_ref, l_ref, acc_ref)

    def attend(blk, carry):
        s = jnp.where(sel_ref[pl.ds(blk, 1), :] > 0.5, _dot(load_k(blk), qs_t), NEG_INF)
        _tsoftmax_step(s, vt_ref[blk], m_ref, l_ref, acc_ref)
        return carry

    lax.fori_loop(0, c0, attend, 0)

    ki = lax.broadcasted_iota(jnp.int32, (MOBA_BLOCK, tq), 0)
    qi = lax.broadcasted_iota(jnp.int32, (MOBA_BLOCK, tq), 1)
    s_own = _dot(load_k(c0), qs_t)
    causal = ki <= qi
    s_own = jnp.concatenate([jnp.where(causal, s_own[:, 0:tq], NEG_INF),
                             jnp.where(causal, s_own[:, tq:2 * tq], NEG_INF)], axis=1)
    _tsoftmax_step(s_own, vt_ref[c0], m_ref, l_ref, acc_ref)
    o = _tsoftmax_result(l_ref, acc_ref).T
    o_ref[0] = jnp.where(_lane_iota() < HEAD_DIM, o[0:tq], o[tq:2 * tq])


def _moba_call(zq, kv, *, tq):
    b, t_q, _ = zq.shape
    l_rows = kv.shape[1]
    assert tq == MOBA_BLOCK
    n_blocks = l_rows // MOBA_BLOCK
    nbp = -(-n_blocks // LANES) * LANES
    kern = functools.partial(_moba_kernel, tq=tq, n_blocks=n_blocks, n_sel=min(MOBA_TOPK, n_blocks))
    q_blk0 = _NEW["q_c"] // LANES
    return pl.pallas_call(
        kern,
        grid=(b, 2, t_q // tq),
        in_specs=[pl.BlockSpec((1, tq, LANES), lambda bi, c, i: (bi, i, q_blk0 + c)),
                  pl.BlockSpec((1, l_rows, LANES), lambda bi, c, i: (bi, 0, BLK_KC + c)),
                  pl.BlockSpec((1, l_rows, LANES), lambda bi, c, i: (bi, 0, BLK_VC + c))],
        out_specs=pl.BlockSpec((1, tq, LANES), lambda bi, c, i: (bi, i, c)),
        out_shape=jax.ShapeDtypeStruct((b, t_q, 2 * LANES), F32),
        scratch_shapes=[pltpu.VMEM((nbp, LANES), F32), pltpu.VMEM((n_blocks, LANES, MOBA_BLOCK), BF16),
                        pltpu.VMEM((nbp, 2 * tq), F32), pltpu.VMEM((nbp, 2 * tq), F32)]
        + _tsoftmax_scratch(LANES, 2 * tq),
        compiler_params=_params(("parallel", "parallel", "arbitrary")),
        name="moba_attention",
    )(zq, kv, kv)


def _moba_pick_kernel(pt_ref, q_ref, *rest, n_blocks, n_sel, n_pages, page, pps):
    page_refs, (o_ref, kmean_ref, rank_ref) = rest[:pps], rest[pps:]
    s = pl.program_id(1)
    per_block = MOBA_BLOCK // page
    n_steps = n_pages // pps
    nbp = kmean_ref.shape[0]

    @pl.when(s == 0)
    def _():
        kmean_ref[...] = jnp.zeros(kmean_ref.shape, F32)

    @pl.when(s < n_steps)
    def _():
        for j in range(pps // per_block):
            tot = None
            for r in range(per_block):
                part = jnp.sum(page_refs[j * per_block + r][0, 0], axis=0, keepdims=True)
                tot = part if tot is None else tot + part
            kmean_ref[pl.ds(s * (pps // per_block) + j, 1), :] = tot * (1.0 / MOBA_BLOCK)

    @pl.when(s == n_steps)
    def _():
        prod = kmean_ref[...] * q_ref[0, 0:1, :]
        seg = (lax.broadcasted_iota(jnp.int32, (N_HEADS * HEAD_DIM, LANES), 0) // HEAD_DIM
               == lax.broadcasted_iota(jnp.int32, (N_HEADS * HEAD_DIM, LANES), 1)).astype(BF16)
        hi, lo = _split_bf16(prod)
        lo2 = (prod - hi.astype(F32) - lo.astype(F32)).astype(BF16)
        gate = _dot(hi, seg) + _dot(lo, seg) + _dot(lo2, seg)
        j = _row_iota(nbp)
        gate = jnp.where(j < n_blocks, gate, NEG_INF)
        rank = _rank_rows(gate, rank_ref, n_blocks)
        rows = []
        for r in range(n_sel):
            hit = (rank == r) & (gate > NEG_INF)
            rows.append(jnp.sum(jnp.where(hit, j, 0), axis=0, keepdims=True))
        for r in range(n_sel):
            hit = (rank == r) & (gate > NEG_INF)
            rows.append(jnp.sum(jnp.where(hit, 1, 0), axis=0, keepdims=True))
        rows.append(jnp.zeros((SUBLANES - 2 * n_sel, LANES), jnp.int32))
        o_ref[0] = jnp.concatenate(rows, axis=0)


def _moba_pick_call(zq, cache, layer, page_table, *, pps=PAGES_PER_STEP):
    b, tq, _ = zq.shape
    n_pages = page_table.shape[1]
    page = cache.shape[2]
    n_blocks = n_pages * page // MOBA_BLOCK
    n_sel = min(MOBA_TOPK, n_blocks + 1)
    assert MOBA_BLOCK % page == 0 and n_pages % pps == 0 and pps % (MOBA_BLOCK // page) == 0
    assert 2 * n_sel <= SUBLANES
    nbp = -(-n_blocks // LANES) * LANES
    block = (1, 1, page, N_HEADS * HEAD_DIM)

    def spec(r):
        return pl.BlockSpec(block, lambda bi, s, pt: (layer, pt[bi, jnp.minimum(s * pps + r, n_pages - 1)], 0, 0))

    grid_spec = pltpu.PrefetchScalarGridSpec(
        num_scalar_prefetch=1,
        grid=(b, n_pages // pps + 1),
        in_specs=[pl.BlockSpec((1, tq, 2 * LANES), lambda bi, s, pt: (bi, 0, _NEW["q_c"] // 256))]
        + [spec(r) for r in range(pps)],
        out_specs=pl.BlockSpec((1, SUBLANES, LANES), lambda bi, s, pt: (bi, 0, 0)),
        scratch_shapes=[pltpu.VMEM((nbp, N_HEADS * HEAD_DIM), F32), pltpu.VMEM((nbp, LANES), F32)])
    kern = functools.partial(_moba_pick_kernel, n_blocks=n_blocks, n_sel=n_sel, n_pages=n_pages, page=page, pps=pps)
    return pl.pallas_call(
        kern, grid_spec=grid_spec,
        out_shape=jax.ShapeDtypeStruct((b, SUBLANES, LANES), jnp.int32),
        compiler_params=_params(("parallel", "arbitrary")),
        name="moba_pick",
    )(page_table, zq, *([cache] * pps)), n_sel


def _moba_gather_kernel(pid_ref, ok_ref, q_ref, *rest, tq, n_sel, per_block, page):
    n_pg = N_HEADS * per_block
    k_refs, v_refs = rest[:n_pg], rest[n_pg:2 * n_pg]
    new_ref, o_ref, m_ref, l_ref, acc_ref = rest[2 * n_pg:]
    bi, r = pl.program_id(0), pl.program_id(1)

    @pl.when(r == 0)
    def _():
        m_ref[...] = jnp.full(m_ref.shape, NEG_INF, F32)
        l_ref[...] = jnp.zeros(l_ref.shape, F32)
        acc_ref[...] = jnp.zeros(acc_ref.shape, F32)

    qs = (jnp.concatenate([_pair_heads(q_ref[0, :, c * LANES:(c + 1) * LANES]) for c in range(2)], axis=0)
          * SCALE).astype(BF16)
    for h in range(N_HEADS):
        rows = pl.ds(h * tq, tq)
        kb = jnp.concatenate([k_refs[h * per_block + j][0, 0] for j in range(per_block)], axis=0).astype(BF16)
        vb = jnp.concatenate([v_refs[h * per_block + j][0, 0] for j in range(per_block)], axis=0).astype(BF16)
        ok = ok_ref[bi, r * N_HEADS + h] > 0
        _row_softmax_step(rows, _dot_nt(qs[h * tq:(h + 1) * tq], kb), ok, vb, m_ref, l_ref, acc_ref)

    @pl.when(r == n_sel - 1)
    def _():
        ki = lax.broadcasted_iota(jnp.int32, (tq, page), 1)
        qi = lax.broadcasted_iota(jnp.int32, (tq, page), 0)
        n_k = new_ref.shape[2] // 2
        outs = []
        for h in range(N_HEADS):
            rows = pl.ds(h * tq, tq)
            c = h // 2
            kb = new_ref[0, :, c * LANES:(c + 1) * LANES].astype(BF16)
            vb = new_ref[0, :, n_k + c * LANES:n_k + (c + 1) * LANES].astype(BF16)
            _row_softmax_step(rows, _dot_nt(qs[h * tq:(h + 1) * tq], kb), ki <= qi, vb, m_ref, l_ref, acc_ref)
            outs.append(acc_ref[rows, :] / jnp.maximum(l_ref[rows, :], 1e-30))
        lane = _lane_iota()
        o_ref[0] = jnp.concatenate([jnp.where(lane < HEAD_DIM, outs[0], outs[1]),
                                    jnp.where(lane < HEAD_DIM, outs[2], outs[3])], axis=1)


def _moba_decode_call(zq, cache, layer, page_table, new_page):
    b, tq, _ = zq.shape
    page = cache.shape[2]
    per_block = MOBA_BLOCK // page
    picks, n_sel = _moba_pick_call(zq, cache, layer, page_table)
    blk = picks[:, :n_sel, :N_HEADS]
    ok = picks[:, n_sel:2 * n_sel, :N_HEADS].reshape(b, n_sel * N_HEADS)
    logical = blk[..., None] * per_block + jnp.arange(per_block, dtype=jnp.int32)
    pid = jnp.take_along_axis(page_table, logical.reshape(b, -1), axis=1)
    block = (1, 1, page, LANES)

    def spec(h, j, field):
        return pl.BlockSpec(block, lambda bi, r, pid_ref, ok_ref:
                            (layer, pid_ref[bi, (r * N_HEADS + h) * per_block + j], 0, 2 * field + h // 2))

    hj = [(h, j) for h in range(N_HEADS) for j in range(per_block)]
    grid_spec = pltpu.PrefetchScalarGridSpec(
        num_scalar_prefetch=2,
        grid=(b, n_sel),
        in_specs=[pl.BlockSpec((1, tq, 2 * LANES), lambda bi, r, pid_ref, ok_ref: (bi, 0, _NEW["q_c"] // 256))]
        + [spec(h, j, 0) for h, j in hj] + [spec(h, j, 1) for h, j in hj]
        + [pl.BlockSpec((1,) + new_page.shape[1:], lambda bi, r, pid_ref, ok_ref: (bi, 0, 0))],
        out_specs=pl.BlockSpec((1, tq, 2 * LANES), lambda bi, r, pid_ref, ok_ref: (bi, 0, 0)),
        scratch_shapes=_row_softmax_scratch(N_HEADS * tq, LANES))
    kern = functools.partial(_moba_gather_kernel, tq=tq, n_sel=n_sel, per_block=per_block, page=page)
    return pl.pallas_call(
        kern, grid_spec=grid_spec,
        out_shape=jax.ShapeDtypeStruct((b, tq, 2 * LANES), F32),
        compiler_params=_params(("parallel", "arbitrary")),
        name="moba_decode",
    )(pid, ok, zq, *([cache] * (2 * len(hj))), new_page)


def _memory_kv(mem, g_ln, w_kv, g_k):
    b, m_rows, d = mem.shape
    n = w_kv.shape[1]
    half = n // 2
    mask = np.zeros((n,), np.float32)
    mask[:half] = 1
    zero = jnp.zeros((n,), F32)
    gain = jnp.concatenate([jnp.tile(g_k.astype(F32), half // HEAD_DIM), jnp.ones((half,), F32)])
    cfg = jnp.stack([jnp.asarray(mask), gain, zero, zero, zero, zero, zero, zero])
    masks = dict(norm=mask, rope=np.zeros_like(mask), sig=np.zeros_like(mask))
    rope = jnp.zeros((m_rows, 3 * LANES), F32)
    out = _project(mem.reshape(b * m_rows, d), g_ln.reshape(1, d), w_kv.astype(BF16), cfg, rope, masks,
                   tm=m_rows, tn=n)
    return out.reshape(b, m_rows, n)


def _mem_kernel(q_ref, kv_ref, o_ref, *, tq):
    lane = _lane_iota()
    n_kv = kv_ref.shape[2] // 2
    chunks = []
    for c in range(2):
        qst = _pair_heads(q_ref[0, :, c * LANES:(c + 1) * LANES])
        kb = kv_ref[0, :, c * LANES:(c + 1) * LANES].astype(BF16)
        vb = kv_ref[0, :, n_kv + c * LANES:n_kv + (c + 1) * LANES].astype(BF16)
        s = _dot_nt((qst * SCALE).astype(BF16), kb)
        e = jnp.exp(s - jnp.max(s, axis=-1, keepdims=True))
        p = e / jnp.sum(e, axis=-1, keepdims=True)
        o = _dot(p.astype(BF16), vb)
        chunks.append(jnp.where(lane < HEAD_DIM, o[0:tq], o[tq:2 * tq]))
    o_ref[0] = jnp.concatenate(chunks, axis=1)


def _mem_call(zq, mkv, *, tq):
    b, t_q, _ = zq.shape
    tq = min(tq, t_q)
    return pl.pallas_call(
        functools.partial(_mem_kernel, tq=tq),
        grid=(b, t_q // tq),
        in_specs=[pl.BlockSpec((1, tq, 2 * LANES), lambda bi, i: (bi, i, _NEW["q_m"] // 256)),
                  pl.BlockSpec((1,) + mkv.shape[1:], lambda bi, i: (bi, 0, 0))],
        out_specs=pl.BlockSpec((1, tq, 2 * LANES), lambda bi, i: (bi, i, 0)),
        out_shape=jax.ShapeDtypeStruct((b, t_q, 2 * LANES), F32),
        compiler_params=_params(("parallel", "parallel")),
        name="mem_attention",
    )(zq, mkv)


def _combine_kernel(x_ref, oa_ref, ob_ref, oc_ref, om_ref, gate_ref, wb_ref, wo_ref, y_ref):
    d = x_ref.shape[-1]
    h = None
    for bi, o_ref in enumerate((oa_ref, ob_ref, oc_ref, om_ref)):
        t = gate_ref[:, bi * d:(bi + 1) * d] * _dot(o_ref[...].astype(BF16), wb_ref[bi])
        h = t if h is None else h + t
    y_ref[...] = x_ref[...] + _dot(h.astype(BF16), wo_ref[...])


def _combine(x2d, outs, gate, w_branch, w_out, *, tm):
    m, d = x2d.shape
    tm = min(tm, m)
    bw = outs[0].shape[-1]
    o_spec = pl.BlockSpec((tm, bw), lambda i: (i, 0))
    return pl.pallas_call(
        _combine_kernel,
        grid=(m // tm,),
        in_specs=[pl.BlockSpec((tm, d), lambda i: (i, 0)), o_spec, o_spec, o_spec, o_spec,
                  pl.BlockSpec((tm, N_BRANCH * d), lambda i: (i, 0)),
                  pl.BlockSpec((N_BRANCH, bw, d), lambda i: (0, 0, 0)),
                  pl.BlockSpec((d, d), lambda i: (0, 0))],
        out_specs=pl.BlockSpec((tm, d), lambda i: (i, 0)),
        out_shape=jax.ShapeDtypeStruct((m, d), F32),
        compiler_params=_params(("parallel",)),
        name="branch_mix",
    )(x2d, *outs, gate, w_branch, w_out)


FF_CHUNK = 256
HALO = 16


def _rms(x, g):
    return x * lax.rsqrt(jnp.mean(x * x, axis=-1, keepdims=True) + RMS_EPS) * g


def _conv3(cw, u2, u1, u0):
    return cw[3:4] + cw[0:1] * u2 + cw[1:2] * u1 + cw[2:3] * u0


def _ffn_kernel(x_ref, xh_ref, ha_ref, hb_ref, g_ref, wa_ref, wb_ref, cwa_ref, cwb_ref, wdn_ref,
                y_ref, sta_ref, stb_ref, xn_ref, xhn_ref, acc_ref, *, tm):
    i = pl.program_id(1)
    j = pl.program_id(2)

    @pl.when(j == 0)
    def _():
        xn_ref[...] = _rms(x_ref[0], g_ref[...]).astype(BF16)
        xhn_ref[...] = _rms(xh_ref[0], g_ref[...]).astype(BF16)
        acc_ref[...] = jnp.zeros(acc_ref.shape, F32)

    def half(w_ref, hist_ref, cw_ref, st_ref):
        u = _dot(xn_ref[...], w_ref[...])
        u_prev = _dot(xhn_ref[...], w_ref[...])[HALO - SUBLANES:HALO]
        prev = jnp.where(i == 0, hist_ref[0], u_prev)
        ext = jnp.concatenate([prev, u], axis=0)
        st_ref[0, 0] = ext[tm:tm + SUBLANES]
        return _conv3(cw_ref[...], pltpu.roll(ext, 2, 0)[SUBLANES:], pltpu.roll(ext, 1, 0)[SUBLANES:], u)

    a = half(wa_ref, ha_ref, cwa_ref, sta_ref)
    b = half(wb_ref, hb_ref, cwb_ref, stb_ref)
    acc_ref[...] += _dot((a * jax.nn.sigmoid(a) * b).astype(BF16), wdn_ref[...])

    @pl.when(j == pl.num_programs(2) - 1)
    def _():
        y_ref[0] = x_ref[0] + acc_ref[...]


def _conv_table(conv_w, conv_b):
    return jnp.concatenate([conv_w, conv_b[None, :], jnp.zeros((SUBLANES - CONV_WIDTH - 1, conv_b.shape[0]), F32)])


def _conv_ffn(x, hist, g, w_up, conv_w, conv_b, w_down, *, tm):
    b, t, d = x.shape
    d_ff = w_down.shape[0]
    n_j = d_ff // FF_CHUNK
    tm = min(tm, t)
    cw = _conv_table(conv_w, conv_b)
    hist8 = jnp.concatenate([jnp.zeros((b, SUBLANES - 2, 2 * d_ff), F32), hist], axis=1)
    a_col = lambda bi, i, j: (0, j)
    b_col = lambda bi, i, j: (0, n_j + j)
    st_spec = pl.BlockSpec((1, 1, SUBLANES, FF_CHUNK), lambda bi, i, j: (bi, i, 0, j))
    st_shape = jax.ShapeDtypeStruct((b, t // tm, SUBLANES, d_ff), F32)
    y, st_a, st_b = pl.pallas_call(
        functools.partial(_ffn_kernel, tm=tm),
        grid=(b, t // tm, n_j),
        in_specs=[pl.BlockSpec((1, tm, d), lambda bi, i, j: (bi, i, 0)),
                  pl.BlockSpec((1, HALO, d), lambda bi, i, j: (bi, jnp.maximum(i * (tm // HALO) - 1, 0), 0)),
                  pl.BlockSpec((1, SUBLANES, FF_CHUNK), lambda bi, i, j: (bi, 0, j)),
                  pl.BlockSpec((1, SUBLANES, FF_CHUNK), lambda bi, i, j: (bi, 0, n_j + j)),
                  pl.BlockSpec((1, d), lambda bi, i, j: (0, 0)),
                  pl.BlockSpec((d, FF_CHUNK), a_col), pl.BlockSpec((d, FF_CHUNK), b_col),
                  pl.BlockSpec((SUBLANES, FF_CHUNK), a_col), pl.BlockSpec((SUBLANES, FF_CHUNK), b_col),
                  pl.BlockSpec((FF_CHUNK, d), lambda bi, i, j: (j, 0))],
        out_specs=[pl.BlockSpec((1, tm, d), lambda bi, i, j: (bi, i, 0)), st_spec, st_spec],
        out_shape=[jax.ShapeDtypeStruct((b, t, d), F32), st_shape, st_shape],
        scratch_shapes=[pltpu.VMEM((tm, d), BF16), pltpu.VMEM((HALO, d), BF16), pltpu.VMEM((tm, d), F32)],
        compiler_params=_params(("parallel", "arbitrary", "arbitrary")),
        name="conv_ffn",
    )(x, x, hist8, hist8, g.reshape(1, d), w_up, w_up, cw, cw, w_down)
    return y, jnp.concatenate([st_a[:, -1, SUBLANES - 2:], st_b[:, -1, SUBLANES - 2:]], axis=-1)


def _ffn_row_kernel(x_ref, h0a_ref, h0b_ref, h1a_ref, h1b_ref, g_ref, wa_ref, wb_ref, cwa_ref, cwb_ref, wdn_ref,
                    y_ref, ua_ref, ub_ref, xn_ref, acc_ref):
    j = pl.program_id(0)

    @pl.when(j == 0)
    def _():
        xn_ref[...] = _rms(x_ref[...], g_ref[...]).astype(BF16)
        acc_ref[...] = jnp.zeros(acc_ref.shape, F32)

    ua = _dot(xn_ref[...], wa_ref[...])
    ub = _dot(xn_ref[...], wb_ref[...])
    ua_ref[...] = ua
    ub_ref[...] = ub
    a = _conv3(cwa_ref[...], h0a_ref[...], h1a_ref[...], ua)
    b = _conv3(cwb_ref[...], h0b_ref[...], h1b_ref[...], ub)
    acc_ref[...] += _dot((a * jax.nn.sigmoid(a) * b).astype(BF16), wdn_ref[...])

    @pl.when(j == pl.num_programs(0) - 1)
    def _():
        y_ref[...] = x_ref[...] + acc_ref[...]


def _conv_ffn_rows(x2d, hist, g, w_up, conv_w, conv_b, w_down):
    b, d = x2d.shape
    d_ff = w_down.shape[0]
    n_j = d_ff // FF_CHUNK
    cw = _conv_table(conv_w, conv_b)
    h0, h1 = hist[:, 0], hist[:, 1]
    a_col = lambda j: (0, j)
    b_col = lambda j: (0, n_j + j)
    row_a, row_b = pl.BlockSpec((b, FF_CHUNK), a_col), pl.BlockSpec((b, FF_CHUNK), b_col)
    y, ua, ub = pl.pallas_call(
        _ffn_row_kernel,
        grid=(n_j,),
        in_specs=[pl.BlockSpec((b, d), lambda j: (0, 0)), row_a, row_b, row_a, row_b,
                  pl.BlockSpec((1, d), lambda j: (0, 0)),
                  pl.BlockSpec((d, FF_CHUNK), a_col), pl.BlockSpec((d, FF_CHUNK), b_col),
                  pl.BlockSpec((SUBLANES, FF_CHUNK), a_col), pl.BlockSpec((SUBLANES, FF_CHUNK), b_col),
                  pl.BlockSpec((FF_CHUNK, d), lambda j: (j, 0))],
        out_specs=[pl.BlockSpec((b, d), lambda j: (0, 0)), row_a, row_a],
        out_shape=[jax.ShapeDtypeStruct((b, d), F32), jax.ShapeDtypeStruct((b, d_ff), F32),
                   jax.ShapeDtypeStruct((b, d_ff), F32)],
        scratch_shapes=[pltpu.VMEM((b, d), BF16), pltpu.VMEM((b, d), F32)],
        compiler_params=_params(("arbitrary",)),
        name="conv_ffn_rows",
    )(x2d, h0, h0, h1, h1, g.reshape(1, d), w_up, w_up, cw, cw, w_down)
    return y, jnp.stack([h1, jnp.concatenate([ua, ub], axis=-1)], axis=1)


def _cols(zh, name, width):
    return zh[..., _NEW[name]:_NEW[name] + width]


def _new_rows(zh):
    b, t, _ = zh.shape
    dsa = jnp.concatenate([_cols(zh, "k_a", 2 * HEAD_DIM), _cols(zh, "ik", HEAD_DIM)], axis=-1)
    nsa = _cols(zh, "kc", 4 * HEAD_DIM)
    moba = _cols(zh, "k_c", 2 * N_HEADS * HEAD_DIM)
    win = _cols(zh, "kw", 2 * HEAD_DIM)
    return (dsa.reshape(b, t, 3, HEAD_DIM), nsa.reshape(b, t, 4, HEAD_DIM),
            moba.reshape(b, t, 2, N_HEADS, HEAD_DIM), win.reshape(b, t, 2, HEAD_DIM))


def _prompt_layer(x, mem, p):
    b, t, d = x.shape
    zh2d, gate = _projections(x.reshape(b * t, d), jnp.arange(t, dtype=jnp.int32), p["ln"][0], p["w_heads"],
                              p["cfg"], p["w_gate"], tm=256)
    zh = zh2d.reshape(b, t, N_HEADCOLS)
    o_a = _dsa_call(zh, zh, zh, kv_blk=BLK_KV_A, ik_blk=BLK_IK, tq=128, n_keys=t)
    cmp = _nsa_compress(zh, p["cmp_w"], blk=BLK_CMP)
    o_b = _nsa_call(zh, cmp, zh, zh, sel_blk=BLK_SEL, win_blk=BLK_WIN, tq=128, n_keys=t)
    o_c = _moba_call(zh, zh, tq=MOBA_BLOCK)
    mkv = _memory_kv(mem, p["ln"][2], p["w_mem_kv"], p["g_mem"][1])
    o_m = _mem_call(zh, mkv, tq=256)
    outs = [o.reshape(b * t, o.shape[-1]) for o in (o_a, o_b, o_c, o_m)]
    x1 = _combine(x.reshape(b * t, d), outs, gate, p["w_branch"], p["w_out"], tm=512).reshape(b, t, d)
    hist = jnp.zeros((b, CONV_WIDTH - 1, p["w_up"].shape[1]), F32)
    y, conv = _conv_ffn(x1, hist, p["ln"][1], p["w_up"], p["conv_w"], p["conv_b"], p["w_down"], tm=1024)
    dsa, nsa, moba, win = _new_rows(zh)
    keep = min(NSA_WINDOW, t)
    return y, dsa, nsa, moba, win[:, t - keep:], mkv.reshape(b, mkv.shape[1], 2, N_HEADS, HEAD_DIM), conv


def _first_row(x, n):
    return jnp.pad(x, ((0, 0), (0, n - 1)) + ((0, 0),) * (x.ndim - 2))


def _sample_layer(x, layer, caches, page_table, win_state, mem_kv, conv_hist, p):
    b, _, d = x.shape
    cache_dsa, cache_nsa, cache_moba = caches
    page = cache_dsa.shape[2]
    past = page_table.shape[1] * page
    x2d = x.reshape(b, d)
    zh, gate = _projections(x2d, jnp.full((b,), past, jnp.int32), p["ln"][0], p["w_heads"], p["cfg"],
                            p["w_gate"], tm=b)
    zq = _first_row(zh[:, None, :], DEC_ROWS)
    dsa, nsa, moba, win = _new_rows(zh[:, None, :])
    o_a = _dsa_decode_call(zq, cache_dsa, layer, page_table, _first_row(dsa.reshape(b, 1, -1), DEC_KEY_CHUNK))
    win_all = jnp.concatenate([win_state.reshape(b, -1, 2 * HEAD_DIM), win.reshape(b, 1, 2 * HEAD_DIM)], axis=1)
    w_pad = -(-win_all.shape[1] // LANES) * LANES
    win_rows = jnp.pad(win_all, ((0, 0), (0, w_pad - win_all.shape[1]), (0, 0)))
    o_b = _nsa_decode_call(zq, cache_nsa, layer, page_table, _first_row(nsa.reshape(b, 1, -1), DEC_KEY_CHUNK),
                           win_rows, p["cmp_w"])
    o_c = _moba_decode_call(zq, cache_moba, layer, page_table, _first_row(moba.reshape(b, 1, -1), page))
    o_m = _mem_call(zq, mem_kv.reshape(b, mem_kv.shape[1], -1), tq=DEC_ROWS)
    outs = [o[:, 0, :] for o in (o_a, o_b, o_c, o_m)]
    x1 = _combine(x2d, outs, gate, p["w_branch"], p["w_out"], tm=b)
    y, conv = _conv_ffn_rows(x1, conv_hist, p["ln"][1], p["w_up"], p["conv_w"], p["conv_b"], p["w_down"])
    keep = win_state.shape[1]
    win_new = win_all[:, win_all.shape[1] - keep:].reshape(b, keep, 2, HEAD_DIM)
    return y.reshape(b, 1, d), dsa, nsa, moba, win_new, conv


def kernel(x_prompt, x_sample, cache_dsa, cache_nsa, cache_moba, state_nsa_win, cache_mem, state_ffn_conv,
           page_table, mem_prompt, ln, w_in, g_dsa, g_nsa, g_moba, g_mem, w_mem_kv, w_cmp1, w_cmp2, pe_cmp,
           w_branch, w_out, w_up, conv_w, conv_b, w_down):
    depth = ln.shape[0]
    caches = tuple(c.reshape(*c.shape[:3], -1) for c in (cache_dsa, cache_nsa, cache_moba))
    xp, xs = x_prompt, x_sample
    outs_p = [[] for _ in range(6)]
    outs_s = [[] for _ in range(5)]
    for l in range(depth):
        w_heads, cfg = _head_weights(w_in[l], g_dsa[l], g_nsa[l], g_moba[l], g_mem[l])
        p = dict(ln=ln[l], w_heads=w_heads, cfg=cfg, w_gate=w_in[l][:, GATE_ORIG:].astype(BF16), g_mem=g_mem[l],
                 w_mem_kv=w_mem_kv[l], cmp_w=_compress_weights(w_cmp1[l], w_cmp2[l], pe_cmp[l]),
                 w_branch=w_branch[l].astype(BF16), w_out=w_out[l].astype(BF16), w_up=w_up[l].astype(BF16),
                 conv_w=conv_w[l], conv_b=conv_b[l], w_down=w_down[l].astype(BF16))
        xp, *rest = _prompt_layer(xp, mem_prompt, p)
        for acc, r in zip(outs_p, rest):
            acc.append(r)
        xs, *rest = _sample_layer(xs, l, caches, page_table, state_nsa_win[l], cache_mem[l], state_ffn_conv[l], p)
        for acc, r in zip(outs_s, rest):
            acc.append(r)
    dsa_p, nsa_p, moba_p, win_p, memkv_p, conv_p = [jnp.stack(a) for a in outs_p]
    dsa_s, nsa_s, moba_s, win_s, conv_s = [jnp.stack(a) for a in outs_s]
    return (xp, xs, dsa_p, dsa_s, nsa_p, nsa_s, moba_p, moba_s, win_p, win_s, memkv_p, conv_p, conv_s)
```

```python
import functools
import math

import numpy as np
import jax
import jax.numpy as jnp
from jax import lax
from jax.experimental import pallas as pl
from jax.experimental.pallas import tpu as pltpu

HEAD_DIM = 64
ROPE_DIM = HEAD_DIM // 4
ROPE_THETA = 500000.0
N_HEADS = 4
DSA_TOPK = 256
NSA_CMP_LEN = 32
NSA_CMP_STRIDE = 16
NSA_SEL_BLOCK = 64
NSA_SEL_TOPN = 16
NSA_WINDOW = 512
MOBA_BLOCK = 256
MOBA_TOPK = 3
N_BRANCH = 4
CONV_WIDTH = 3
RMS_EPS = 1e-6

LANES = 128
SUBLANES = 8
VMEM_LIMIT = 56 * 1024 * 1024
DEC_ROWS = SUBLANES
KEY_CHUNK = 512
DEC_KEY_CHUNK = 2048
PAGES_PER_STEP = 8

F32 = jnp.float32
BF16 = jnp.bfloat16
NEG_INF = float("-inf")
SCALE = HEAD_DIM ** -0.5

KEY_NEG_INF = int(np.uint32(0xFF800000) ^ np.uint32(0x7FFFFFFF)) - 2 ** 32
KEY_POS_INF = 0x7F800000
INT_MIN = -2 ** 31
STRIP = 32
GROUP = 128

_NT = (((1,), (1,)), ((), ()))

_ORIG = dict(q_a=0, k_a=256, v_a=320, iq=384, ik=640, iw=704, q_b=708, kc=964, vc=1028, ks=1092, vs=1156,
             kw=1220, vw=1284, g_b=1348, q_c=1360, k_c=1616, v_c=1872, q_m=2128)
GATE_ORIG = 2384
_NEW = dict(q_a=0, iq=256, q_b=512, q_c=768, q_m=1024, k_c=1280, v_c=1536, k_a=1792, v_a=1856, ik=1920,
            iw=1984, kc=2048, vc=2112, ks=2176, vs=2240, kw=2304, vw=2368, g_b=2432)
_WIDTH = dict(q_a=256, iq=256, q_b=256, q_c=256, q_m=256, k_c=256, v_c=256, k_a=64, v_a=64, ik=64, iw=4,
              kc=64, vc=64, ks=64, vs=64, kw=64, vw=64, g_b=12)
N_HEADCOLS = 2560
_NORMED = ("q_a", "q_b", "q_c", "q_m", "k_c", "k_a", "kc", "ks", "kw")
_ROPED = ("q_a", "iq", "q_b", "q_c", "k_c", "k_a", "ik", "kc", "ks", "kw")
_SIGMOID = ("g_b",)

BLK_KV_A, BLK_IK, BLK_CMP, BLK_SEL, BLK_WIN, BLK_GB = 14, 15, 16, 17, 18, 19
BLK_KC, BLK_VC = 10, 12


def _head_layout():
    masks = {k: np.zeros((N_HEADCOLS,), np.float32) for k in ("norm", "rope", "sig")}
    for name, new in _NEW.items():
        w = _WIDTH[name]
        if name in _NORMED:
            masks["norm"][new:new + w] = 1
        if name in _ROPED:
            masks["rope"][new:new + w] = 1
        if name in _SIGMOID:
            masks["sig"][new:new + w] = 1
    return masks


_COL_MASKS = _head_layout()


def _chunk_any(mask):
    return tuple(bool(mask[c * LANES:(c + 1) * LANES].any()) for c in range(mask.shape[0] // LANES))


def _params(sem):
    return pltpu.CompilerParams(dimension_semantics=sem, vmem_limit_bytes=VMEM_LIMIT)


def _lane_iota(n=LANES):
    return lax.broadcasted_iota(jnp.int32, (1, n), 1)


def _row_iota(n):
    return lax.broadcasted_iota(jnp.int32, (n, 1), 0)


def _split_bf16(x):
    hi = x.astype(BF16)
    lo = (x - hi.astype(F32)).astype(BF16)
    return hi, lo


def _dot(a, b):
    return jnp.dot(a, b, preferred_element_type=F32)


def _dot_nt(a, b):
    return lax.dot_general(a, b, _NT, preferred_element_type=F32)


def _proj_kernel(x_ref, g_ref, w_ref, cfg_ref, rope_ref, o_ref, *, norm_chunks, rope_chunks, sig_chunks):
    x = x_ref[...]
    xn = x * lax.rsqrt(jnp.mean(x * x, axis=-1, keepdims=True) + RMS_EPS) * g_ref[...]
    z = _dot(xn.astype(BF16), w_ref[...])
    rr = lax.broadcasted_iota(jnp.int32, (LANES, LANES), 0) // HEAD_DIM
    cc = lax.broadcasted_iota(jnp.int32, (LANES, LANES), 1) // HEAD_DIM
    seg = (rr == cc).astype(BF16)
    for c in range(z.shape[1] // LANES):
        sl = slice(c * LANES, (c + 1) * LANES)
        zc = z[:, sl]
        if norm_chunks[c]:
            hi, lo = _split_bf16(zc * zc)
            ss = _dot(hi, seg) + _dot(lo, seg)
            r = lax.rsqrt(ss * (1.0 / HEAD_DIM) + RMS_EPS)
            zc = jnp.where(cfg_ref[0:1, sl] > 0, zc * r * cfg_ref[1:2, sl], zc)
        if rope_chunks[c]:
            rot = (zc * rope_ref[:, 0:LANES]
                   + pltpu.roll(zc, LANES - ROPE_DIM // 2, 1) * rope_ref[:, LANES:2 * LANES]
                   + pltpu.roll(zc, ROPE_DIM // 2, 1) * rope_ref[:, 2 * LANES:3 * LANES])
            zc = jnp.where(cfg_ref[2:3, sl] > 0, rot, zc)
        if sig_chunks[c]:
            zc = jnp.where(cfg_ref[3:4, sl] > 0, jax.nn.sigmoid(zc), zc)
        o_ref[:, sl] = zc


def _project(x2d, g, w_bf16, cfg, rope, masks, *, tm, tn):
    m, d = x2d.shape
    n = w_bf16.shape[1]
    tn = min(tn, n)
    tm = min(tm, m)
    flags = {k: _chunk_any(v) for k, v in masks.items()}
    per_tile = tn // LANES
    for k, v in flags.items():
        assert all(v[t * per_tile:(t + 1) * per_tile] == v[:per_tile] for t in range(n // tn)), k
    rope_tiles = rope.shape[0] // tm
    kern = functools.partial(_proj_kernel, norm_chunks=flags["norm"][:per_tile],
                             rope_chunks=flags["rope"][:per_tile], sig_chunks=flags["sig"][:per_tile])
    return pl.pallas_call(
        kern,
        grid=(m // tm, n // tn),
        in_specs=[pl.BlockSpec((tm, d), lambda i, j: (i, 0)),
                  pl.BlockSpec((1, d), lambda i, j: (0, 0)),
                  pl.BlockSpec((d, tn), lambda i, j: (0, j)),
                  pl.BlockSpec((SUBLANES, tn), lambda i, j: (0, j)),
                  pl.BlockSpec((tm, 3 * LANES), lambda i, j: (i % rope_tiles, 0))],
        out_specs=pl.BlockSpec((tm, tn), lambda i, j: (i, j)),
        out_shape=jax.ShapeDtypeStruct((m, n), F32),
        compiler_params=_params(("parallel", "arbitrary")),
        name="project",
    )(x2d, g, w_bf16, cfg, rope)


def _rope_table(pos):
    half = ROPE_DIM // 2
    inv_freq = ROPE_THETA ** (-jnp.arange(half, dtype=F32) / half)
    ang = pos.astype(F32)[:, None] * inv_freq[None, :]
    cos, sin = jnp.cos(ang), jnp.sin(ang)
    t = pos.shape[0]
    ones = jnp.ones((t, HEAD_DIM - ROPE_DIM), F32)
    zeros = jnp.zeros((t, HEAD_DIM - ROPE_DIM), F32)
    zh = jnp.zeros((t, half), F32)
    c64 = jnp.concatenate([cos, cos, ones], axis=1)
    s1 = jnp.concatenate([-sin, zh, zeros], axis=1)
    s2 = jnp.concatenate([zh, sin, zeros], axis=1)
    return jnp.concatenate([c64, c64, s1, s1, s2, s2], axis=1)


def _head_weights(w_in_l, g_dsa, g_nsa, g_moba, g_mem):
    d = w_in_l.shape[0]
    pieces, at = [], 0
    for name, new in sorted(_NEW.items(), key=lambda kv: kv[1]):
        if new > at:
            pieces.append(jnp.zeros((d, new - at), w_in_l.dtype))
        pieces.append(w_in_l[:, _ORIG[name]:_ORIG[name] + _WIDTH[name]])
        at = new + _WIDTH[name]
    pieces.append(jnp.zeros((d, N_HEADCOLS - at), w_in_l.dtype))
    w = jnp.concatenate(pieces, axis=1).astype(BF16)
    gain = jnp.ones((N_HEADCOLS,), F32)
    for name, gvec in (("q_a", g_dsa[0]), ("k_a", g_dsa[1]), ("q_b", g_nsa[0]), ("kc", g_nsa[1]),
                       ("ks", g_nsa[2]), ("kw", g_nsa[3]), ("q_c", g_moba[0]), ("k_c", g_moba[1]),
                       ("q_m", g_mem[0])):
        reps = _WIDTH[name] // HEAD_DIM
        gain = lax.dynamic_update_slice(gain, jnp.tile(gvec.astype(F32), reps), (_NEW[name],))
    zero = jnp.zeros((N_HEADCOLS,), F32)
    cfg = jnp.stack([jnp.asarray(_COL_MASKS["norm"]), gain, jnp.asarray(_COL_MASKS["rope"]),
                     jnp.asarray(_COL_MASKS["sig"]), zero, zero, zero, zero])
    return w, cfg


def _gate_cfg(n):
    z = jnp.zeros((n,), F32)
    o = jnp.ones((n,), F32)
    return jnp.stack([z, o, z, o, z, z, z, z])


def _projections(x2d, pos, ln0, w_heads, cfg, w_gate, *, tm):
    d = x2d.shape[1]
    rope = _rope_table(pos)
    g = ln0.reshape(1, d)
    zh = _project(x2d, g, w_heads, cfg, rope, _COL_MASKS, tm=tm, tn=N_HEADCOLS)
    n_g = w_gate.shape[1]
    gmask = dict(norm=np.zeros((n_g,), np.float32), rope=np.zeros((n_g,), np.float32),
                 sig=np.ones((n_g,), np.float32))
    gate = _project(x2d, g, w_gate, _gate_cfg(n_g), rope, gmask, tm=tm, tn=1024)
    return zh, gate


def _stack_heads(x256):
    lane = _lane_iota()
    parts = []
    for c in range(2):
        ch = x256[:, c * LANES:(c + 1) * LANES]
        parts.append(jnp.where(lane < HEAD_DIM, ch, 0.0))
        parts.append(jnp.where(lane < HEAD_DIM, pltpu.roll(ch, HEAD_DIM, 1), 0.0))
    return jnp.concatenate(parts, axis=0)


def _unstack_heads_t(o_t, tq):
    o = o_t.T
    lane = _lane_iota()
    chunks = []
    for c in range(2):
        even, odd = o[2 * c * tq:(2 * c + 1) * tq], o[(2 * c + 1) * tq:(2 * c + 2) * tq]
        chunks.append(jnp.where(lane < HEAD_DIM, pltpu.roll(even, HEAD_DIM, 1), odd))
    return jnp.concatenate(chunks, axis=1)


def _pair_heads(q128):
    lane = _lane_iota()
    return jnp.concatenate([jnp.where(lane < HEAD_DIM, q128, 0.0), jnp.where(lane >= HEAD_DIM, q128, 0.0)], axis=0)


def _rows_t(x):
    tq = x.shape[0]
    if tq < LANES:
        x = jnp.concatenate([x, jnp.zeros((LANES - tq, x.shape[1]), x.dtype)], axis=0)
    return x.T[:, 0:tq]


def _tile_lanes(x, n):
    return jnp.concatenate([x] * n, axis=1)


def _tsoftmax_init(m_ref, l_ref, acc_ref):
    m_ref[...] = jnp.full(m_ref.shape, NEG_INF, F32)
    l_ref[...] = jnp.zeros(l_ref.shape, F32)
    acc_ref[...] = jnp.zeros(acc_ref.shape, F32)


def _attend_loop(n_chunks, score_fn, mask_fn, load_vt, s_ref, p_ref, m_ref, l_ref, acc_ref, *, reps):
    kc, r_cols = p_ref.shape
    w = r_cols // reps
    _tsoftmax_init(m_ref, l_ref, acc_ref)

    def body(c, carry):
        strip_mask = mask_fn(c)
        mx = jnp.full((STRIP, r_cols), NEG_INF, F32)
        for g0 in range(0, kc, GROUP):
            sg = score_fn(c, g0, GROUP)
            for r0 in range(g0, g0 + GROUP, STRIP):
                s = sg[r0 - g0:r0 - g0 + STRIP]
                ok = strip_mask(r0, STRIP)
                s = jnp.concatenate([jnp.where(ok, s[:, k * w:(k + 1) * w], NEG_INF) for k in range(reps)], axis=1)
                s_ref[r0:r0 + STRIP, :] = s
                mx = jnp.maximum(mx, s)
        m_old = m_ref[...]
        m_new = jnp.maximum(m_old, jnp.max(mx, axis=0, keepdims=True))
        m_safe = jnp.where(m_new == NEG_INF, 0.0, m_new)
        tot = jnp.zeros((STRIP, r_cols), F32)
        for r0 in range(0, kc, STRIP):
            p = jnp.exp(s_ref[r0:r0 + STRIP, :] - m_safe)
            tot = tot + p
            p_ref[r0:r0 + STRIP, :] = p.astype(BF16)
        alpha = jnp.exp(m_old - m_safe)
        l_ref[...] = alpha * l_ref[...] + jnp.sum(tot, axis=0, keepdims=True)
        acc_ref[...] = alpha * acc_ref[...] + _dot(load_vt(c), p_ref[...])
        m_ref[...] = m_new
        return carry

    lax.fori_loop(0, n_chunks, body, 0)


def _tsoftmax_result(l_ref, acc_ref):
    return acc_ref[...] / jnp.maximum(l_ref[...], 1e-30)


def _tmasked_softmax(s_t, valid_t):
    s = jnp.where(valid_t, s_t, NEG_INF)
    m = jnp.max(s, axis=0, keepdims=True)
    m = jnp.where(m == NEG_INF, 0.0, m)
    e = jnp.exp(s - m)
    return e / jnp.maximum(jnp.sum(e, axis=0, keepdims=True), 1e-30)


def _attend_scratch(d, r, kc):
    return [pltpu.VMEM((kc, r), F32), pltpu.VMEM((kc, r), BF16),
            pltpu.VMEM((1, r), F32), pltpu.VMEM((1, r), F32), pltpu.VMEM((d, r), F32)]


def _chunk_loader(ref, kc):
    def load(c, r0=0, n=kc):
        rows = pl.ds(pl.multiple_of(c * kc + r0, math.gcd(kc, n)), n)
        x = ref[0, rows, :] if len(ref.shape) == 3 else ref[rows, :]
        return x.astype(BF16)
    return load


def _fill_transposed(src_ref, dst_ref, kc):
    def body(c, carry):
        dst_ref[c] = src_ref[0, pl.ds(pl.multiple_of(c * kc, kc), kc), :].T.astype(BF16)
        return carry
    lax.fori_loop(0, dst_ref.shape[0], body, 0)


def _rank_rows(val_t, src_ref, n_rows):
    src_ref[...] = val_t
    j = _row_iota(val_t.shape[0])

    def body(i, rank):
        row = src_ref[pl.ds(i, 1), :]
        beats = (row > val_t) | ((row == val_t) & (i < j))
        return rank + jnp.where(beats, 1, 0)

    return lax.fori_loop(0, n_rows, body, jnp.zeros(val_t.shape, jnp.int32))


def _select_top_t(val_t, rank_ref, n_valid, n_sel):
    rank = _rank_rows(val_t, rank_ref, n_valid)
    return jnp.where((rank < n_sel) & (val_t > NEG_INF), 1.0, 0.0)


def _dsa_core(q, iq, iw, load_kv, load_ik, load_kvt, key_ref, s_ref, p_ref, m_ref, l_ref, acc_ref,
              *, tq, kc, n_top, p0, idx_bits):
    n_chunks = (p0 + tq - 1) // kc + 1
    pos = p0 + _lane_iota(tq)
    iqs_t = _stack_heads(iq).T.astype(BF16)
    qs_t = (_stack_heads(q) * SCALE).T.astype(BF16)
    iw_t = _rows_t(iw)
    w_rows = [iw_t[HEAD_DIM + h:HEAD_DIM + h + 1, :] for h in range(N_HEADS)]

    def chunk_pos(c):
        return c * kc + _row_iota(kc)

    def score_chunk(c, carry):
        lg = _dot(load_ik(c), iqs_t)
        sc = w_rows[0] * jnp.maximum(lg[:, 0:tq], 0.0)
        for h in range(1, N_HEADS):
            sc = sc + w_rows[h] * jnp.maximum(lg[:, h * tq:(h + 1) * tq], 0.0)
        sc = jnp.where(chunk_pos(c) <= pos, sc, NEG_INF)
        bits = pltpu.bitcast(sc, jnp.int32)
        key_ref[c] = bits ^ ((bits >> 31) & 0x7FFFFFFF)
        return carry

    lax.fori_loop(0, n_chunks, score_chunk, 0)

    def count(pred):
        def body(c, acc):
            hit = jnp.where(pred(key_ref[c], chunk_pos(c)), 1, 0)
            parts = [hit[r * SUBLANES:(r + 1) * SUBLANES] for r in range(kc // SUBLANES)]
            while len(parts) > 1:
                parts = [a + b for a, b in zip(parts[0::2], parts[1::2])]
            return acc + parts[0]
        acc = lax.fori_loop(0, n_chunks, body, jnp.zeros((SUBLANES, tq), jnp.int32))
        return jnp.sum(acc, axis=0, keepdims=True)

    def thr_bit(b, thr):
        cand = thr + jnp.left_shift(jnp.int32(1), 31 - b)
        cnt = count(lambda key, kpos: key >= cand)
        return jnp.where(cnt >= n_top, cand, thr)

    thr = lax.fori_loop(0, 32, thr_bit, jnp.full((1, tq), INT_MIN, jnp.int32))

    def last_tied():
        need = n_top - count(lambda key, kpos: key > thr)

        def idx_bit(b, last):
            cand = last + jnp.left_shift(jnp.int32(1), idx_bits - 1 - b)
            cnt = count(lambda key, kpos: (key == thr) & (kpos < cand))
            return jnp.where(cnt < need, cand, last)

        return lax.fori_loop(0, idx_bits, idx_bit, jnp.zeros((1, tq), jnp.int32))

    over = (count(lambda key, kpos: key >= thr) > n_top) & (thr > KEY_NEG_INF)
    last = lax.cond(jnp.max(jnp.where(over, 1, 0)) > 0, last_tied,
                    lambda: jnp.full((1, tq), 2 ** idx_bits, jnp.int32))
    last = jnp.where(thr > KEY_NEG_INF, last, -1)

    def selected(c):
        def strip(r0, rows):
            key = key_ref[c, r0:r0 + rows, :]
            kpos = c * kc + r0 + _row_iota(rows)
            return ((key > thr) | ((key == thr) & (kpos <= last))) & (key < KEY_POS_INF)
        return strip

    _attend_loop(n_chunks, lambda c, r0, n: _dot(load_kv(c, r0, n), qs_t), selected, load_kvt,
                 s_ref, p_ref, m_ref, l_ref, acc_ref, reps=N_HEADS)
    return _unstack_heads_t(_tsoftmax_result(l_ref, acc_ref), tq)


def _dsa_kernel(q_ref, iq_ref, iw_ref, kv_ref, ik_ref, o_ref, kvt_ref, key_ref, s_ref, p_ref, m_ref, l_ref, acc_ref,
                *, tq, kc, n_top, idx_bits):
    i = pl.program_id(1)

    @pl.when(i == 0)
    def _():
        _fill_transposed(kv_ref, kvt_ref, kc)

    o_ref[0] = _dsa_core(q_ref[0], iq_ref[0], iw_ref[0], _chunk_loader(kv_ref, kc), _chunk_loader(ik_ref, kc),
                         lambda c: kvt_ref[c], key_ref, s_ref, p_ref, m_ref, l_ref, acc_ref,
                         tq=tq, kc=kc, n_top=n_top, p0=i * tq, idx_bits=idx_bits)


def _row_softmax_step(rows, s, valid, v_bf16, m_ref, l_ref, acc_ref):
    s = jnp.where(valid, s, NEG_INF)
    m_old = m_ref[rows, :]
    m_new = jnp.maximum(m_old, jnp.max(s, axis=-1, keepdims=True))
    m_safe = jnp.where(m_new == NEG_INF, 0.0, m_new)
    p = jnp.exp(s - m_safe)
    alpha = jnp.exp(m_old - m_safe)
    l_ref[rows, :] = alpha * l_ref[rows, :] + jnp.sum(p, axis=-1, keepdims=True)
    acc_ref[rows, :] = alpha * acc_ref[rows, :] + _dot(p.astype(BF16), v_bf16)
    m_ref[rows, :] = m_new


def _row_softmax_init(m_ref, l_ref, acc_ref):
    m_ref[...] = jnp.full(m_ref.shape, NEG_INF, F32)
    l_ref[...] = jnp.zeros(l_ref.shape, F32)
    acc_ref[...] = jnp.zeros(acc_ref.shape, F32)


def _row_softmax_result(rows, l_ref, acc_ref):
    return acc_ref[rows, :] / jnp.maximum(l_ref[rows, :], 1e-30)


def _row_masked_softmax(s, valid):
    s = jnp.where(valid, s, NEG_INF)
    m = jnp.max(s, axis=-1, keepdims=True)
    m = jnp.where(m == NEG_INF, 0.0, m)
    e = jnp.exp(s - m)
    return e / jnp.maximum(jnp.sum(e, axis=-1, keepdims=True), 1e-30)


def _row_softmax_scratch(rows, d):
    return [pltpu.VMEM((rows, 1), F32), pltpu.VMEM((rows, 1), F32), pltpu.VMEM((rows, d), F32)]


def _unstack_heads(o_heads):
    lane = _lane_iota()
    chunks = []
    for c in range(2):
        chunks.append(jnp.where(lane < HEAD_DIM, pltpu.roll(o_heads[2 * c], HEAD_DIM, 1), o_heads[2 * c + 1]))
    return jnp.concatenate(chunks, axis=1)


def _dsa_rows_core(q, iq, iw, load_kv, load_ik, key_ref, m_ref, l_ref, acc_ref, *, tq, kc, n_top, p0, idx_bits):
    n_chunks = (p0 + tq - 1) // kc + 1
    pos = p0 + _row_iota(tq)
    iqs = _stack_heads(iq).astype(BF16)
    qs = (_stack_heads(q) * SCALE).astype(BF16)
    w_cols = [iw[:, HEAD_DIM + h:HEAD_DIM + h + 1] for h in range(N_HEADS)]

    def chunk_pos(c):
        return c * kc + _lane_iota(kc)

    def score_chunk(c, carry):
        lg = _dot_nt(iqs, load_ik(c))
        sc = w_cols[0] * jnp.maximum(lg[0:tq], 0.0)
        for h in range(1, N_HEADS):
            sc = sc + w_cols[h] * jnp.maximum(lg[h * tq:(h + 1) * tq], 0.0)
        sc = jnp.where(chunk_pos(c) <= pos, sc, NEG_INF)
        bits = pltpu.bitcast(sc, jnp.int32)
        key_ref[c] = bits ^ ((bits >> 31) & 0x7FFFFFFF)
        return carry

    lax.fori_loop(0, n_chunks, score_chunk, 0)

    def count(pred):
        def body(c, acc):
            hit = jnp.where(pred(key_ref[c], chunk_pos(c)), 1, 0)
            parts = [hit[:, t * LANES:(t + 1) * LANES] for t in range(kc // LANES)]
            while len(parts) > 1:
                parts = [a + b for a, b in zip(parts[0::2], parts[1::2])]
            return acc + parts[0]
        acc = lax.fori_loop(0, n_chunks, body, jnp.zeros((tq, LANES), jnp.int32))
        return jnp.sum(acc, axis=1, keepdims=True)

    def thr_bit(b, thr):
        cand = thr + jnp.left_shift(jnp.int32(1), 31 - b)
        cnt = count(lambda key, kpos: key >= cand)
        return jnp.where(cnt >= n_top, cand, thr)

    thr = lax.fori_loop(0, 32, thr_bit, jnp.full((tq, 1), INT_MIN, jnp.int32))

    def last_tied():
        need = n_top - count(lambda key, kpos: key > thr)

        def idx_bit(b, last):
            cand = last + jnp.left_shift(jnp.int32(1), idx_bits - 1 - b)
            cnt = count(lambda key, kpos: (key == thr) & (kpos < cand))
            return jnp.where(cnt < need, cand, last)

        return lax.fori_loop(0, idx_bits, idx_bit, jnp.zeros((tq, 1), jnp.int32))

    over = (count(lambda key, kpos: key >= thr) > n_top) & (thr > KEY_NEG_INF)
    last = lax.cond(jnp.max(jnp.where(over, 1, 0)) > 0, last_tied,
                    lambda: jnp.full((tq, 1), 2 ** idx_bits, jnp.int32))

    _row_softmax_init(m_ref, l_ref, acc_ref)

    def attend(c, carry):
        kvc = load_kv(c)
        s = _dot_nt(qs, kvc)
        key = key_ref[c]
        sel = (key > thr) | ((key == thr) & (chunk_pos(c) <= last))
        sel = sel & (key > KEY_NEG_INF) & (key < KEY_POS_INF)
        for h in range(N_HEADS):
            _row_softmax_step(pl.ds(h * tq, tq), s[h * tq:(h + 1) * tq], sel, kvc, m_ref, l_ref, acc_ref)
        return carry

    lax.fori_loop(0, n_chunks, attend, 0)
    return _unstack_heads([_row_softmax_result(pl.ds(h * tq, tq), l_ref, acc_ref) for h in range(N_HEADS)])


def _dsa_decode_kernel(pt_ref, q_ref, iq_ref, iw_ref, *rest, tq, kc, n_top, idx_bits, n_pages, page, pps):
    page_refs, (new_ref, o_ref, kv_ref, ik_ref, key_ref, m_ref, l_ref, acc_ref) = rest[:pps], rest[pps:]
    s = pl.program_id(1)
    n_steps = n_pages // pps

    def put(at, rows):
        kv_ref[at, :] = rows[:, 0:LANES].astype(BF16)
        ik = rows[:, LANES:]
        ik_ref[at, :] = jnp.concatenate([ik, jnp.zeros((rows.shape[0], 2 * LANES - rows.shape[1]), F32)],
                                        axis=1).astype(BF16)

    @pl.when(s < n_steps)
    def _():
        for r in range(pps):
            put(pl.ds(pl.multiple_of((s * pps + r) * page, page), page), page_refs[r][0, 0])

    @pl.when(s == n_steps)
    def _():
        put(pl.ds(n_pages * page, kc), new_ref[0])
        o_ref[0] = _dsa_rows_core(q_ref[0], iq_ref[0], iw_ref[0], _chunk_loader(kv_ref, kc), _chunk_loader(ik_ref, kc),
                                  key_ref, m_ref, l_ref, acc_ref, tq=tq, kc=kc, n_top=n_top, p0=n_pages * page,
                                  idx_bits=idx_bits)


def _dsa_scratch(n_kc, tq, kc):
    return ([pltpu.VMEM((n_kc, LANES, kc), BF16), pltpu.VMEM((n_kc, kc, tq), jnp.int32)]
            + _attend_scratch(LANES, N_HEADS * tq, kc))


def _page_specs(cache, layer, n_pages, pps):
    block = (1, 1) + cache.shape[2:]
    zeros = (0,) * (len(cache.shape) - 2)

    def spec(r):
        return pl.BlockSpec(block, lambda bi, s, pt: (layer, pt[bi, jnp.minimum(s * pps + r, n_pages - 1)]) + zeros)

    return [spec(r) for r in range(pps)]


def _dsa_decode_call(zq, cache, layer, page_table, new_rows, *, kc=DEC_KEY_CHUNK, pps=PAGES_PER_STEP):
    b, tq, _ = zq.shape
    n_pages = page_table.shape[1]
    page = cache.shape[2]
    past = n_pages * page
    assert past % kc == 0 and n_pages % pps == 0
    l_rows = past + kc
    n_kc = l_rows // kc
    kern = functools.partial(_dsa_decode_kernel, tq=tq, kc=kc, n_top=min(DSA_TOPK, (past + 1) // 4),
                             idx_bits=max(1, math.ceil(math.log2(l_rows))), n_pages=n_pages, page=page, pps=pps)
    grid_spec = pltpu.PrefetchScalarGridSpec(
        num_scalar_prefetch=1,
        grid=(b, n_pages // pps + 1),
        in_specs=[pl.BlockSpec((1, tq, 2 * LANES), lambda bi, s, pt: (bi, 0, _NEW["q_a"] // 256)),
                  pl.BlockSpec((1, tq, 2 * LANES), lambda bi, s, pt: (bi, 0, _NEW["iq"] // 256)),
                  pl.BlockSpec((1, tq, LANES), lambda bi, s, pt: (bi, 0, BLK_IK))]
        + _page_specs(cache, layer, n_pages, pps)
        + [pl.BlockSpec((1,) + new_rows.shape[1:], lambda bi, s, pt: (bi, 0, 0))],
        out_specs=pl.BlockSpec((1, tq, 2 * LANES), lambda bi, s, pt: (bi, 0, 0)),
        scratch_shapes=[pltpu.VMEM((l_rows, LANES), BF16), pltpu.VMEM((l_rows, LANES), BF16),
                        pltpu.VMEM((n_kc, tq, kc), jnp.int32)] + _row_softmax_scratch(N_HEADS * tq, LANES))
    return pl.pallas_call(
        kern, grid_spec=grid_spec,
        out_shape=jax.ShapeDtypeStruct((b, tq, 2 * LANES), F32),
        compiler_params=_params(("parallel", "arbitrary")),
        name="dsa_decode",
    )(page_table, zq, zq, zq, *([cache] * pps), new_rows)


def _dsa_call(zq, kv, ik, *, kv_blk, ik_blk, tq, n_keys, kc=KEY_CHUNK):
    b, t_q, _ = zq.shape
    l_rows = kv.shape[1]
    n_kc = l_rows // kc
    kern = functools.partial(_dsa_kernel, tq=tq, kc=kc, n_top=min(DSA_TOPK, n_keys // 4),
                             idx_bits=max(1, math.ceil(math.log2(l_rows))))
    return pl.pallas_call(
        kern,
        grid=(b, t_q // tq),
        in_specs=[pl.BlockSpec((1, tq, 2 * LANES), lambda bi, i: (bi, i, _NEW["q_a"] // 256)),
                  pl.BlockSpec((1, tq, 2 * LANES), lambda bi, i: (bi, i, _NEW["iq"] // 256)),
                  pl.BlockSpec((1, tq, LANES), lambda bi, i: (bi, i, BLK_IK)),
                  pl.BlockSpec((1, l_rows, LANES), lambda bi, i: (bi, 0, kv_blk)),
                  pl.BlockSpec((1, l_rows, LANES), lambda bi, i: (bi, 0, ik_blk))],
        out_specs=pl.BlockSpec((1, tq, 2 * LANES), lambda bi, i: (bi, i, 0)),
        out_shape=jax.ShapeDtypeStruct((b, t_q, 2 * LANES), F32),
        scratch_shapes=_dsa_scratch(n_kc, tq, kc),
        compiler_params=_params(("parallel", "arbitrary")),
        name="dsa_attention",
    )(zq, zq, zq, kv, ik)


def _compress(load_rows, wlo_ref, whi_ref, pe_ref, w2_ref, n_blocks):
    a = jnp.zeros((n_blocks, LANES), F32)
    b = jnp.zeros((n_blocks, LANES), F32)
    for p in range(NSA_CMP_STRIDE):
        xp = load_rows(p)
        a = a + _dot((xp + pe_ref[p:p + 1, :]).astype(BF16), wlo_ref[p])
        b = b + _dot((xp + pe_ref[NSA_CMP_STRIDE + p:NSA_CMP_STRIDE + p + 1, :]).astype(BF16), whi_ref[p])
    h = jax.nn.gelu(a + pltpu.roll(b, n_blocks - 1, 0))
    return _dot(h.astype(BF16), w2_ref[...])


def _cmp_kernel(rows_ref, wlo_ref, whi_ref, pe_ref, w2_ref, o_ref, *, n_blocks):
    o_ref[0] = _compress(lambda p: rows_ref[0, pl.ds(p, n_blocks, stride=NSA_CMP_STRIDE), :],
                         wlo_ref, whi_ref, pe_ref, w2_ref, n_blocks)


def _block_diag2(m0, m1):
    z = jnp.zeros_like(m0)
    return jnp.concatenate([jnp.concatenate([m0, z], axis=-1), jnp.concatenate([z, m1], axis=-1)], axis=-2)


def _compress_weights(w_cmp1, w_cmp2, pe_cmp):
    w1 = w_cmp1.reshape(2, NSA_CMP_LEN, HEAD_DIM, HEAD_DIM)
    wlo = _block_diag2(w1[0, :NSA_CMP_STRIDE], w1[1, :NSA_CMP_STRIDE]).astype(BF16)
    whi = _block_diag2(w1[0, NSA_CMP_STRIDE:], w1[1, NSA_CMP_STRIDE:]).astype(BF16)
    pe = jnp.concatenate([pe_cmp[0], pe_cmp[1]], axis=-1)
    w2 = _block_diag2(w_cmp2[0], w_cmp2[1]).astype(BF16)
    return wlo, whi, pe, w2


def _compress_specs(index_map3, index_map2):
    return [pl.BlockSpec((NSA_CMP_STRIDE, LANES, LANES), index_map3),
            pl.BlockSpec((NSA_CMP_STRIDE, LANES, LANES), index_map3),
            pl.BlockSpec((NSA_CMP_LEN, LANES), index_map2),
            pl.BlockSpec((LANES, LANES), index_map2)]


def _nsa_compress(rows, cmp_w, *, blk):
    b, l_rows, _ = rows.shape
    n_blocks = l_rows // NSA_CMP_STRIDE
    return pl.pallas_call(
        functools.partial(_cmp_kernel, n_blocks=n_blocks),
        grid=(b,),
        in_specs=[pl.BlockSpec((1, l_rows, LANES), lambda bi: (bi, 0, blk))]
        + _compress_specs(lambda bi: (0, 0, 0), lambda bi: (0, 0)),
        out_specs=pl.BlockSpec((1, n_blocks, LANES), lambda bi: (bi, 0, 0)),
        out_shape=jax.ShapeDtypeStruct((b, n_blocks, LANES), F32),
        compiler_params=_params(("parallel",)),
        name="nsa_compress",
    )(rows, *cmp_w)


def _nsa_core(q, g, cmp_rows, load_sel, load_selt, wrows, wpos, rank_ref, s_ref, p_ref, m_ref, l_ref, acc_ref,
              *, tq, kc, p0, n_cmp, n_blk, n_sel):
    pos = p0 + _lane_iota(tq)
    qs_t = (_stack_heads(q) * SCALE).T.astype(BF16)

    ncp = cmp_rows.shape[0]
    cmp_t = cmp_rows.T.astype(BF16)
    s_c = _dot(cmp_rows.astype(BF16), qs_t)
    n_idx = _row_iota(ncp)
    valid_c = (n_idx * NSA_CMP_STRIDE + (NSA_CMP_LEN - 1) <= pos) & (n_idx < n_cmp)
    p_c = [_tmasked_softmax(s_c[:, h * tq:(h + 1) * tq], valid_c) for h in range(N_HEADS)]
    o_c = _dot(cmp_t, jnp.concatenate(p_c, axis=1).astype(BF16))
    p_sum = p_c[0] + p_c[1] + p_c[2] + p_c[3]

    nbp = rank_ref.shape[0]
    jb = lax.broadcasted_iota(jnp.int32, (nbp, ncp), 0)
    nn = lax.broadcasted_iota(jnp.int32, (nbp, ncp), 1)
    cover_t = ((nn * NSA_CMP_STRIDE < (jb + 1) * NSA_SEL_BLOCK)
               & (nn * NSA_CMP_STRIDE + (NSA_CMP_LEN - 1) >= jb * NSA_SEL_BLOCK) & (nn < n_cmp)).astype(BF16)
    hi, lo = _split_bf16(p_sum)
    imp = _dot(cover_t, hi) + _dot(cover_t, lo)
    j = _row_iota(nbp)
    cur = pos // NSA_SEL_BLOCK
    forced = (j == 0) | (j == cur) | (j == cur - 1)
    imp = jnp.where(forced, jnp.inf, jnp.where(j <= cur, imp, NEG_INF))
    n_vis = jnp.minimum((p0 + tq - 1) // NSA_SEL_BLOCK + 1, n_blk)
    sel = _select_top_t(imp, rank_ref, n_vis, n_sel).astype(BF16)

    n_chunks = (p0 + tq - 1) // kc + 1
    ek = lax.broadcasted_iota(jnp.int32, (kc, nbp), 0)
    eb = lax.broadcasted_iota(jnp.int32, (kc, nbp), 1)

    def selected(c):
        expand = (eb == (c * kc + ek) // NSA_SEL_BLOCK).astype(BF16)
        picked = _dot(expand, sel)

        def strip(r0, rows):
            return (picked[r0:r0 + rows] > 0.5) & (c * kc + r0 + _row_iota(rows) <= pos)
        return strip

    _attend_loop(n_chunks, lambda c, r0, n: _dot(load_sel(c, r0, n), qs_t), selected, load_selt,
                 s_ref, p_ref, m_ref, l_ref, acc_ref, reps=N_HEADS)
    o_s = _tsoftmax_result(l_ref, acc_ref)

    s_w = _dot(wrows.astype(BF16), qs_t)
    valid_w = (wpos <= pos) & (pos - wpos < NSA_WINDOW)
    p_w = [_tmasked_softmax(s_w[:, h * tq:(h + 1) * tq], valid_w) for h in range(N_HEADS)]
    o_w = _dot(wrows.T.astype(BF16), jnp.concatenate(p_w, axis=1).astype(BF16))

    g_t = _rows_t(g)
    mixed = []
    for h in range(N_HEADS):
        cols = slice(h * tq, (h + 1) * tq)
        mixed.append(g_t[3 * h:3 * h + 1] * o_c[:, cols] + g_t[3 * h + 1:3 * h + 2] * o_s[:, cols]
                     + g_t[3 * h + 2:3 * h + 3] * o_w[:, cols])
    return _unstack_heads_t(jnp.concatenate(mixed, axis=1), tq)


def _nsa_kernel(q_ref, g_ref, cmp_ref, sel_ref, win_ref, o_ref, selt_ref, rank_ref, s_ref, p_ref, m_ref, l_ref, acc_ref,
                *, tq, kc, n_cmp, n_blk, n_sel, win_rows):
    i = pl.program_id(1)

    @pl.when(i == 0)
    def _():
        _fill_transposed(sel_ref, selt_ref, kc)

    p0 = i * tq
    start = pl.multiple_of(jnp.clip(p0 - NSA_WINDOW, 0, win_ref.shape[1] - win_rows), SUBLANES)
    o_ref[0] = _nsa_core(q_ref[0], g_ref[0], cmp_ref[0], _chunk_loader(sel_ref, kc), lambda c: selt_ref[c],
                         win_ref[0, pl.ds(start, win_rows), :], start + _row_iota(win_rows),
                         rank_ref, s_ref, p_ref, m_ref, l_ref, acc_ref,
                         tq=tq, kc=kc, p0=p0, n_cmp=n_cmp, n_blk=n_blk, n_sel=n_sel)


def _top_mask_row(v, n_sel):
    n = v.shape[1]
    i = lax.broadcasted_iota(jnp.int32, (n, n), 0)
    j = lax.broadcasted_iota(jnp.int32, (n, n), 1)
    v_col = jnp.sum(jnp.where(i == j, v, 0.0), axis=1, keepdims=True)
    beats = (v_col > v) | ((v_col == v) & (i < j))
    rank = jnp.sum(jnp.where(beats, 1, 0), axis=0, keepdims=True)
    return jnp.where((rank < n_sel) & (v > NEG_INF), 1.0, 0.0)


def _nsa_rows_core(q, g, cmpv, load_sel, wrows, wpos, m_ref, l_ref, acc_ref, *, tq, kc, p0, n_cmp, n_blk, n_sel):
    pos = p0 + _row_iota(tq)
    qs = (_stack_heads(q) * SCALE).astype(BF16)

    ncp = cmpv.shape[0]
    s_c = _dot_nt(qs, cmpv)
    n_idx = _lane_iota(ncp)
    valid_c = (n_idx * NSA_CMP_STRIDE + (NSA_CMP_LEN - 1) <= pos) & (n_idx < n_cmp)
    o_c = []
    p_sum = jnp.zeros((tq, ncp), F32)
    for h in range(N_HEADS):
        p = _row_masked_softmax(s_c[h * tq:(h + 1) * tq], valid_c)
        p_sum = p_sum + p
        o_c.append(_dot(p.astype(BF16), cmpv))

    nbp = -(-n_blk // LANES) * LANES
    nn = lax.broadcasted_iota(jnp.int32, (ncp, nbp), 0)
    jb = lax.broadcasted_iota(jnp.int32, (ncp, nbp), 1)
    cover = ((nn * NSA_CMP_STRIDE < (jb + 1) * NSA_SEL_BLOCK)
             & (nn * NSA_CMP_STRIDE + (NSA_CMP_LEN - 1) >= jb * NSA_SEL_BLOCK) & (nn < n_cmp)).astype(BF16)
    hi, lo = _split_bf16(p_sum)
    imp = _dot(hi, cover) + _dot(lo, cover)
    j = _lane_iota(nbp)
    cur = pos // NSA_SEL_BLOCK
    forced = (j == 0) | (j == cur) | (j == cur - 1)
    imp = jnp.where(forced, jnp.inf, jnp.where(j <= cur, imp, NEG_INF))
    sel = jnp.broadcast_to(_top_mask_row(imp[0:1], n_sel), (tq, nbp)).astype(BF16)

    _row_softmax_init(m_ref, l_ref, acc_ref)
    n_chunks = (p0 + tq - 1) // kc + 1
    eb = lax.broadcasted_iota(jnp.int32, (nbp, kc), 0)
    ek = lax.broadcasted_iota(jnp.int32, (nbp, kc), 1)

    def attend(c, carry):
        rows = load_sel(c)
        s = _dot_nt(qs, rows)
        expand = (eb == (c * kc + ek) // NSA_SEL_BLOCK).astype(BF16)
        picked = _dot(sel, expand)
        valid = (picked > 0.5) & (c * kc + _lane_iota(kc) <= pos)
        for h in range(N_HEADS):
            _row_softmax_step(pl.ds(h * tq, tq), s[h * tq:(h + 1) * tq], valid, rows, m_ref, l_ref, acc_ref)
        return carry

    lax.fori_loop(0, n_chunks, attend, 0)

    s_w = _dot_nt(qs, wrows)
    valid_w = (wpos <= pos) & (pos - wpos < NSA_WINDOW)
    heads = []
    for h in range(N_HEADS):
        p_w = _row_masked_softmax(s_w[h * tq:(h + 1) * tq], valid_w)
        o_w = _dot(p_w.astype(BF16), wrows)
        o_s = _row_softmax_result(pl.ds(h * tq, tq), l_ref, acc_ref)
        heads.append(g[:, 3 * h:3 * h + 1] * o_c[h] + g[:, 3 * h + 1:3 * h + 2] * o_s
                     + g[:, 3 * h + 2:3 * h + 3] * o_w)
    return _unstack_heads(heads)


def _nsa_decode_kernel(pt_ref, q_ref, g_ref, *rest, tq, kc, n_cmp, n_blk, n_sel, n_pages, page, pps):
    page_refs = rest[:pps]
    (new_ref, win_ref, wlo_ref, whi_ref, pe_ref, w2_ref, o_ref, cbuf_ref, sbuf_ref, m_ref, l_ref, acc_ref) = rest[pps:]
    s = pl.program_id(1)
    n_steps = n_pages // pps

    @pl.when(s < n_steps)
    def _():
        for r in range(pps):
            at = pl.ds(pl.multiple_of((s * pps + r) * page, page), page)
            cbuf_ref[at, :] = page_refs[r][0, 0, :, 0:LANES]
            sbuf_ref[at, :] = page_refs[r][0, 0, :, LANES:2 * LANES].astype(BF16)

    @pl.when(s == n_steps)
    def _():
        past = n_pages * page
        sbuf_ref[pl.ds(past, kc), :] = new_ref[0, :, LANES:2 * LANES].astype(BF16)
        n_blocks = past // NSA_CMP_STRIDE
        cmpv = _compress(lambda r: cbuf_ref[pl.ds(r, n_blocks, stride=NSA_CMP_STRIDE), :],
                         wlo_ref, whi_ref, pe_ref, w2_ref, n_blocks).astype(BF16)
        wrows = win_ref[0].astype(BF16)
        o_ref[0] = _nsa_rows_core(q_ref[0], g_ref[0], cmpv, _chunk_loader(sbuf_ref, kc), wrows,
                                  past - NSA_WINDOW + _lane_iota(wrows.shape[0]), m_ref, l_ref, acc_ref,
                                  tq=tq, kc=kc, p0=past, n_cmp=n_cmp, n_blk=n_blk, n_sel=n_sel)


def _nsa_sizes(n_keys):
    n_cmp = (n_keys - NSA_CMP_LEN) // NSA_CMP_STRIDE + 1
    n_blk = -(-n_keys // NSA_SEL_BLOCK)
    return n_cmp, n_blk, min(NSA_SEL_TOPN, n_blk), -(-n_blk // LANES) * LANES


def _nsa_scratch(n_kc, nbp, tq, kc):
    return ([pltpu.VMEM((n_kc, LANES, kc), BF16), pltpu.VMEM((nbp, tq), F32)]
            + _attend_scratch(LANES, N_HEADS * tq, kc))


def _nsa_decode_call(zq, cache, layer, page_table, new_rows, win_rows, cmp_w, *, kc=DEC_KEY_CHUNK,
                     pps=PAGES_PER_STEP):
    b, tq, _ = zq.shape
    n_pages = page_table.shape[1]
    page = cache.shape[2]
    past = n_pages * page
    assert past % kc == 0 and n_pages % pps == 0
    n_cmp, n_blk, n_sel, _ = _nsa_sizes(past + 1)
    kern = functools.partial(_nsa_decode_kernel, tq=tq, kc=kc, n_cmp=n_cmp, n_blk=n_blk, n_sel=n_sel,
                             n_pages=n_pages, page=page, pps=pps)
    grid_spec = pltpu.PrefetchScalarGridSpec(
        num_scalar_prefetch=1,
        grid=(b, n_pages // pps + 1),
        in_specs=[pl.BlockSpec((1, tq, 2 * LANES), lambda bi, s, pt: (bi, 0, _NEW["q_b"] // 256)),
                  pl.BlockSpec((1, tq, LANES), lambda bi, s, pt: (bi, 0, BLK_GB))]
        + _page_specs(cache, layer, n_pages, pps)
        + [pl.BlockSpec((1,) + new_rows.shape[1:], lambda bi, s, pt: (bi, 0, 0)),
           pl.BlockSpec((1,) + win_rows.shape[1:], lambda bi, s, pt: (bi, 0, 0))]
        + _compress_specs(lambda bi, s, pt: (0, 0, 0), lambda bi, s, pt: (0, 0)),
        out_specs=pl.BlockSpec((1, tq, 2 * LANES), lambda bi, s, pt: (bi, 0, 0)),
        scratch_shapes=[pltpu.VMEM((past, LANES), F32), pltpu.VMEM((past + kc, LANES), BF16)]
        + _row_softmax_scratch(N_HEADS * tq, LANES))
    return pl.pallas_call(
        kern, grid_spec=grid_spec,
        out_shape=jax.ShapeDtypeStruct((b, tq, 2 * LANES), F32),
        compiler_params=_params(("parallel", "arbitrary")),
        name="nsa_decode",
    )(page_table, zq, zq, *([cache] * pps), new_rows, win_rows, *cmp_w)


def _nsa_call(zq, cmp, sel, win, *, sel_blk, win_blk, tq, n_keys, kc=KEY_CHUNK):
    b, t_q, _ = zq.shape
    l_rows = sel.shape[1]
    n_cmp, n_blk, n_sel, nbp = _nsa_sizes(n_keys)
    kern = functools.partial(_nsa_kernel, tq=tq, kc=kc, n_cmp=n_cmp, n_blk=n_blk, n_sel=n_sel,
                             win_rows=min(NSA_WINDOW + tq, l_rows))
    return pl.pallas_call(
        kern,
        grid=(b, t_q // tq),
        in_specs=[pl.BlockSpec((1, tq, 2 * LANES), lambda bi, i: (bi, i, _NEW["q_b"] // 256)),
                  pl.BlockSpec((1, tq, LANES), lambda bi, i: (bi, i, BLK_GB)),
                  pl.BlockSpec((1, cmp.shape[1], LANES), lambda bi, i: (bi, 0, 0)),
                  pl.BlockSpec((1, l_rows, LANES), lambda bi, i: (bi, 0, sel_blk)),
                  pl.BlockSpec((1, l_rows, LANES), lambda bi, i: (bi, 0, win_blk))],
        out_specs=pl.BlockSpec((1, tq, 2 * LANES), lambda bi, i: (bi, i, 0)),
        out_shape=jax.ShapeDtypeStruct((b, t_q, 2 * LANES), F32),
        scratch_shapes=_nsa_scratch(l_rows // kc, nbp, tq, kc),
        compiler_params=_params(("parallel", "arbitrary")),
        name="nsa_attention",
    )(zq, zq, cmp, sel, win)


def _moba_kernel(q_ref, k_ref, v_ref, o_ref, kmean_ref, vt_ref, rank_ref, sel_ref, s_ref, p_ref, m_ref, l_ref, acc_ref,
                 *, tq, n_blocks, n_sel):
    c0 = pl.program_id(2)

    @pl.when(c0 == 0)
    def _():
        kmean_ref[...] = jnp.zeros(kmean_ref.shape, F32)

        def mean_block(blk, carry):
            rows = k_ref[0, pl.ds(pl.multiple_of(blk * MOBA_BLOCK, MOBA_BLOCK), MOBA_BLOCK), :]
            kmean_ref[pl.ds(blk, 1), :] = jnp.sum(rows, axis=0, keepdims=True) * (1.0 / MOBA_BLOCK)
            return carry

        lax.fori_loop(0, n_blocks, mean_block, 0)
        _fill_transposed(v_ref, vt_ref, MOBA_BLOCK)

    nbp = kmean_ref.shape[0]
    qst = _pair_heads(q_ref[0])
    qst_t = qst.T
    q_hi, q_lo = _split_bf16(qst_t)
    k_hi, k_lo = _split_bf16(kmean_ref[...])
    gate = _dot(k_hi, q_hi) + _dot(k_hi, q_lo) + _dot(k_lo, q_hi)
    gate = jnp.where(_row_iota(nbp) < c0, gate, NEG_INF)
    sel_ref[...] = _select_top_t(gate, rank_ref, c0, n_sel)
    qs_t = (qst_t * SCALE).astype(BF16)
    load_k = _chunk_loader(k_ref, MOBA_BLOCK)
    q_in_block = _lane_iota(2 * tq) % tq

    def admissible(blk):
        picked = sel_ref[pl.ds(blk, 1), :] > 0.5

        def strip(r0, rows):
            return picked | ((blk == c0) & (r0 + _row_iota(rows) <= q_in_block))
        return strip

    _attend_loop(c0 + 1, lambda blk, r0, n: _dot(load_k(blk, r0, n), qs_t), admissible, lambda blk: vt_ref[blk],
                 s_ref, p_ref, m_ref, l_ref, acc_ref, reps=1)
    o = _tsoftmax_result(l_ref, acc_ref).T
    o_ref[0] = jnp.where(_lane_iota() < HEAD_DIM, o[0:tq], o[tq:2 * tq])


def _moba_call(zq, kv, *, tq):
    b, t_q, _ = zq.shape
    l_rows = kv.shape[1]
    assert tq == MOBA_BLOCK
    n_blocks = l_rows // MOBA_BLOCK
    nbp = -(-n_blocks // LANES) * LANES
    kern = functools.partial(_moba_kernel, tq=tq, n_blocks=n_blocks, n_sel=min(MOBA_TOPK, n_blocks))
    q_blk0 = _NEW["q_c"] // LANES
    return pl.pallas_call(
        kern,
        grid=(b, 2, t_q // tq),
        in_specs=[pl.BlockSpec((1, tq, LANES), lambda bi, c, i: (bi, i, q_blk0 + c)),
                  pl.BlockSpec((1, l_rows, LANES), lambda bi, c, i: (bi, 0, BLK_KC + c)),
                  pl.BlockSpec((1, l_rows, LANES), lambda bi, c, i: (bi, 0, BLK_VC + c))],
        out_specs=pl.BlockSpec((1, tq, LANES), lambda bi, c, i: (bi, i, c)),
        out_shape=jax.ShapeDtypeStruct((b, t_q, 2 * LANES), F32),
        scratch_shapes=[pltpu.VMEM((nbp, LANES), F32), pltpu.VMEM((n_blocks, LANES, MOBA_BLOCK), BF16),
                        pltpu.VMEM((nbp, 2 * tq), F32), pltpu.VMEM((nbp, 2 * tq), F32)]
        + _attend_scratch(LANES, 2 * tq, MOBA_BLOCK),
        compiler_params=_params(("parallel", "parallel", "arbitrary")),
        name="moba_attention",
    )(zq, kv, kv)


def _moba_pick_kernel(pt_ref, q_ref, *rest, n_blocks, n_sel, n_pages, page, pps):
    page_refs, (o_ref, kmean_ref, rank_ref) = rest[:pps], rest[pps:]
    s = pl.program_id(1)
    per_block = MOBA_BLOCK // page
    n_steps = n_pages // pps
    nbp = kmean_ref.shape[0]

    @pl.when(s == 0)
    def _():
        kmean_ref[...] = jnp.zeros(kmean_ref.shape, F32)

    @pl.when(s < n_steps)
    def _():
        for j in range(pps // per_block):
            tot = None
            for r in range(per_block):
                part = jnp.sum(page_refs[j * per_block + r][0, 0], axis=0, keepdims=True)
                tot = part if tot is None else tot + part
            kmean_ref[pl.ds(s * (pps // per_block) + j, 1), :] = tot * (1.0 / MOBA_BLOCK)

    @pl.when(s == n_steps)
    def _():
        prod = kmean_ref[...] * q_ref[0, 0:1, :]
        seg = (lax.broadcasted_iota(jnp.int32, (N_HEADS * HEAD_DIM, LANES), 0) // HEAD_DIM
               == lax.broadcasted_iota(jnp.int32, (N_HEADS * HEAD_DIM, LANES), 1)).astype(BF16)
        hi, lo = _split_bf16(prod)
        lo2 = (prod - hi.astype(F32) - lo.astype(F32)).astype(BF16)
        gate = _dot(hi, seg) + _dot(lo, seg) + _dot(lo2, seg)
        j = _row_iota(nbp)
        gate = jnp.where(j < n_blocks, gate, NEG_INF)
        rank = _rank_rows(gate, rank_ref, n_blocks)
        rows = []
        for r in range(n_sel):
            hit = (rank == r) & (gate > NEG_INF)
            rows.append(jnp.sum(jnp.where(hit, j, 0), axis=0, keepdims=True))
        for r in range(n_sel):
            hit = (rank == r) & (gate > NEG_INF)
            rows.append(jnp.sum(jnp.where(hit, 1, 0), axis=0, keepdims=True))
        rows.append(jnp.zeros((SUBLANES - 2 * n_sel, LANES), jnp.int32))
        o_ref[0] = jnp.concatenate(rows, axis=0)


def _moba_pick_call(zq, cache, layer, page_table, *, pps=PAGES_PER_STEP):
    b, tq, _ = zq.shape
    n_pages = page_table.shape[1]
    page = cache.shape[2]
    n_blocks = n_pages * page // MOBA_BLOCK
    n_sel = min(MOBA_TOPK, n_blocks + 1)
    assert MOBA_BLOCK % page == 0 and n_pages % pps == 0 and pps % (MOBA_BLOCK // page) == 0
    assert 2 * n_sel <= SUBLANES
    nbp = -(-n_blocks // LANES) * LANES
    block = (1, 1, page, N_HEADS * HEAD_DIM)

    def spec(r):
        return pl.BlockSpec(block, lambda bi, s, pt: (layer, pt[bi, jnp.minimum(s * pps + r, n_pages - 1)], 0, 0))

    grid_spec = pltpu.PrefetchScalarGridSpec(
        num_scalar_prefetch=1,
        grid=(b, n_pages // pps + 1),
        in_specs=[pl.BlockSpec((1, tq, 2 * LANES), lambda bi, s, pt: (bi, 0, _NEW["q_c"] // 256))]
        + [spec(r) for r in range(pps)],
        out_specs=pl.BlockSpec((1, SUBLANES, LANES), lambda bi, s, pt: (bi, 0, 0)),
        scratch_shapes=[pltpu.VMEM((nbp, N_HEADS * HEAD_DIM), F32), pltpu.VMEM((nbp, LANES), F32)])
    kern = functools.partial(_moba_pick_kernel, n_blocks=n_blocks, n_sel=n_sel, n_pages=n_pages, page=page, pps=pps)
    return pl.pallas_call(
        kern, grid_spec=grid_spec,
        out_shape=jax.ShapeDtypeStruct((b, SUBLANES, LANES), jnp.int32),
        compiler_params=_params(("parallel", "arbitrary")),
        name="moba_pick",
    )(page_table, zq, *([cache] * pps)), n_sel


def _moba_gather_kernel(pid_ref, ok_ref, q_ref, *rest, tq, n_sel, per_block, page):
    n_pg = N_HEADS * per_block
    k_refs, v_refs = rest[:n_pg], rest[n_pg:2 * n_pg]
    new_ref, o_ref, m_ref, l_ref, acc_ref = rest[2 * n_pg:]
    bi, r = pl.program_id(0), pl.program_id(1)

    @pl.when(r == 0)
    def _():
        m_ref[...] = jnp.full(m_ref.shape, NEG_INF, F32)
        l_ref[...] = jnp.zeros(l_ref.shape, F32)
        acc_ref[...] = jnp.zeros(acc_ref.shape, F32)

    qs = (jnp.concatenate([_pair_heads(q_ref[0, :, c * LANES:(c + 1) * LANES]) for c in range(2)], axis=0)
          * SCALE).astype(BF16)
    for h in range(N_HEADS):
        rows = pl.ds(h * tq, tq)
        kb = jnp.concatenate([k_refs[h * per_block + j][0, 0] for j in range(per_block)], axis=0).astype(BF16)
        vb = jnp.concatenate([v_refs[h * per_block + j][0, 0] for j in range(per_block)], axis=0).astype(BF16)
        ok = ok_ref[bi, r * N_HEADS + h] > 0
        _row_softmax_step(rows, _dot_nt(qs[h * tq:(h + 1) * tq], kb), ok, vb, m_ref, l_ref, acc_ref)

    @pl.when(r == n_sel - 1)
    def _():
        ki = lax.broadcasted_iota(jnp.int32, (tq, page), 1)
        qi = lax.broadcasted_iota(jnp.int32, (tq, page), 0)
        n_k = new_ref.shape[2] // 2
        outs = []
        for h in range(N_HEADS):
            rows = pl.ds(h * tq, tq)
            c = h // 2
            kb = new_ref[0, :, c * LANES:(c + 1) * LANES].astype(BF16)
            vb = new_ref[0, :, n_k + c * LANES:n_k + (c + 1) * LANES].astype(BF16)
            _row_softmax_step(rows, _dot_nt(qs[h * tq:(h + 1) * tq], kb), ki <= qi, vb, m_ref, l_ref, acc_ref)
            outs.append(acc_ref[rows, :] / jnp.maximum(l_ref[rows, :], 1e-30))
        lane = _lane_iota()
        o_ref[0] = jnp.concatenate([jnp.where(lane < HEAD_DIM, outs[0], outs[1]),
                                    jnp.where(lane < HEAD_DIM, outs[2], outs[3])], axis=1)


def _moba_decode_call(zq, cache, layer, page_table, new_page):
    b, tq, _ = zq.shape
    page = cache.shape[2]
    per_block = MOBA_BLOCK // page
    picks, n_sel = _moba_pick_call(zq, cache, layer, page_table)
    blk = picks[:, :n_sel, :N_HEADS]
    ok = picks[:, n_sel:2 * n_sel, :N_HEADS].reshape(b, n_sel * N_HEADS)
    logical = blk[..., None] * per_block + jnp.arange(per_block, dtype=jnp.int32)
    pid = jnp.take_along_axis(page_table, logical.reshape(b, -1), axis=1)
    block = (1, 1, page, LANES)

    def spec(h, j, field):
        return pl.BlockSpec(block, lambda bi, r, pid_ref, ok_ref:
                            (layer, pid_ref[bi, (r * N_HEADS + h) * per_block + j], 0, 2 * field + h // 2))

    hj = [(h, j) for h in range(N_HEADS) for j in range(per_block)]
    grid_spec = pltpu.PrefetchScalarGridSpec(
        num_scalar_prefetch=2,
        grid=(b, n_sel),
        in_specs=[pl.BlockSpec((1, tq, 2 * LANES), lambda bi, r, pid_ref, ok_ref: (bi, 0, _NEW["q_c"] // 256))]
        + [spec(h, j, 0) for h, j in hj] + [spec(h, j, 1) for h, j in hj]
        + [pl.BlockSpec((1,) + new_page.shape[1:], lambda bi, r, pid_ref, ok_ref: (bi, 0, 0))],
        out_specs=pl.BlockSpec((1, tq, 2 * LANES), lambda bi, r, pid_ref, ok_ref: (bi, 0, 0)),
        scratch_shapes=_row_softmax_scratch(N_HEADS * tq, LANES))
    kern = functools.partial(_moba_gather_kernel, tq=tq, n_sel=n_sel, per_block=per_block, page=page)
    return pl.pallas_call(
        kern, grid_spec=grid_spec,
        out_shape=jax.ShapeDtypeStruct((b, tq, 2 * LANES), F32),
        compiler_params=_params(("parallel", "arbitrary")),
        name="moba_decode",
    )(pid, ok, zq, *([cache] * (2 * len(hj))), new_page)


def _memory_kv(mem, g_ln, w_kv, g_k):
    b, m_rows, d = mem.shape
    n = w_kv.shape[1]
    half = n // 2
    mask = np.zeros((n,), np.float32)
    mask[:half] = 1
    zero = jnp.zeros((n,), F32)
    gain = jnp.concatenate([jnp.tile(g_k.astype(F32), half // HEAD_DIM), jnp.ones((half,), F32)])
    cfg = jnp.stack([jnp.asarray(mask), gain, zero, zero, zero, zero, zero, zero])
    masks = dict(norm=mask, rope=np.zeros_like(mask), sig=np.zeros_like(mask))
    rope = jnp.zeros((m_rows, 3 * LANES), F32)
    out = _project(mem.reshape(b * m_rows, d), g_ln.reshape(1, d), w_kv.astype(BF16), cfg, rope, masks,
                   tm=m_rows, tn=n)
    return out.reshape(b, m_rows, n)


def _mem_kernel(q_ref, kv_ref, o_ref, *, tq):
    lane = _lane_iota()
    n_kv = kv_ref.shape[2] // 2
    chunks = []
    for c in range(2):
        qst = _pair_heads(q_ref[0, :, c * LANES:(c + 1) * LANES])
        kb = kv_ref[0, :, c * LANES:(c + 1) * LANES].astype(BF16)
        vb = kv_ref[0, :, n_kv + c * LANES:n_kv + (c + 1) * LANES].astype(BF16)
        s = _dot_nt((qst * SCALE).astype(BF16), kb)
        e = jnp.exp(s - jnp.max(s, axis=-1, keepdims=True))
        p = e / jnp.sum(e, axis=-1, keepdims=True)
        o = _dot(p.astype(BF16), vb)
        chunks.append(jnp.where(lane < HEAD_DIM, o[0:tq], o[tq:2 * tq]))
    o_ref[0] = jnp.concatenate(chunks, axis=1)


def _mem_call(zq, mkv, *, tq):
    b, t_q, _ = zq.shape
    tq = min(tq, t_q)
    return pl.pallas_call(
        functools.partial(_mem_kernel, tq=tq),
        grid=(b, t_q // tq),
        in_specs=[pl.BlockSpec((1, tq, 2 * LANES), lambda bi, i: (bi, i, _NEW["q_m"] // 256)),
                  pl.BlockSpec((1,) + mkv.shape[1:], lambda bi, i: (bi, 0, 0))],
        out_specs=pl.BlockSpec((1, tq, 2 * LANES), lambda bi, i: (bi, i, 0)),
        out_shape=jax.ShapeDtypeStruct((b, t_q, 2 * LANES), F32),
        compiler_params=_params(("parallel", "parallel")),
        name="mem_attention",
    )(zq, mkv)


def _combine_kernel(x_ref, oa_ref, ob_ref, oc_ref, om_ref, gate_ref, wb_ref, wo_ref, y_ref):
    d = x_ref.shape[-1]
    h = None
    for bi, o_ref in enumerate((oa_ref, ob_ref, oc_ref, om_ref)):
        t = gate_ref[:, bi * d:(bi + 1) * d] * _dot(o_ref[...].astype(BF16), wb_ref[bi])
        h = t if h is None else h + t
    y_ref[...] = x_ref[...] + _dot(h.astype(BF16), wo_ref[...])


def _combine(x2d, outs, gate, w_branch, w_out, *, tm):
    m, d = x2d.shape
    tm = min(tm, m)
    bw = outs[0].shape[-1]
    o_spec = pl.BlockSpec((tm, bw), lambda i: (i, 0))
    return pl.pallas_call(
        _combine_kernel,
        grid=(m // tm,),
        in_specs=[pl.BlockSpec((tm, d), lambda i: (i, 0)), o_spec, o_spec, o_spec, o_spec,
                  pl.BlockSpec((tm, N_BRANCH * d), lambda i: (i, 0)),
                  pl.BlockSpec((N_BRANCH, bw, d), lambda i: (0, 0, 0)),
                  pl.BlockSpec((d, d), lambda i: (0, 0))],
        out_specs=pl.BlockSpec((tm, d), lambda i: (i, 0)),
        out_shape=jax.ShapeDtypeStruct((m, d), F32),
        compiler_params=_params(("parallel",)),
        name="branch_mix",
    )(x2d, *outs, gate, w_branch, w_out)


FF_CHUNK = 256
HALO = 16


def _rms(x, g):
    return x * lax.rsqrt(jnp.mean(x * x, axis=-1, keepdims=True) + RMS_EPS) * g


def _conv3(cw, u2, u1, u0):
    return cw[3:4] + cw[0:1] * u2 + cw[1:2] * u1 + cw[2:3] * u0


def _ffn_kernel(x_ref, xh_ref, ha_ref, hb_ref, g_ref, wa_ref, wb_ref, cwa_ref, cwb_ref, wdn_ref,
                y_ref, sta_ref, stb_ref, xn_ref, xhn_ref, acc_ref, *, tm):
    i = pl.program_id(1)
    j = pl.program_id(2)

    @pl.when(j == 0)
    def _():
        xn_ref[...] = _rms(x_ref[0], g_ref[...]).astype(BF16)
        xhn_ref[...] = _rms(xh_ref[0], g_ref[...]).astype(BF16)
        acc_ref[...] = jnp.zeros(acc_ref.shape, F32)

    def half(w_ref, hist_ref, cw_ref, st_ref):
        u = _dot(xn_ref[...], w_ref[...])
        u_prev = _dot(xhn_ref[...], w_ref[...])[HALO - SUBLANES:HALO]
        prev = jnp.where(i == 0, hist_ref[0], u_prev)
        ext = jnp.concatenate([prev, u], axis=0)
        st_ref[0, 0] = ext[tm:tm + SUBLANES]
        return _conv3(cw_ref[...], pltpu.roll(ext, 2, 0)[SUBLANES:], pltpu.roll(ext, 1, 0)[SUBLANES:], u)

    a = half(wa_ref, ha_ref, cwa_ref, sta_ref)
    b = half(wb_ref, hb_ref, cwb_ref, stb_ref)
    acc_ref[...] += _dot((a * jax.nn.sigmoid(a) * b).astype(BF16), wdn_ref[...])

    @pl.when(j == pl.num_programs(2) - 1)
    def _():
        y_ref[0] = x_ref[0] + acc_ref[...]


def _conv_table(conv_w, conv_b):
    return jnp.concatenate([conv_w, conv_b[None, :], jnp.zeros((SUBLANES - CONV_WIDTH - 1, conv_b.shape[0]), F32)])


def _conv_ffn(x, hist, g, w_up, conv_w, conv_b, w_down, *, tm):
    b, t, d = x.shape
    d_ff = w_down.shape[0]
    n_j = d_ff // FF_CHUNK
    tm = min(tm, t)
    cw = _conv_table(conv_w, conv_b)
    hist8 = jnp.concatenate([jnp.zeros((b, SUBLANES - 2, 2 * d_ff), F32), hist], axis=1)
    a_col = lambda bi, i, j: (0, j)
    b_col = lambda bi, i, j: (0, n_j + j)
    st_spec = pl.BlockSpec((1, 1, SUBLANES, FF_CHUNK), lambda bi, i, j: (bi, i, 0, j))
    st_shape = jax.ShapeDtypeStruct((b, t // tm, SUBLANES, d_ff), F32)
    y, st_a, st_b = pl.pallas_call(
        functools.partial(_ffn_kernel, tm=tm),
        grid=(b, t // tm, n_j),
        in_specs=[pl.BlockSpec((1, tm, d), lambda bi, i, j: (bi, i, 0)),
                  pl.BlockSpec((1, HALO, d), lambda bi, i, j: (bi, jnp.maximum(i * (tm // HALO) - 1, 0), 0)),
                  pl.BlockSpec((1, SUBLANES, FF_CHUNK), lambda bi, i, j: (bi, 0, j)),
                  pl.BlockSpec((1, SUBLANES, FF_CHUNK), lambda bi, i, j: (bi, 0, n_j + j)),
                  pl.BlockSpec((1, d), lambda bi, i, j: (0, 0)),
                  pl.BlockSpec((d, FF_CHUNK), a_col), pl.BlockSpec((d, FF_CHUNK), b_col),
                  pl.BlockSpec((SUBLANES, FF_CHUNK), a_col), pl.BlockSpec((SUBLANES, FF_CHUNK), b_col),
                  pl.BlockSpec((FF_CHUNK, d), lambda bi, i, j: (j, 0))],
        out_specs=[pl.BlockSpec((1, tm, d), lambda bi, i, j: (bi, i, 0)), st_spec, st_spec],
        out_shape=[jax.ShapeDtypeStruct((b, t, d), F32), st_shape, st_shape],
        scratch_shapes=[pltpu.VMEM((tm, d), BF16), pltpu.VMEM((HALO, d), BF16), pltpu.VMEM((tm, d), F32)],
        compiler_params=_params(("parallel", "arbitrary", "arbitrary")),
        name="conv_ffn",
    )(x, x, hist8, hist8, g.reshape(1, d), w_up, w_up, cw, cw, w_down)
    return y, jnp.concatenate([st_a[:, -1, SUBLANES - 2:], st_b[:, -1, SUBLANES - 2:]], axis=-1)


def _ffn_row_kernel(x_ref, h0a_ref, h0b_ref, h1a_ref, h1b_ref, g_ref, wa_ref, wb_ref, cwa_ref, cwb_ref, wdn_ref,
                    y_ref, ua_ref, ub_ref, xn_ref, acc_ref):
    j = pl.program_id(0)

    @pl.when(j == 0)
    def _():
        xn_ref[...] = _rms(x_ref[...], g_ref[...]).astype(BF16)
        acc_ref[...] = jnp.zeros(acc_ref.shape, F32)

    ua = _dot(xn_ref[...], wa_ref[...])
    ub = _dot(xn_ref[...], wb_ref[...])
    ua_ref[...] = ua
    ub_ref[...] = ub
    a = _conv3(cwa_ref[...], h0a_ref[...], h1a_ref[...], ua)
    b = _conv3(cwb_ref[...], h0b_ref[...], h1b_ref[...], ub)
    acc_ref[...] += _dot((a * jax.nn.sigmoid(a) * b).astype(BF16), wdn_ref[...])

    @pl.when(j == pl.num_programs(0) - 1)
    def _():
        y_ref[...] = x_ref[...] + acc_ref[...]


def _conv_ffn_rows(x2d, hist, g, w_up, conv_w, conv_b, w_down):
    b, d = x2d.shape
    d_ff = w_down.shape[0]
    n_j = d_ff // FF_CHUNK
    cw = _conv_table(conv_w, conv_b)
    h0, h1 = hist[:, 0], hist[:, 1]
    a_col = lambda j: (0, j)
    b_col = lambda j: (0, n_j + j)
    row_a, row_b = pl.BlockSpec((b, FF_CHUNK), a_col), pl.BlockSpec((b, FF_CHUNK), b_col)
    y, ua, ub = pl.pallas_call(
        _ffn_row_kernel,
        grid=(n_j,),
        in_specs=[pl.BlockSpec((b, d), lambda j: (0, 0)), row_a, row_b, row_a, row_b,
                  pl.BlockSpec((1, d), lambda j: (0, 0)),
                  pl.BlockSpec((d, FF_CHUNK), a_col), pl.BlockSpec((d, FF_CHUNK), b_col),
                  pl.BlockSpec((SUBLANES, FF_CHUNK), a_col), pl.BlockSpec((SUBLANES, FF_CHUNK), b_col),
                  pl.BlockSpec((FF_CHUNK, d), lambda j: (j, 0))],
        out_specs=[pl.BlockSpec((b, d), lambda j: (0, 0)), row_a, row_a],
        out_shape=[jax.ShapeDtypeStruct((b, d), F32), jax.ShapeDtypeStruct((b, d_ff), F32),
                   jax.ShapeDtypeStruct((b, d_ff), F32)],
        scratch_shapes=[pltpu.VMEM((b, d), BF16), pltpu.VMEM((b, d), F32)],
        compiler_params=_params(("arbitrary",)),
        name="conv_ffn_rows",
    )(x2d, h0, h0, h1, h1, g.reshape(1, d), w_up, w_up, cw, cw, w_down)
    return y, jnp.stack([h1, jnp.concatenate([ua, ub], axis=-1)], axis=1)


def _cols(zh, name, width):
    return zh[..., _NEW[name]:_NEW[name] + width]


def _new_rows(zh):
    b, t, _ = zh.shape
    dsa = jnp.concatenate([_cols(zh, "k_a", 2 * HEAD_DIM), _cols(zh, "ik", HEAD_DIM)], axis=-1)
    nsa = _cols(zh, "kc", 4 * HEAD_DIM)
    moba = _cols(zh, "k_c", 2 * N_HEADS * HEAD_DIM)
    win = _cols(zh, "kw", 2 * HEAD_DIM)
    return (dsa.reshape(b, t, 3, HEAD_DIM), nsa.reshape(b, t, 4, HEAD_DIM),
            moba.reshape(b, t, 2, N_HEADS, HEAD_DIM), win.reshape(b, t, 2, HEAD_DIM))


def _prompt_layer(x, mem, p):
    b, t, d = x.shape
    zh2d, gate = _projections(x.reshape(b * t, d), jnp.arange(t, dtype=jnp.int32), p["ln"][0], p["w_heads"],
                              p["cfg"], p["w_gate"], tm=256)
    zh = zh2d.reshape(b, t, N_HEADCOLS)
    o_a = _dsa_call(zh, zh, zh, kv_blk=BLK_KV_A, ik_blk=BLK_IK, tq=128, n_keys=t)
    cmp = _nsa_compress(zh, p["cmp_w"], blk=BLK_CMP)
    o_b = _nsa_call(zh, cmp, zh, zh, sel_blk=BLK_SEL, win_blk=BLK_WIN, tq=128, n_keys=t)
    o_c = _moba_call(zh, zh, tq=MOBA_BLOCK)
    mkv = _memory_kv(mem, p["ln"][2], p["w_mem_kv"], p["g_mem"][1])
    o_m = _mem_call(zh, mkv, tq=256)
    outs = [o.reshape(b * t, o.shape[-1]) for o in (o_a, o_b, o_c, o_m)]
    x1 = _combine(x.reshape(b * t, d), outs, gate, p["w_branch"], p["w_out"], tm=512).reshape(b, t, d)
    hist = jnp.zeros((b, CONV_WIDTH - 1, p["w_up"].shape[1]), F32)
    y, conv = _conv_ffn(x1, hist, p["ln"][1], p["w_up"], p["conv_w"], p["conv_b"], p["w_down"], tm=1024)
    dsa, nsa, moba, win = _new_rows(zh)
    keep = min(NSA_WINDOW, t)
    return y, dsa, nsa, moba, win[:, t - keep:], mkv.reshape(b, mkv.shape[1], 2, N_HEADS, HEAD_DIM), conv


def _first_row(x, n):
    return jnp.pad(x, ((0, 0), (0, n - 1)) + ((0, 0),) * (x.ndim - 2))


def _sample_layer(x, layer, caches, page_table, win_state, mem_kv, conv_hist, p):
    b, _, d = x.shape
    cache_dsa, cache_nsa, cache_moba = caches
    page = cache_dsa.shape[2]
    past = page_table.shape[1] * page
    x2d = x.reshape(b, d)
    zh, gate = _projections(x2d, jnp.full((b,), past, jnp.int32), p["ln"][0], p["w_heads"], p["cfg"],
                            p["w_gate"], tm=b)
    zq = _first_row(zh[:, None, :], DEC_ROWS)
    dsa, nsa, moba, win = _new_rows(zh[:, None, :])
    o_a = _dsa_decode_call(zq, cache_dsa, layer, page_table, _first_row(dsa.reshape(b, 1, -1), DEC_KEY_CHUNK))
    win_all = jnp.concatenate([win_state.reshape(b, -1, 2 * HEAD_DIM), win.reshape(b, 1, 2 * HEAD_DIM)], axis=1)
    w_pad = -(-win_all.shape[1] // LANES) * LANES
    win_rows = jnp.pad(win_all, ((0, 0), (0, w_pad - win_all.shape[1]), (0, 0)))
    o_b = _nsa_decode_call(zq, cache_nsa, layer, page_table, _first_row(nsa.reshape(b, 1, -1), DEC_KEY_CHUNK),
                           win_rows, p["cmp_w"])
    o_c = _moba_decode_call(zq, cache_moba, layer, page_table, _first_row(moba.reshape(b, 1, -1), page))
    o_m = _mem_call(zq, mem_kv.reshape(b, mem_kv.shape[1], -1), tq=DEC_ROWS)
    outs = [o[:, 0, :] for o in (o_a, o_b, o_c, o_m)]
    x1 = _combine(x2d, outs, gate, p["w_branch"], p["w_out"], tm=b)
    y, conv = _conv_ffn_rows(x1, conv_hist, p["ln"][1], p["w_up"], p["conv_w"], p["conv_b"], p["w_down"])
    keep = win_state.shape[1]
    win_new = win_all[:, win_all.shape[1] - keep:].reshape(b, keep, 2, HEAD_DIM)
    return y.reshape(b, 1, d), dsa, nsa, moba, win_new, conv


def kernel(x_prompt, x_sample, cache_dsa, cache_nsa, cache_moba, state_nsa_win, cache_mem, state_ffn_conv,
           page_table, mem_prompt, ln, w_in, g_dsa, g_nsa, g_moba, g_mem, w_mem_kv, w_cmp1, w_cmp2, pe_cmp,
           w_branch, w_out, w_up, conv_w, conv_b, w_down):
    depth = ln.shape[0]
    caches = tuple(c.reshape(*c.shape[:3], -1) for c in (cache_dsa, cache_nsa, cache_moba))
    xp, xs = x_prompt, x_sample
    outs_p = [[] for _ in range(6)]
    outs_s = [[] for _ in range(5)]
    for l in range(depth):
        w_heads, cfg = _head_weights(w_in[l], g_dsa[l], g_nsa[l], g_moba[l], g_mem[l])
        p = dict(ln=ln[l], w_heads=w_heads, cfg=cfg, w_gate=w_in[l][:, GATE_ORIG:].astype(BF16), g_mem=g_mem[l],
                 w_mem_kv=w_mem_kv[l], cmp_w=_compress_weights(w_cmp1[l], w_cmp2[l], pe_cmp[l]),
                 w_branch=w_branch[l].astype(BF16), w_out=w_out[l].astype(BF16), w_up=w_up[l].astype(BF16),
                 conv_w=conv_w[l], conv_b=conv_b[l], w_down=w_down[l].astype(BF16))
        xp, *rest = _prompt_layer(xp, mem_prompt, p)
        for acc, r in zip(outs_p, rest):
            acc.append(r)
        xs, *rest = _sample_layer(xs, l, caches, page_table, state_nsa_win[l], cache_mem[l], state_ffn_conv[l], p)
        for acc, r in zip(outs_s, rest):
            acc.append(r)
    dsa_p, nsa_p, moba_p, win_p, memkv_p, conv_p = [jnp.stack(a) for a in outs_p]
    dsa_s, nsa_s, moba_s, win_s, conv_s = [jnp.stack(a) for a in outs_s]
    return (xp, xs, dsa_p, dsa_s, nsa_p, nsa_s, moba_p, moba_s, win_p, win_s, memkv_p, conv_p, conv_s)
```

```python
import functools
import math

import numpy as np
import jax
import jax.numpy as jnp
from jax import lax
from jax.experimental import pallas as pl
from jax.experimental.pallas import tpu as pltpu

HEAD_DIM = 64
ROPE_DIM = HEAD_DIM // 4
ROPE_THETA = 500000.0
N_HEADS = 4
DSA_TOPK = 256
NSA_CMP_LEN = 32
NSA_CMP_STRIDE = 16
NSA_SEL_BLOCK = 64
NSA_SEL_TOPN = 16
NSA_WINDOW = 512
MOBA_BLOCK = 256
MOBA_TOPK = 3
N_BRANCH = 4
CONV_WIDTH = 3
RMS_EPS = 1e-6

LANES = 128
SUBLANES = 8
VMEM_LIMIT = 56 * 1024 * 1024
DEC_ROWS = SUBLANES
KEY_CHUNK = 512
DEC_KEY_CHUNK = 2048
PAGES_PER_STEP = 8

F32 = jnp.float32
BF16 = jnp.bfloat16
NEG_INF = float("-inf")
SCALE = HEAD_DIM ** -0.5

KEY_NEG_INF = int(np.uint32(0xFF800000) ^ np.uint32(0x7FFFFFFF)) - 2 ** 32
KEY_POS_INF = 0x7F800000
INT_MIN = -2 ** 31
STRIP = 32
GROUP = 256

_NT = (((1,), (1,)), ((), ()))

_ORIG = dict(q_a=0, k_a=256, v_a=320, iq=384, ik=640, iw=704, q_b=708, kc=964, vc=1028, ks=1092, vs=1156,
             kw=1220, vw=1284, g_b=1348, q_c=1360, k_c=1616, v_c=1872, q_m=2128)
GATE_ORIG = 2384
_NEW = dict(q_a=0, iq=256, q_b=512, q_c=768, q_m=1024, k_c=1280, v_c=1536, k_a=1792, v_a=1856, ik=1920,
            iw=1984, kc=2048, vc=2112, ks=2176, vs=2240, kw=2304, vw=2368, g_b=2432)
_WIDTH = dict(q_a=256, iq=256, q_b=256, q_c=256, q_m=256, k_c=256, v_c=256, k_a=64, v_a=64, ik=64, iw=4,
              kc=64, vc=64, ks=64, vs=64, kw=64, vw=64, g_b=12)
N_HEADCOLS = 2560
_NORMED = ("q_a", "q_b", "q_c", "q_m", "k_c", "k_a", "kc", "ks", "kw")
_ROPED = ("q_a", "iq", "q_b", "q_c", "k_c", "k_a", "ik", "kc", "ks", "kw")
_SIGMOID = ("g_b",)

BLK_KV_A, BLK_IK, BLK_CMP, BLK_SEL, BLK_WIN, BLK_GB = 14, 15, 16, 17, 18, 19
BLK_KC, BLK_VC = 10, 12


def _head_layout():
    masks = {k: np.zeros((N_HEADCOLS,), np.float32) for k in ("norm", "rope", "sig")}
    for name, new in _NEW.items():
        w = _WIDTH[name]
        if name in _NORMED:
            masks["norm"][new:new + w] = 1
        if name in _ROPED:
            masks["rope"][new:new + w] = 1
        if name in _SIGMOID:
            masks["sig"][new:new + w] = 1
    return masks


_COL_MASKS = _head_layout()


def _chunk_any(mask):
    return tuple(bool(mask[c * LANES:(c + 1) * LANES].any()) for c in range(mask.shape[0] // LANES))


def _params(sem):
    return pltpu.CompilerParams(dimension_semantics=sem, vmem_limit_bytes=VMEM_LIMIT)


def _lane_iota(n=LANES):
    return lax.broadcasted_iota(jnp.int32, (1, n), 1)


def _row_iota(n):
    return lax.broadcasted_iota(jnp.int32, (n, 1), 0)


def _split_bf16(x):
    hi = x.astype(BF16)
    lo = (x - hi.astype(F32)).astype(BF16)
    return hi, lo


def _dot(a, b):
    return jnp.dot(a, b, preferred_element_type=F32)


def _dot_nt(a, b):
    return lax.dot_general(a, b, _NT, preferred_element_type=F32)


def _proj_kernel(x_ref, g_ref, w_ref, cfg_ref, rope_ref, o_ref, *, norm_chunks, rope_chunks, sig_chunks):
    x = x_ref[...]
    xn = x * lax.rsqrt(jnp.mean(x * x, axis=-1, keepdims=True) + RMS_EPS) * g_ref[...]
    z = _dot(xn.astype(BF16), w_ref[...])
    rr = lax.broadcasted_iota(jnp.int32, (LANES, LANES), 0) // HEAD_DIM
    cc = lax.broadcasted_iota(jnp.int32, (LANES, LANES), 1) // HEAD_DIM
    seg = (rr == cc).astype(BF16)
    for c in range(z.shape[1] // LANES):
        sl = slice(c * LANES, (c + 1) * LANES)
        zc = z[:, sl]
        if norm_chunks[c]:
            hi, lo = _split_bf16(zc * zc)
            ss = _dot(hi, seg) + _dot(lo, seg)
            r = lax.rsqrt(ss * (1.0 / HEAD_DIM) + RMS_EPS)
            zc = jnp.where(cfg_ref[0:1, sl] > 0, zc * r * cfg_ref[1:2, sl], zc)
        if rope_chunks[c]:
            rot = (zc * rope_ref[:, 0:LANES]
                   + pltpu.roll(zc, LANES - ROPE_DIM // 2, 1) * rope_ref[:, LANES:2 * LANES]
                   + pltpu.roll(zc, ROPE_DIM // 2, 1) * rope_ref[:, 2 * LANES:3 * LANES])
            zc = jnp.where(cfg_ref[2:3, sl] > 0, rot, zc)
        if sig_chunks[c]:
            zc = jnp.where(cfg_ref[3:4, sl] > 0, jax.nn.sigmoid(zc), zc)
        o_ref[:, sl] = zc


def _project(x2d, g, w_bf16, cfg, rope, masks, *, tm, tn):
    m, d = x2d.shape
    n = w_bf16.shape[1]
    tn = min(tn, n)
    tm = min(tm, m)
    flags = {k: _chunk_any(v) for k, v in masks.items()}
    per_tile = tn // LANES
    for k, v in flags.items():
        assert all(v[t * per_tile:(t + 1) * per_tile] == v[:per_tile] for t in range(n // tn)), k
    rope_tiles = rope.shape[0] // tm
    kern = functools.partial(_proj_kernel, norm_chunks=flags["norm"][:per_tile],
                             rope_chunks=flags["rope"][:per_tile], sig_chunks=flags["sig"][:per_tile])
    return pl.pallas_call(
        kern,
        grid=(m // tm, n // tn),
        in_specs=[pl.BlockSpec((tm, d), lambda i, j: (i, 0)),
                  pl.BlockSpec((1, d), lambda i, j: (0, 0)),
                  pl.BlockSpec((d, tn), lambda i, j: (0, j)),
                  pl.BlockSpec((SUBLANES, tn), lambda i, j: (0, j)),
                  pl.BlockSpec((tm, 3 * LANES), lambda i, j: (i % rope_tiles, 0))],
        out_specs=pl.BlockSpec((tm, tn), lambda i, j: (i, j)),
        out_shape=jax.ShapeDtypeStruct((m, n), F32),
        compiler_params=_params(("parallel", "arbitrary")),
        name="project",
    )(x2d, g, w_bf16, cfg, rope)


def _rope_table(pos):
    half = ROPE_DIM // 2
    inv_freq = ROPE_THETA ** (-jnp.arange(half, dtype=F32) / half)
    ang = pos.astype(F32)[:, None] * inv_freq[None, :]
    cos, sin = jnp.cos(ang), jnp.sin(ang)
    t = pos.shape[0]
    ones = jnp.ones((t, HEAD_DIM - ROPE_DIM), F32)
    zeros = jnp.zeros((t, HEAD_DIM - ROPE_DIM), F32)
    zh = jnp.zeros((t, half), F32)
    c64 = jnp.concatenate([cos, cos, ones], axis=1)
    s1 = jnp.concatenate([-sin, zh, zeros], axis=1)
    s2 = jnp.concatenate([zh, sin, zeros], axis=1)
    return jnp.concatenate([c64, c64, s1, s1, s2, s2], axis=1)


def _head_weights(w_in_l, g_dsa, g_nsa, g_moba, g_mem):
    d = w_in_l.shape[0]
    pieces, at = [], 0
    for name, new in sorted(_NEW.items(), key=lambda kv: kv[1]):
        if new > at:
            pieces.append(jnp.zeros((d, new - at), w_in_l.dtype))
        pieces.append(w_in_l[:, _ORIG[name]:_ORIG[name] + _WIDTH[name]])
        at = new + _WIDTH[name]
    pieces.append(jnp.zeros((d, N_HEADCOLS - at), w_in_l.dtype))
    w = jnp.concatenate(pieces, axis=1).astype(BF16)
    gain = jnp.ones((N_HEADCOLS,), F32)
    for name, gvec in (("q_a", g_dsa[0]), ("k_a", g_dsa[1]), ("q_b", g_nsa[0]), ("kc", g_nsa[1]),
                       ("ks", g_nsa[2]), ("kw", g_nsa[3]), ("q_c", g_moba[0]), ("k_c", g_moba[1]),
                       ("q_m", g_mem[0])):
        reps = _WIDTH[name] // HEAD_DIM
        gain = lax.dynamic_update_slice(gain, jnp.tile(gvec.astype(F32), reps), (_NEW[name],))
    zero = jnp.zeros((N_HEADCOLS,), F32)
    cfg = jnp.stack([jnp.asarray(_COL_MASKS["norm"]), gain, jnp.asarray(_COL_MASKS["rope"]),
                     jnp.asarray(_COL_MASKS["sig"]), zero, zero, zero, zero])
    return w, cfg


def _gate_cfg(n):
    z = jnp.zeros((n,), F32)
    o = jnp.ones((n,), F32)
    return jnp.stack([z, o, z, o, z, z, z, z])


def _projections(x2d, pos, ln0, w_heads, cfg, w_gate, *, tm):
    d = x2d.shape[1]
    rope = _rope_table(pos)
    g = ln0.reshape(1, d)
    zh = _project(x2d, g, w_heads, cfg, rope, _COL_MASKS, tm=tm, tn=N_HEADCOLS)
    n_g = w_gate.shape[1]
    gmask = dict(norm=np.zeros((n_g,), np.float32), rope=np.zeros((n_g,), np.float32),
                 sig=np.ones((n_g,), np.float32))
    gate = _project(x2d, g, w_gate, _gate_cfg(n_g), rope, gmask, tm=tm, tn=1024)
    return zh, gate


def _stack_heads(x256):
    lane = _lane_iota()
    parts = []
    for c in range(2):
        ch = x256[:, c * LANES:(c + 1) * LANES]
        parts.append(jnp.where(lane < HEAD_DIM, ch, 0.0))
        parts.append(jnp.where(lane < HEAD_DIM, pltpu.roll(ch, HEAD_DIM, 1), 0.0))
    return jnp.concatenate(parts, axis=0)


def _unstack_heads_t(o_t, tq):
    o = o_t.T
    lane = _lane_iota()
    chunks = []
    for c in range(2):
        even, odd = o[2 * c * tq:(2 * c + 1) * tq], o[(2 * c + 1) * tq:(2 * c + 2) * tq]
        chunks.append(jnp.where(lane < HEAD_DIM, pltpu.roll(even, HEAD_DIM, 1), odd))
    return jnp.concatenate(chunks, axis=1)


def _pair_heads(q128):
    lane = _lane_iota()
    return jnp.concatenate([jnp.where(lane < HEAD_DIM, q128, 0.0), jnp.where(lane >= HEAD_DIM, q128, 0.0)], axis=0)


def _rows_t(x):
    tq = x.shape[0]
    if tq < LANES:
        x = jnp.concatenate([x, jnp.zeros((LANES - tq, x.shape[1]), x.dtype)], axis=0)
    return x.T[:, 0:tq]


def _tile_lanes(x, n):
    return jnp.concatenate([x] * n, axis=1)


def _tsoftmax_init(m_ref, l_ref, acc_ref):
    m_ref[...] = jnp.full(m_ref.shape, NEG_INF, F32)
    l_ref[...] = jnp.zeros(l_ref.shape, F32)
    acc_ref[...] = jnp.zeros(acc_ref.shape, F32)


def _attend_loop(n_chunks, n_max, score_fn, mask_fn, load_vt, s_ref, p_ref, m_ref, l_ref, acc_ref, *, reps):
    kc, r_cols = p_ref.shape
    w = r_cols // reps
    _tsoftmax_init(m_ref, l_ref, acc_ref)

    def scores_into(half, c):
        cc = jnp.minimum(c, n_max - 1)
        for g0 in range(0, kc, GROUP):
            s_ref[half * kc + g0:half * kc + g0 + GROUP, :] = score_fn(cc, g0, GROUP)

    def consume(half, c):
        live = c < n_chunks
        cc = jnp.minimum(c, n_max - 1)
        strip_mask = mask_fn(cc)
        mx = jnp.full((STRIP, r_cols), NEG_INF, F32)
        for r0 in range(0, kc, STRIP):
            at = slice(half * kc + r0, half * kc + r0 + STRIP)
            s = s_ref[at, :]
            ok = strip_mask(r0, STRIP) & live
            s = jnp.concatenate([jnp.where(ok, s[:, k * w:(k + 1) * w], NEG_INF) for k in range(reps)], axis=1)
            s_ref[at, :] = s
            mx = jnp.maximum(mx, s)
        m_old = m_ref[...]
        m_new = jnp.maximum(m_old, jnp.max(mx, axis=0, keepdims=True))
        m_safe = jnp.where(m_new == NEG_INF, 0.0, m_new)
        tot = jnp.zeros((STRIP, r_cols), F32)
        for r0 in range(0, kc, STRIP):
            p = jnp.exp(s_ref[half * kc + r0:half * kc + r0 + STRIP, :] - m_safe)
            tot = tot + p
            p_ref[r0:r0 + STRIP, :] = p.astype(BF16)
        alpha = jnp.exp(m_old - m_safe)
        l_ref[...] = alpha * l_ref[...] + jnp.sum(tot, axis=0, keepdims=True)
        acc_ref[...] = alpha * acc_ref[...] + _dot(load_vt(cc), p_ref[...])
        m_ref[...] = m_new

    scores_into(0, 0)

    def body(pair, carry):
        scores_into(1, 2 * pair + 1)
        consume(0, 2 * pair)
        scores_into(0, 2 * pair + 2)
        consume(1, 2 * pair + 1)
        return carry

    lax.fori_loop(0, (n_chunks + 1) // 2, body, 0)


def _tsoftmax_result(l_ref, acc_ref):
    return acc_ref[...] / jnp.maximum(l_ref[...], 1e-30)


def _tmasked_softmax(s_t, valid_t):
    s = jnp.where(valid_t, s_t, NEG_INF)
    m = jnp.max(s, axis=0, keepdims=True)
    m = jnp.where(m == NEG_INF, 0.0, m)
    e = jnp.exp(s - m)
    return e / jnp.maximum(jnp.sum(e, axis=0, keepdims=True), 1e-30)


def _attend_scratch(d, r, kc):
    return [pltpu.VMEM((2 * kc, r), F32), pltpu.VMEM((kc, r), BF16),
            pltpu.VMEM((1, r), F32), pltpu.VMEM((1, r), F32), pltpu.VMEM((d, r), F32)]


def _chunk_loader(ref, kc):
    def load(c, r0=0, n=kc):
        rows = pl.ds(pl.multiple_of(c * kc + r0, math.gcd(kc, n)), n)
        x = ref[0, rows, :] if len(ref.shape) == 3 else ref[rows, :]
        return x.astype(BF16)
    return load


def _fill_transposed(src_ref, dst_ref, kc):
    def body(c, carry):
        dst_ref[c] = src_ref[0, pl.ds(pl.multiple_of(c * kc, kc), kc), :].T.astype(BF16)
        return carry
    lax.fori_loop(0, dst_ref.shape[0], body, 0)


def _rank_rows(val_t, src_ref, n_rows):
    src_ref[...] = val_t
    j = _row_iota(val_t.shape[0])

    def body(i, rank):
        row = src_ref[pl.ds(i, 1), :]
        beats = (row > val_t) | ((row == val_t) & (i < j))
        return rank + jnp.where(beats, 1, 0)

    return lax.fori_loop(0, n_rows, body, jnp.zeros(val_t.shape, jnp.int32))


def _select_top_t(val_t, rank_ref, n_valid, n_sel):
    rank = _rank_rows(val_t, rank_ref, n_valid)
    return jnp.where((rank < n_sel) & (val_t > NEG_INF), 1.0, 0.0)


def _dsa_core(q, iq, iw, load_kv, load_ik, load_kvt, key_ref, s_ref, p_ref, m_ref, l_ref, acc_ref,
              *, tq, kc, n_top, p0, idx_bits):
    n_chunks = (p0 + tq - 1) // kc + 1
    pos = p0 + _lane_iota(tq)
    iqs_t = _stack_heads(iq).T.astype(BF16)
    qs_t = (_stack_heads(q) * SCALE).T.astype(BF16)
    iw_t = _rows_t(iw)
    w_rows = [iw_t[HEAD_DIM + h:HEAD_DIM + h + 1, :] for h in range(N_HEADS)]

    def chunk_pos(c):
        return c * kc + _row_iota(kc)

    def score_chunk(c, carry):
        lg = _dot(load_ik(c), iqs_t)
        sc = w_rows[0] * jnp.maximum(lg[:, 0:tq], 0.0)
        for h in range(1, N_HEADS):
            sc = sc + w_rows[h] * jnp.maximum(lg[:, h * tq:(h + 1) * tq], 0.0)
        sc = jnp.where(chunk_pos(c) <= pos, sc, NEG_INF)
        bits = pltpu.bitcast(sc, jnp.int32)
        key_ref[c] = bits ^ ((bits >> 31) & 0x7FFFFFFF)
        return carry

    lax.fori_loop(0, n_chunks, score_chunk, 0)

    def count(pred):
        def body(c, acc):
            hit = jnp.where(pred(key_ref[c], chunk_pos(c)), 1, 0)
            parts = [hit[r * SUBLANES:(r + 1) * SUBLANES] for r in range(kc // SUBLANES)]
            while len(parts) > 1:
                parts = [a + b for a, b in zip(parts[0::2], parts[1::2])]
            return acc + parts[0]
        acc = lax.fori_loop(0, n_chunks, body, jnp.zeros((SUBLANES, tq), jnp.int32))
        return jnp.sum(acc, axis=0, keepdims=True)

    def thr_bit(b, thr):
        cand = thr + jnp.left_shift(jnp.int32(1), 31 - b)
        cnt = count(lambda key, kpos: key >= cand)
        return jnp.where(cnt >= n_top, cand, thr)

    thr = lax.fori_loop(0, 32, thr_bit, jnp.full((1, tq), INT_MIN, jnp.int32))

    def last_tied():
        need = n_top - count(lambda key, kpos: key > thr)

        def idx_bit(b, last):
            cand = last + jnp.left_shift(jnp.int32(1), idx_bits - 1 - b)
            cnt = count(lambda key, kpos: (key == thr) & (kpos < cand))
            return jnp.where(cnt < need, cand, last)

        return lax.fori_loop(0, idx_bits, idx_bit, jnp.zeros((1, tq), jnp.int32))

    over = (count(lambda key, kpos: key >= thr) > n_top) & (thr > KEY_NEG_INF)
    last = lax.cond(jnp.max(jnp.where(over, 1, 0)) > 0, last_tied,
                    lambda: jnp.full((1, tq), 2 ** idx_bits, jnp.int32))
    last = jnp.where(thr > KEY_NEG_INF, last, -1)

    def selected(c):
        def strip(r0, rows):
            key = key_ref[c, r0:r0 + rows, :]
            kpos = c * kc + r0 + _row_iota(rows)
            return ((key > thr) | ((key == thr) & (kpos <= last))) & (key < KEY_POS_INF)
        return strip

    _attend_loop(n_chunks, key_ref.shape[0], lambda c, r0, n: _dot(load_kv(c, r0, n), qs_t), selected, load_kvt,
                 s_ref, p_ref, m_ref, l_ref, acc_ref, reps=N_HEADS)
    return _unstack_heads_t(_tsoftmax_result(l_ref, acc_ref), tq)


def _dsa_kernel(q_ref, iq_ref, iw_ref, kv_ref, ik_ref, o_ref, kvt_ref, key_ref, s_ref, p_ref, m_ref, l_ref, acc_ref,
                *, tq, kc, n_top, idx_bits):
    i = pl.program_id(1)

    @pl.when(i == 0)
    def _():
        _fill_transposed(kv_ref, kvt_ref, kc)

    o_ref[0] = _dsa_core(q_ref[0], iq_ref[0], iw_ref[0], _chunk_loader(kv_ref, kc), _chunk_loader(ik_ref, kc),
                         lambda c: kvt_ref[c], key_ref, s_ref, p_ref, m_ref, l_ref, acc_ref,
                         tq=tq, kc=kc, n_top=n_top, p0=i * tq, idx_bits=idx_bits)


def _row_softmax_step(rows, s, valid, v_bf16, m_ref, l_ref, acc_ref):
    s = jnp.where(valid, s, NEG_INF)
    m_old = m_ref[rows, :]
    m_new = jnp.maximum(m_old, jnp.max(s, axis=-1, keepdims=True))
    m_safe = jnp.where(m_new == NEG_INF, 0.0, m_new)
    p = jnp.exp(s - m_safe)
    alpha = jnp.exp(m_old - m_safe)
    l_ref[rows, :] = alpha * l_ref[rows, :] + jnp.sum(p, axis=-1, keepdims=True)
    acc_ref[rows, :] = alpha * acc_ref[rows, :] + _dot(p.astype(BF16), v_bf16)
    m_ref[rows, :] = m_new


def _row_softmax_init(m_ref, l_ref, acc_ref):
    m_ref[...] = jnp.full(m_ref.shape, NEG_INF, F32)
    l_ref[...] = jnp.zeros(l_ref.shape, F32)
    acc_ref[...] = jnp.zeros(acc_ref.shape, F32)


def _row_softmax_result(rows, l_ref, acc_ref):
    return acc_ref[rows, :] / jnp.maximum(l_ref[rows, :], 1e-30)


def _row_masked_softmax(s, valid):
    s = jnp.where(valid, s, NEG_INF)
    m = jnp.max(s, axis=-1, keepdims=True)
    m = jnp.where(m == NEG_INF, 0.0, m)
    e = jnp.exp(s - m)
    return e / jnp.maximum(jnp.sum(e, axis=-1, keepdims=True), 1e-30)


def _row_softmax_scratch(rows, d):
    return [pltpu.VMEM((rows, 1), F32), pltpu.VMEM((rows, 1), F32), pltpu.VMEM((rows, d), F32)]


def _unstack_heads(o_heads):
    lane = _lane_iota()
    chunks = []
    for c in range(2):
        chunks.append(jnp.where(lane < HEAD_DIM, pltpu.roll(o_heads[2 * c], HEAD_DIM, 1), o_heads[2 * c + 1]))
    return jnp.concatenate(chunks, axis=1)


def _dsa_rows_core(q, iq, iw, load_kv, load_ik, key_ref, m_ref, l_ref, acc_ref, *, tq, kc, n_top, p0, idx_bits):
    n_chunks = (p0 + tq - 1) // kc + 1
    pos = p0 + _row_iota(tq)
    iqs = _stack_heads(iq).astype(BF16)
    qs = (_stack_heads(q) * SCALE).astype(BF16)
    w_cols = [iw[:, HEAD_DIM + h:HEAD_DIM + h + 1] for h in range(N_HEADS)]

    def chunk_pos(c):
        return c * kc + _lane_iota(kc)

    def score_chunk(c, carry):
        lg = _dot_nt(iqs, load_ik(c))
        sc = w_cols[0] * jnp.maximum(lg[0:tq], 0.0)
        for h in range(1, N_HEADS):
            sc = sc + w_cols[h] * jnp.maximum(lg[h * tq:(h + 1) * tq], 0.0)
        sc = jnp.where(chunk_pos(c) <= pos, sc, NEG_INF)
        bits = pltpu.bitcast(sc, jnp.int32)
        key_ref[c] = bits ^ ((bits >> 31) & 0x7FFFFFFF)
        return carry

    lax.fori_loop(0, n_chunks, score_chunk, 0)

    def count(pred):
        def body(c, acc):
            hit = jnp.where(pred(key_ref[c], chunk_pos(c)), 1, 0)
            parts = [hit[:, t * LANES:(t + 1) * LANES] for t in range(kc // LANES)]
            while len(parts) > 1:
                parts = [a + b for a, b in zip(parts[0::2], parts[1::2])]
            return acc + parts[0]
        acc = lax.fori_loop(0, n_chunks, body, jnp.zeros((tq, LANES), jnp.int32))
        return jnp.sum(acc, axis=1, keepdims=True)

    def thr_bit(b, thr):
        cand = thr + jnp.left_shift(jnp.int32(1), 31 - b)
        cnt = count(lambda key, kpos: key >= cand)
        return jnp.where(cnt >= n_top, cand, thr)

    thr = lax.fori_loop(0, 32, thr_bit, jnp.full((tq, 1), INT_MIN, jnp.int32))

    def last_tied():
        need = n_top - count(lambda key, kpos: key > thr)

        def idx_bit(b, last):
            cand = last + jnp.left_shift(jnp.int32(1), idx_bits - 1 - b)
            cnt = count(lambda key, kpos: (key == thr) & (kpos < cand))
            return jnp.where(cnt < need, cand, last)

        return lax.fori_loop(0, idx_bits, idx_bit, jnp.zeros((tq, 1), jnp.int32))

    over = (count(lambda key, kpos: key >= thr) > n_top) & (thr > KEY_NEG_INF)
    last = lax.cond(jnp.max(jnp.where(over, 1, 0)) > 0, last_tied,
                    lambda: jnp.full((tq, 1), 2 ** idx_bits, jnp.int32))

    _row_softmax_init(m_ref, l_ref, acc_ref)

    def attend(c, carry):
        kvc = load_kv(c)
        s = _dot_nt(qs, kvc)
        key = key_ref[c]
        sel = (key > thr) | ((key == thr) & (chunk_pos(c) <= last))
        sel = sel & (key > KEY_NEG_INF) & (key < KEY_POS_INF)
        for h in range(N_HEADS):
            _row_softmax_step(pl.ds(h * tq, tq), s[h * tq:(h + 1) * tq], sel, kvc, m_ref, l_ref, acc_ref)
        return carry

    lax.fori_loop(0, n_chunks, attend, 0)
    return _unstack_heads([_row_softmax_result(pl.ds(h * tq, tq), l_ref, acc_ref) for h in range(N_HEADS)])


def _dsa_decode_kernel(pt_ref, q_ref, iq_ref, iw_ref, *rest, tq, kc, n_top, idx_bits, n_pages, page, pps):
    page_refs, (new_ref, o_ref, kv_ref, ik_ref, key_ref, m_ref, l_ref, acc_ref) = rest[:pps], rest[pps:]
    s = pl.program_id(1)
    n_steps = n_pages // pps

    def put(at, rows):
        kv_ref[at, :] = rows[:, 0:LANES].astype(BF16)
        ik = rows[:, LANES:]
        ik_ref[at, :] = jnp.concatenate([ik, jnp.zeros((rows.shape[0], 2 * LANES - rows.shape[1]), F32)],
                                        axis=1).astype(BF16)

    @pl.when(s < n_steps)
    def _():
        for r in range(pps):
            put(pl.ds(pl.multiple_of((s * pps + r) * page, page), page), page_refs[r][0, 0])

    @pl.when(s == n_steps)
    def _():
        put(pl.ds(n_pages * page, kc), new_ref[0])
        o_ref[0] = _dsa_rows_core(q_ref[0], iq_ref[0], iw_ref[0], _chunk_loader(kv_ref, kc), _chunk_loader(ik_ref, kc),
                                  key_ref, m_ref, l_ref, acc_ref, tq=tq, kc=kc, n_top=n_top, p0=n_pages * page,
                                  idx_bits=idx_bits)


def _dsa_scratch(n_kc, tq, kc):
    return ([pltpu.VMEM((n_kc, LANES, kc), BF16), pltpu.VMEM((n_kc, kc, tq), jnp.int32)]
            + _attend_scratch(LANES, N_HEADS * tq, kc))


def _page_specs(cache, layer, n_pages, pps):
    block = (1, 1) + cache.shape[2:]
    zeros = (0,) * (len(cache.shape) - 2)

    def spec(r):
        return pl.BlockSpec(block, lambda bi, s, pt: (layer, pt[bi, jnp.minimum(s * pps + r, n_pages - 1)]) + zeros)

    return [spec(r) for r in range(pps)]


def _dsa_decode_call(zq, cache, layer, page_table, new_rows, *, kc=DEC_KEY_CHUNK, pps=PAGES_PER_STEP):
    b, tq, _ = zq.shape
    n_pages = page_table.shape[1]
    page = cache.shape[2]
    past = n_pages * page
    assert past % kc == 0 and n_pages % pps == 0
    l_rows = past + kc
    n_kc = l_rows // kc
    kern = functools.partial(_dsa_decode_kernel, tq=tq, kc=kc, n_top=min(DSA_TOPK, (past + 1) // 4),
                             idx_bits=max(1, math.ceil(math.log2(l_rows))), n_pages=n_pages, page=page, pps=pps)
    grid_spec = pltpu.PrefetchScalarGridSpec(
        num_scalar_prefetch=1,
        grid=(b, n_pages // pps + 1),
        in_specs=[pl.BlockSpec((1, tq, 2 * LANES), lambda bi, s, pt: (bi, 0, _NEW["q_a"] // 256)),
                  pl.BlockSpec((1, tq, 2 * LANES), lambda bi, s, pt: (bi, 0, _NEW["iq"] // 256)),
                  pl.BlockSpec((1, tq, LANES), lambda bi, s, pt: (bi, 0, BLK_IK))]
        + _page_specs(cache, layer, n_pages, pps)
        + [pl.BlockSpec((1,) + new_rows.shape[1:], lambda bi, s, pt: (bi, 0, 0))],
        out_specs=pl.BlockSpec((1, tq, 2 * LANES), lambda bi, s, pt: (bi, 0, 0)),
        scratch_shapes=[pltpu.VMEM((l_rows, LANES), BF16), pltpu.VMEM((l_rows, LANES), BF16),
                        pltpu.VMEM((n_kc, tq, kc), jnp.int32)] + _row_softmax_scratch(N_HEADS * tq, LANES))
    return pl.pallas_call(
        kern, grid_spec=grid_spec,
        out_shape=jax.ShapeDtypeStruct((b, tq, 2 * LANES), F32),
        compiler_params=_params(("parallel", "arbitrary")),
        name="dsa_decode",
    )(page_table, zq, zq, zq, *([cache] * pps), new_rows)


def _dsa_call(zq, kv, ik, *, kv_blk, ik_blk, tq, n_keys, kc=KEY_CHUNK):
    b, t_q, _ = zq.shape
    l_rows = kv.shape[1]
    n_kc = l_rows // kc
    kern = functools.partial(_dsa_kernel, tq=tq, kc=kc, n_top=min(DSA_TOPK, n_keys // 4),
                             idx_bits=max(1, math.ceil(math.log2(l_rows))))
    return pl.pallas_call(
        kern,
        grid=(b, t_q // tq),
        in_specs=[pl.BlockSpec((1, tq, 2 * LANES), lambda bi, i: (bi, i, _NEW["q_a"] // 256)),
                  pl.BlockSpec((1, tq, 2 * LANES), lambda bi, i: (bi, i, _NEW["iq"] // 256)),
                  pl.BlockSpec((1, tq, LANES), lambda bi, i: (bi, i, BLK_IK)),
                  pl.BlockSpec((1, l_rows, LANES), lambda bi, i: (bi, 0, kv_blk)),
                  pl.BlockSpec((1, l_rows, LANES), lambda bi, i: (bi, 0, ik_blk))],
        out_specs=pl.BlockSpec((1, tq, 2 * LANES), lambda bi, i: (bi, i, 0)),
        out_shape=jax.ShapeDtypeStruct((b, t_q, 2 * LANES), F32),
        scratch_shapes=_dsa_scratch(n_kc, tq, kc),
        compiler_params=_params(("parallel", "arbitrary")),
        name="dsa_attention",
    )(zq, zq, zq, kv, ik)


def _compress(load_rows, wlo_ref, whi_ref, pe_ref, w2_ref, n_blocks):
    a = jnp.zeros((n_blocks, LANES), F32)
    b = jnp.zeros((n_blocks, LANES), F32)
    for p in range(NSA_CMP_STRIDE):
        xp = load_rows(p)
        a = a + _dot((xp + pe_ref[p:p + 1, :]).astype(BF16), wlo_ref[p])
        b = b + _dot((xp + pe_ref[NSA_CMP_STRIDE + p:NSA_CMP_STRIDE + p + 1, :]).astype(BF16), whi_ref[p])
    h = jax.nn.gelu(a + pltpu.roll(b, n_blocks - 1, 0))
    return _dot(h.astype(BF16), w2_ref[...])


def _cmp_kernel(rows_ref, wlo_ref, whi_ref, pe_ref, w2_ref, o_ref, *, n_blocks):
    o_ref[0] = _compress(lambda p: rows_ref[0, pl.ds(p, n_blocks, stride=NSA_CMP_STRIDE), :],
                         wlo_ref, whi_ref, pe_ref, w2_ref, n_blocks)


def _block_diag2(m0, m1):
    z = jnp.zeros_like(m0)
    return jnp.concatenate([jnp.concatenate([m0, z], axis=-1), jnp.concatenate([z, m1], axis=-1)], axis=-2)


def _compress_weights(w_cmp1, w_cmp2, pe_cmp):
    w1 = w_cmp1.reshape(2, NSA_CMP_LEN, HEAD_DIM, HEAD_DIM)
    wlo = _block_diag2(w1[0, :NSA_CMP_STRIDE], w1[1, :NSA_CMP_STRIDE]).astype(BF16)
    whi = _block_diag2(w1[0, NSA_CMP_STRIDE:], w1[1, NSA_CMP_STRIDE:]).astype(BF16)
    pe = jnp.concatenate([pe_cmp[0], pe_cmp[1]], axis=-1)
    w2 = _block_diag2(w_cmp2[0], w_cmp2[1]).astype(BF16)
    return wlo, whi, pe, w2


def _compress_specs(index_map3, index_map2):
    return [pl.BlockSpec((NSA_CMP_STRIDE, LANES, LANES), index_map3),
            pl.BlockSpec((NSA_CMP_STRIDE, LANES, LANES), index_map3),
            pl.BlockSpec((NSA_CMP_LEN, LANES), index_map2),
            pl.BlockSpec((LANES, LANES), index_map2)]


def _nsa_compress(rows, cmp_w, *, blk):
    b, l_rows, _ = rows.shape
    n_blocks = l_rows // NSA_CMP_STRIDE
    return pl.pallas_call(
        functools.partial(_cmp_kernel, n_blocks=n_blocks),
        grid=(b,),
        in_specs=[pl.BlockSpec((1, l_rows, LANES), lambda bi: (bi, 0, blk))]
        + _compress_specs(lambda bi: (0, 0, 0), lambda bi: (0, 0)),
        out_specs=pl.BlockSpec((1, n_blocks, LANES), lambda bi: (bi, 0, 0)),
        out_shape=jax.ShapeDtypeStruct((b, n_blocks, LANES), F32),
        compiler_params=_params(("parallel",)),
        name="nsa_compress",
    )(rows, *cmp_w)


def _nsa_core(q, g, cmp_rows, load_sel, load_selt, wrows, wpos, rank_ref, s_ref, p_ref, m_ref, l_ref, acc_ref,
              *, tq, kc, p0, n_cmp, n_blk, n_sel, n_max):
    pos = p0 + _lane_iota(tq)
    qs_t = (_stack_heads(q) * SCALE).T.astype(BF16)

    ncp = cmp_rows.shape[0]
    cmp_t = cmp_rows.T.astype(BF16)
    s_c = _dot(cmp_rows.astype(BF16), qs_t)
    n_idx = _row_iota(ncp)
    valid_c = (n_idx * NSA_CMP_STRIDE + (NSA_CMP_LEN - 1) <= pos) & (n_idx < n_cmp)
    p_c = [_tmasked_softmax(s_c[:, h * tq:(h + 1) * tq], valid_c) for h in range(N_HEADS)]
    o_c = _dot(cmp_t, jnp.concatenate(p_c, axis=1).astype(BF16))
    p_sum = p_c[0] + p_c[1] + p_c[2] + p_c[3]

    nbp = rank_ref.shape[0]
    jb = lax.broadcasted_iota(jnp.int32, (nbp, ncp), 0)
    nn = lax.broadcasted_iota(jnp.int32, (nbp, ncp), 1)
    cover_t = ((nn * NSA_CMP_STRIDE < (jb + 1) * NSA_SEL_BLOCK)
               & (nn * NSA_CMP_STRIDE + (NSA_CMP_LEN - 1) >= jb * NSA_SEL_BLOCK) & (nn < n_cmp)).astype(BF16)
    hi, lo = _split_bf16(p_sum)
    imp = _dot(cover_t, hi) + _dot(cover_t, lo)
    j = _row_iota(nbp)
    cur = pos // NSA_SEL_BLOCK
    forced = (j == 0) | (j == cur) | (j == cur - 1)
    imp = jnp.where(forced, jnp.inf, jnp.where(j <= cur, imp, NEG_INF))
    n_vis = jnp.minimum((p0 + tq - 1) // NSA_SEL_BLOCK + 1, n_blk)
    sel = _select_top_t(imp, rank_ref, n_vis, n_sel).astype(BF16)

    n_chunks = (p0 + tq - 1) // kc + 1
    ek = lax.broadcasted_iota(jnp.int32, (kc, nbp), 0)
    eb = lax.broadcasted_iota(jnp.int32, (kc, nbp), 1)

    def selected(c):
        expand = (eb == (c * kc + ek) // NSA_SEL_BLOCK).astype(BF16)
        picked = _dot(expand, sel)

        def strip(r0, rows):
            return (picked[r0:r0 + rows] > 0.5) & (c * kc + r0 + _row_iota(rows) <= pos)
        return strip

    _attend_loop(n_chunks, n_max, lambda c, r0, n: _dot(load_sel(c, r0, n), qs_t), selected, load_selt,
                 s_ref, p_ref, m_ref, l_ref, acc_ref, reps=N_HEADS)
    o_s = _tsoftmax_result(l_ref, acc_ref)

    s_w = _dot(wrows.astype(BF16), qs_t)
    valid_w = (wpos <= pos) & (pos - wpos < NSA_WINDOW)
    p_w = [_tmasked_softmax(s_w[:, h * tq:(h + 1) * tq], valid_w) for h in range(N_HEADS)]
    o_w = _dot(wrows.T.astype(BF16), jnp.concatenate(p_w, axis=1).astype(BF16))

    g_t = _rows_t(g)
    mixed = []
    for h in range(N_HEADS):
        cols = slice(h * tq, (h + 1) * tq)
        mixed.append(g_t[3 * h:3 * h + 1] * o_c[:, cols] + g_t[3 * h + 1:3 * h + 2] * o_s[:, cols]
                     + g_t[3 * h + 2:3 * h + 3] * o_w[:, cols])
    return _unstack_heads_t(jnp.concatenate(mixed, axis=1), tq)


def _nsa_kernel(q_ref, g_ref, cmp_ref, sel_ref, win_ref, o_ref, selt_ref, rank_ref, s_ref, p_ref, m_ref, l_ref, acc_ref,
                *, tq, kc, n_cmp, n_blk, n_sel, win_rows):
    i = pl.program_id(1)

    @pl.when(i == 0)
    def _():
        _fill_transposed(sel_ref, selt_ref, kc)

    p0 = i * tq
    start = pl.multiple_of(jnp.clip(p0 - NSA_WINDOW, 0, win_ref.shape[1] - win_rows), SUBLANES)
    o_ref[0] = _nsa_core(q_ref[0], g_ref[0], cmp_ref[0], _chunk_loader(sel_ref, kc), lambda c: selt_ref[c],
                         win_ref[0, pl.ds(start, win_rows), :], start + _row_iota(win_rows),
                         rank_ref, s_ref, p_ref, m_ref, l_ref, acc_ref,
                         tq=tq, kc=kc, p0=p0, n_cmp=n_cmp, n_blk=n_blk, n_sel=n_sel, n_max=selt_ref.shape[0])


def _top_mask_row(v, n_sel):
    n = v.shape[1]
    i = lax.broadcasted_iota(jnp.int32, (n, n), 0)
    j = lax.broadcasted_iota(jnp.int32, (n, n), 1)
    v_col = jnp.sum(jnp.where(i == j, v, 0.0), axis=1, keepdims=True)
    beats = (v_col > v) | ((v_col == v) & (i < j))
    rank = jnp.sum(jnp.where(beats, 1, 0), axis=0, keepdims=True)
    return jnp.where((rank < n_sel) & (v > NEG_INF), 1.0, 0.0)


def _nsa_rows_core(q, g, cmpv, load_sel, wrows, wpos, m_ref, l_ref, acc_ref, *, tq, kc, p0, n_cmp, n_blk, n_sel):
    pos = p0 + _row_iota(tq)
    qs = (_stack_heads(q) * SCALE).astype(BF16)

    ncp = cmpv.shape[0]
    s_c = _dot_nt(qs, cmpv)
    n_idx = _lane_iota(ncp)
    valid_c = (n_idx * NSA_CMP_STRIDE + (NSA_CMP_LEN - 1) <= pos) & (n_idx < n_cmp)
    o_c = []
    p_sum = jnp.zeros((tq, ncp), F32)
    for h in range(N_HEADS):
        p = _row_masked_softmax(s_c[h * tq:(h + 1) * tq], valid_c)
        p_sum = p_sum + p
        o_c.append(_dot(p.astype(BF16), cmpv))

    nbp = -(-n_blk // LANES) * LANES
    nn = lax.broadcasted_iota(jnp.int32, (ncp, nbp), 0)
    jb = lax.broadcasted_iota(jnp.int32, (ncp, nbp), 1)
    cover = ((nn * NSA_CMP_STRIDE < (jb + 1) * NSA_SEL_BLOCK)
             & (nn * NSA_CMP_STRIDE + (NSA_CMP_LEN - 1) >= jb * NSA_SEL_BLOCK) & (nn < n_cmp)).astype(BF16)
    hi, lo = _split_bf16(p_sum)
    imp = _dot(hi, cover) + _dot(lo, cover)
    j = _lane_iota(nbp)
    cur = pos // NSA_SEL_BLOCK
    forced = (j == 0) | (j == cur) | (j == cur - 1)
    imp = jnp.where(forced, jnp.inf, jnp.where(j <= cur, imp, NEG_INF))
    sel = jnp.broadcast_to(_top_mask_row(imp[0:1], n_sel), (tq, nbp)).astype(BF16)

    _row_softmax_init(m_ref, l_ref, acc_ref)
    n_chunks = (p0 + tq - 1) // kc + 1
    eb = lax.broadcasted_iota(jnp.int32, (nbp, kc), 0)
    ek = lax.broadcasted_iota(jnp.int32, (nbp, kc), 1)

    def attend(c, carry):
        rows = load_sel(c)
        s = _dot_nt(qs, rows)
        expand = (eb == (c * kc + ek) // NSA_SEL_BLOCK).astype(BF16)
        picked = _dot(sel, expand)
        valid = (picked > 0.5) & (c * kc + _lane_iota(kc) <= pos)
        for h in range(N_HEADS):
            _row_softmax_step(pl.ds(h * tq, tq), s[h * tq:(h + 1) * tq], valid, rows, m_ref, l_ref, acc_ref)
        return carry

    lax.fori_loop(0, n_chunks, attend, 0)

    s_w = _dot_nt(qs, wrows)
    valid_w = (wpos <= pos) & (pos - wpos < NSA_WINDOW)
    heads = []
    for h in range(N_HEADS):
        p_w = _row_masked_softmax(s_w[h * tq:(h + 1) * tq], valid_w)
        o_w = _dot(p_w.astype(BF16), wrows)
        o_s = _row_softmax_result(pl.ds(h * tq, tq), l_ref, acc_ref)
        heads.append(g[:, 3 * h:3 * h + 1] * o_c[h] + g[:, 3 * h + 1:3 * h + 2] * o_s
                     + g[:, 3 * h + 2:3 * h + 3] * o_w)
    return _unstack_heads(heads)


def _nsa_decode_kernel(pt_ref, q_ref, g_ref, *rest, tq, kc, n_cmp, n_blk, n_sel, n_pages, page, pps):
    page_refs = rest[:pps]
    (new_ref, win_ref, wlo_ref, whi_ref, pe_ref, w2_ref, o_ref, cbuf_ref, sbuf_ref, m_ref, l_ref, acc_ref) = rest[pps:]
    s = pl.program_id(1)
    n_steps = n_pages // pps

    @pl.when(s < n_steps)
    def _():
        for r in range(pps):
            at = pl.ds(pl.multiple_of((s * pps + r) * page, page), page)
            cbuf_ref[at, :] = page_refs[r][0, 0, :, 0:LANES]
            sbuf_ref[at, :] = page_refs[r][0, 0, :, LANES:2 * LANES].astype(BF16)

    @pl.when(s == n_steps)
    def _():
        past = n_pages * page
        sbuf_ref[pl.ds(past, kc), :] = new_ref[0, :, LANES:2 * LANES].astype(BF16)
        n_blocks = past // NSA_CMP_STRIDE
        cmpv = _compress(lambda r: cbuf_ref[pl.ds(r, n_blocks, stride=NSA_CMP_STRIDE), :],
                         wlo_ref, whi_ref, pe_ref, w2_ref, n_blocks).astype(BF16)
        wrows = win_ref[0].astype(BF16)
        o_ref[0] = _nsa_rows_core(q_ref[0], g_ref[0], cmpv, _chunk_loader(sbuf_ref, kc), wrows,
                                  past - NSA_WINDOW + _lane_iota(wrows.shape[0]), m_ref, l_ref, acc_ref,
                                  tq=tq, kc=kc, p0=past, n_cmp=n_cmp, n_blk=n_blk, n_sel=n_sel)


def _nsa_sizes(n_keys):
    n_cmp = (n_keys - NSA_CMP_LEN) // NSA_CMP_STRIDE + 1
    n_blk = -(-n_keys // NSA_SEL_BLOCK)
    return n_cmp, n_blk, min(NSA_SEL_TOPN, n_blk), -(-n_blk // LANES) * LANES


def _nsa_scratch(n_kc, nbp, tq, kc):
    return ([pltpu.VMEM((n_kc, LANES, kc), BF16), pltpu.VMEM((nbp, tq), F32)]
            + _attend_scratch(LANES, N_HEADS * tq, kc))


def _nsa_decode_call(zq, cache, layer, page_table, new_rows, win_rows, cmp_w, *, kc=DEC_KEY_CHUNK,
                     pps=PAGES_PER_STEP):
    b, tq, _ = zq.shape
    n_pages = page_table.shape[1]
    page = cache.shape[2]
    past = n_pages * page
    assert past % kc == 0 and n_pages % pps == 0
    n_cmp, n_blk, n_sel, _ = _nsa_sizes(past + 1)
    kern = functools.partial(_nsa_decode_kernel, tq=tq, kc=kc, n_cmp=n_cmp, n_blk=n_blk, n_sel=n_sel,
                             n_pages=n_pages, page=page, pps=pps)
    grid_spec = pltpu.PrefetchScalarGridSpec(
        num_scalar_prefetch=1,
        grid=(b, n_pages // pps + 1),
        in_specs=[pl.BlockSpec((1, tq, 2 * LANES), lambda bi, s, pt: (bi, 0, _NEW["q_b"] // 256)),
                  pl.BlockSpec((1, tq, LANES), lambda bi, s, pt: (bi, 0, BLK_GB))]
        + _page_specs(cache, layer, n_pages, pps)
        + [pl.BlockSpec((1,) + new_rows.shape[1:], lambda bi, s, pt: (bi, 0, 0)),
           pl.BlockSpec((1,) + win_rows.shape[1:], lambda bi, s, pt: (bi, 0, 0))]
        + _compress_specs(lambda bi, s, pt: (0, 0, 0), lambda bi, s, pt: (0, 0)),
        out_specs=pl.BlockSpec((1, tq, 2 * LANES), lambda bi, s, pt: (bi, 0, 0)),
        scratch_shapes=[pltpu.VMEM((past, LANES), F32), pltpu.VMEM((past + kc, LANES), BF16)]
        + _row_softmax_scratch(N_HEADS * tq, LANES))
    return pl.pallas_call(
        kern, grid_spec=grid_spec,
        out_shape=jax.ShapeDtypeStruct((b, tq, 2 * LANES), F32),
        compiler_params=_params(("parallel", "arbitrary")),
        name="nsa_decode",
    )(page_table, zq, zq, *([cache] * pps), new_rows, win_rows, *cmp_w)


def _nsa_call(zq, cmp, sel, win, *, sel_blk, win_blk, tq, n_keys, kc=KEY_CHUNK):
    b, t_q, _ = zq.shape
    l_rows = sel.shape[1]
    n_cmp, n_blk, n_sel, nbp = _nsa_sizes(n_keys)
    kern = functools.partial(_nsa_kernel, tq=tq, kc=kc, n_cmp=n_cmp, n_blk=n_blk, n_sel=n_sel,
                             win_rows=min(NSA_WINDOW + tq, l_rows))
    return pl.pallas_call(
        kern,
        grid=(b, t_q // tq),
        in_specs=[pl.BlockSpec((1, tq, 2 * LANES), lambda bi, i: (bi, i, _NEW["q_b"] // 256)),
                  pl.BlockSpec((1, tq, LANES), lambda bi, i: (bi, i, BLK_GB)),
                  pl.BlockSpec((1, cmp.shape[1], LANES), lambda bi, i: (bi, 0, 0)),
                  pl.BlockSpec((1, l_rows, LANES), lambda bi, i: (bi, 0, sel_blk)),
                  pl.BlockSpec((1, l_rows, LANES), lambda bi, i: (bi, 0, win_blk))],
        out_specs=pl.BlockSpec((1, tq, 2 * LANES), lambda bi, i: (bi, i, 0)),
        out_shape=jax.ShapeDtypeStruct((b, t_q, 2 * LANES), F32),
        scratch_shapes=_nsa_scratch(l_rows // kc, nbp, tq, kc),
        compiler_params=_params(("parallel", "arbitrary")),
        name="nsa_attention",
    )(zq, zq, cmp, sel, win)


def _moba_kernel(q_ref, k_ref, v_ref, o_ref, kmean_ref, vt_ref, rank_ref, sel_ref, s_ref, p_ref, m_ref, l_ref, acc_ref,
                 *, tq, n_blocks, n_sel):
    c0 = pl.program_id(2)

    @pl.when(c0 == 0)
    def _():
        kmean_ref[...] = jnp.zeros(kmean_ref.shape, F32)

        def mean_block(blk, carry):
            rows = k_ref[0, pl.ds(pl.multiple_of(blk * MOBA_BLOCK, MOBA_BLOCK), MOBA_BLOCK), :]
            kmean_ref[pl.ds(blk, 1), :] = jnp.sum(rows, axis=0, keepdims=True) * (1.0 / MOBA_BLOCK)
            return carry

        lax.fori_loop(0, n_blocks, mean_block, 0)
        _fill_transposed(v_ref, vt_ref, MOBA_BLOCK)

    nbp = kmean_ref.shape[0]
    qst = _pair_heads(q_ref[0])
    qst_t = qst.T
    q_hi, q_lo = _split_bf16(qst_t)
    k_hi, k_lo = _split_bf16(kmean_ref[...])
    gate = _dot(k_hi, q_hi) + _dot(k_hi, q_lo) + _dot(k_lo, q_hi)
    gate = jnp.where(_row_iota(nbp) < c0, gate, NEG_INF)
    sel_ref[...] = _select_top_t(gate, rank_ref, c0, n_sel)
    qs_t = (qst_t * SCALE).astype(BF16)
    load_k = _chunk_loader(k_ref, MOBA_BLOCK)
    q_in_block = _lane_iota(2 * tq) % tq

    def admissible(blk):
        picked = sel_ref[pl.ds(blk, 1), :] > 0.5

        def strip(r0, rows):
            return picked | ((blk == c0) & (r0 + _row_iota(rows) <= q_in_block))
        return strip

    _attend_loop(c0 + 1, n_blocks, lambda blk, r0, n: _dot(load_k(blk, r0, n), qs_t), admissible,
                 lambda blk: vt_ref[blk],
                 s_ref, p_ref, m_ref, l_ref, acc_ref, reps=1)
    o = _tsoftmax_result(l_ref, acc_ref).T
    o_ref[0] = jnp.where(_lane_iota() < HEAD_DIM, o[0:tq], o[tq:2 * tq])


def _moba_call(zq, kv, *, tq):
    b, t_q, _ = zq.shape
    l_rows = kv.shape[1]
    assert tq == MOBA_BLOCK
    n_blocks = l_rows // MOBA_BLOCK
    nbp = -(-n_blocks // LANES) * LANES
    kern = functools.partial(_moba_kernel, tq=tq, n_blocks=n_blocks, n_sel=min(MOBA_TOPK, n_blocks))
    q_blk0 = _NEW["q_c"] // LANES
    return pl.pallas_call(
        kern,
        grid=(b, 2, t_q // tq),
        in_specs=[pl.BlockSpec((1, tq, LANES), lambda bi, c, i: (bi, i, q_blk0 + c)),
                  pl.BlockSpec((1, l_rows, LANES), lambda bi, c, i: (bi, 0, BLK_KC + c)),
                  pl.BlockSpec((1, l_rows, LANES), lambda bi, c, i: (bi, 0, BLK_VC + c))],
        out_specs=pl.BlockSpec((1, tq, LANES), lambda bi, c, i: (bi, i, c)),
        out_shape=jax.ShapeDtypeStruct((b, t_q, 2 * LANES), F32),
        scratch_shapes=[pltpu.VMEM((nbp, LANES), F32), pltpu.VMEM((n_blocks, LANES, MOBA_BLOCK), BF16),
                        pltpu.VMEM((nbp, 2 * tq), F32), pltpu.VMEM((nbp, 2 * tq), F32)]
        + _attend_scratch(LANES, 2 * tq, MOBA_BLOCK),
        compiler_params=_params(("parallel", "parallel", "arbitrary")),
        name="moba_attention",
    )(zq, kv, kv)


def _moba_pick_kernel(pt_ref, q_ref, *rest, n_blocks, n_sel, n_pages, page, pps):
    page_refs, (o_ref, kmean_ref, rank_ref) = rest[:pps], rest[pps:]
    s = pl.program_id(1)
    per_block = MOBA_BLOCK // page
    n_steps = n_pages // pps
    nbp = kmean_ref.shape[0]

    @pl.when(s == 0)
    def _():
        kmean_ref[...] = jnp.zeros(kmean_ref.shape, F32)

    @pl.when(s < n_steps)
    def _():
        for j in range(pps // per_block):
            tot = None
            for r in range(per_block):
                part = jnp.sum(page_refs[j * per_block + r][0, 0], axis=0, keepdims=True)
                tot = part if tot is None else tot + part
            kmean_ref[pl.ds(s * (pps // per_block) + j, 1), :] = tot * (1.0 / MOBA_BLOCK)

    @pl.when(s == n_steps)
    def _():
        prod = kmean_ref[...] * q_ref[0, 0:1, :]
        seg = (lax.broadcasted_iota(jnp.int32, (N_HEADS * HEAD_DIM, LANES), 0) // HEAD_DIM
               == lax.broadcasted_iota(jnp.int32, (N_HEADS * HEAD_DIM, LANES), 1)).astype(BF16)
        hi, lo = _split_bf16(prod)
        lo2 = (prod - hi.astype(F32) - lo.astype(F32)).astype(BF16)
        gate = _dot(hi, seg) + _dot(lo, seg) + _dot(lo2, seg)
        j = _row_iota(nbp)
        gate = jnp.where(j < n_blocks, gate, NEG_INF)
        rank = _rank_rows(gate, rank_ref, n_blocks)
        rows = []
        for r in range(n_sel):
            hit = (rank == r) & (gate > NEG_INF)
            rows.append(jnp.sum(jnp.where(hit, j, 0), axis=0, keepdims=True))
        for r in range(n_sel):
            hit = (rank == r) & (gate > NEG_INF)
            rows.append(jnp.sum(jnp.where(hit, 1, 0), axis=0, keepdims=True))
        rows.append(jnp.zeros((SUBLANES - 2 * n_sel, LANES), jnp.int32))
        o_ref[0] = jnp.concatenate(rows, axis=0)


def _moba_pick_call(zq, cache, layer, page_table, *, pps=PAGES_PER_STEP):
    b, tq, _ = zq.shape
    n_pages = page_table.shape[1]
    page = cache.shape[2]
    n_blocks = n_pages * page // MOBA_BLOCK
    n_sel = min(MOBA_TOPK, n_blocks + 1)
    assert MOBA_BLOCK % page == 0 and n_pages % pps == 0 and pps % (MOBA_BLOCK // page) == 0
    assert 2 * n_sel <= SUBLANES
    nbp = -(-n_blocks // LANES) * LANES
    block = (1, 1, page, N_HEADS * HEAD_DIM)

    def spec(r):
        return pl.BlockSpec(block, lambda bi, s, pt: (layer, pt[bi, jnp.minimum(s * pps + r, n_pages - 1)], 0, 0))

    grid_spec = pltpu.PrefetchScalarGridSpec(
        num_scalar_prefetch=1,
        grid=(b, n_pages // pps + 1),
        in_specs=[pl.BlockSpec((1, tq, 2 * LANES), lambda bi, s, pt: (bi, 0, _NEW["q_c"] // 256))]
        + [spec(r) for r in range(pps)],
        out_specs=pl.BlockSpec((1, SUBLANES, LANES), lambda bi, s, pt: (bi, 0, 0)),
        scratch_shapes=[pltpu.VMEM((nbp, N_HEADS * HEAD_DIM), F32), pltpu.VMEM((nbp, LANES), F32)])
    kern = functools.partial(_moba_pick_kernel, n_blocks=n_blocks, n_sel=n_sel, n_pages=n_pages, page=page, pps=pps)
    return pl.pallas_call(
        kern, grid_spec=grid_spec,
        out_shape=jax.ShapeDtypeStruct((b, SUBLANES, LANES), jnp.int32),
        compiler_params=_params(("parallel", "arbitrary")),
        name="moba_pick",
    )(page_table, zq, *([cache] * pps)), n_sel


def _moba_gather_kernel(pid_ref, ok_ref, q_ref, *rest, tq, n_sel, per_block, page):
    n_pg = N_HEADS * per_block
    k_refs, v_refs = rest[:n_pg], rest[n_pg:2 * n_pg]
    new_ref, o_ref, m_ref, l_ref, acc_ref = rest[2 * n_pg:]
    bi, r = pl.program_id(0), pl.program_id(1)

    @pl.when(r == 0)
    def _():
        m_ref[...] = jnp.full(m_ref.shape, NEG_INF, F32)
        l_ref[...] = jnp.zeros(l_ref.shape, F32)
        acc_ref[...] = jnp.zeros(acc_ref.shape, F32)

    qs = (jnp.concatenate([_pair_heads(q_ref[0, :, c * LANES:(c + 1) * LANES]) for c in range(2)], axis=0)
          * SCALE).astype(BF16)
    for h in range(N_HEADS):
        rows = pl.ds(h * tq, tq)
        kb = jnp.concatenate([k_refs[h * per_block + j][0, 0] for j in range(per_block)], axis=0).astype(BF16)
        vb = jnp.concatenate([v_refs[h * per_block + j][0, 0] for j in range(per_block)], axis=0).astype(BF16)
        ok = ok_ref[bi, r * N_HEADS + h] > 0
        _row_softmax_step(rows, _dot_nt(qs[h * tq:(h + 1) * tq], kb), ok, vb, m_ref, l_ref, acc_ref)

    @pl.when(r == n_sel - 1)
    def _():
        ki = lax.broadcasted_iota(jnp.int32, (tq, page), 1)
        qi = lax.broadcasted_iota(jnp.int32, (tq, page), 0)
        n_k = new_ref.shape[2] // 2
        outs = []
        for h in range(N_HEADS):
            rows = pl.ds(h * tq, tq)
            c = h // 2
            kb = new_ref[0, :, c * LANES:(c + 1) * LANES].astype(BF16)
            vb = new_ref[0, :, n_k + c * LANES:n_k + (c + 1) * LANES].astype(BF16)
            _row_softmax_step(rows, _dot_nt(qs[h * tq:(h + 1) * tq], kb), ki <= qi, vb, m_ref, l_ref, acc_ref)
            outs.append(acc_ref[rows, :] / jnp.maximum(l_ref[rows, :], 1e-30))
        lane = _lane_iota()
        o_ref[0] = jnp.concatenate([jnp.where(lane < HEAD_DIM, outs[0], outs[1]),
                                    jnp.where(lane < HEAD_DIM, outs[2], outs[3])], axis=1)


def _moba_decode_call(zq, cache, layer, page_table, new_page):
    b, tq, _ = zq.shape
    page = cache.shape[2]
    per_block = MOBA_BLOCK // page
    picks, n_sel = _moba_pick_call(zq, cache, layer, page_table)
    blk = picks[:, :n_sel, :N_HEADS]
    ok = picks[:, n_sel:2 * n_sel, :N_HEADS].reshape(b, n_sel * N_HEADS)
    logical = blk[..., None] * per_block + jnp.arange(per_block, dtype=jnp.int32)
    pid = jnp.take_along_axis(page_table, logical.reshape(b, -1), axis=1)
    block = (1, 1, page, LANES)

    def spec(h, j, field):
        return pl.BlockSpec(block, lambda bi, r, pid_ref, ok_ref:
                            (layer, pid_ref[bi, (r * N_HEADS + h) * per_block + j], 0, 2 * field + h // 2))

    hj = [(h, j) for h in range(N_HEADS) for j in range(per_block)]
    grid_spec = pltpu.PrefetchScalarGridSpec(
        num_scalar_prefetch=2,
        grid=(b, n_sel),
        in_specs=[pl.BlockSpec((1, tq, 2 * LANES), lambda bi, r, pid_ref, ok_ref: (bi, 0, _NEW["q_c"] // 256))]
        + [spec(h, j, 0) for h, j in hj] + [spec(h, j, 1) for h, j in hj]
        + [pl.BlockSpec((1,) + new_page.shape[1:], lambda bi, r, pid_ref, ok_ref: (bi, 0, 0))],
        out_specs=pl.BlockSpec((1, tq, 2 * LANES), lambda bi, r, pid_ref, ok_ref: (bi, 0, 0)),
        scratch_shapes=_row_softmax_scratch(N_HEADS * tq, LANES))
    kern = functools.partial(_moba_gather_kernel, tq=tq, n_sel=n_sel, per_block=per_block, page=page)
    return pl.pallas_call(
        kern, grid_spec=grid_spec,
        out_shape=jax.ShapeDtypeStruct((b, tq, 2 * LANES), F32),
        compiler_params=_params(("parallel", "arbitrary")),
        name="moba_decode",
    )(pid, ok, zq, *([cache] * (2 * len(hj))), new_page)


def _memory_kv(mem, g_ln, w_kv, g_k):
    b, m_rows, d = mem.shape
    n = w_kv.shape[1]
    half = n // 2
    mask = np.zeros((n,), np.float32)
    mask[:half] = 1
    zero = jnp.zeros((n,), F32)
    gain = jnp.concatenate([jnp.tile(g_k.astype(F32), half // HEAD_DIM), jnp.ones((half,), F32)])
    cfg = jnp.stack([jnp.asarray(mask), gain, zero, zero, zero, zero, zero, zero])
    masks = dict(norm=mask, rope=np.zeros_like(mask), sig=np.zeros_like(mask))
    rope = jnp.zeros((m_rows, 3 * LANES), F32)
    out = _project(mem.reshape(b * m_rows, d), g_ln.reshape(1, d), w_kv.astype(BF16), cfg, rope, masks,
                   tm=m_rows, tn=n)
    return out.reshape(b, m_rows, n)


def _mem_kernel(q_ref, kv_ref, o_ref, *, tq):
    lane = _lane_iota()
    n_kv = kv_ref.shape[2] // 2
    chunks = []
    for c in range(2):
        qst = _pair_heads(q_ref[0, :, c * LANES:(c + 1) * LANES])
        kb = kv_ref[0, :, c * LANES:(c + 1) * LANES].astype(BF16)
        vb = kv_ref[0, :, n_kv + c * LANES:n_kv + (c + 1) * LANES].astype(BF16)
        s = _dot_nt((qst * SCALE).astype(BF16), kb)
        e = jnp.exp(s - jnp.max(s, axis=-1, keepdims=True))
        p = e / jnp.sum(e, axis=-1, keepdims=True)
        o = _dot(p.astype(BF16), vb)
        chunks.append(jnp.where(lane < HEAD_DIM, o[0:tq], o[tq:2 * tq]))
    o_ref[0] = jnp.concatenate(chunks, axis=1)


def _mem_call(zq, mkv, *, tq):
    b, t_q, _ = zq.shape
    tq = min(tq, t_q)
    return pl.pallas_call(
        functools.partial(_mem_kernel, tq=tq),
        grid=(b, t_q // tq),
        in_specs=[pl.BlockSpec((1, tq, 2 * LANES), lambda bi, i: (bi, i, _NEW["q_m"] // 256)),
                  pl.BlockSpec((1,) + mkv.shape[1:], lambda bi, i: (bi, 0, 0))],
        out_specs=pl.BlockSpec((1, tq, 2 * LANES), lambda bi, i: (bi, i, 0)),
        out_shape=jax.ShapeDtypeStruct((b, t_q, 2 * LANES), F32),
        compiler_params=_params(("parallel", "parallel")),
        name="mem_attention",
    )(zq, mkv)


def _combine_kernel(x_ref, oa_ref, ob_ref, oc_ref, om_ref, gate_ref, wb_ref, wo_ref, y_ref):
    d = x_ref.shape[-1]
    h = None
    for bi, o_ref in enumerate((oa_ref, ob_ref, oc_ref, om_ref)):
        t = gate_ref[:, bi * d:(bi + 1) * d] * _dot(o_ref[...].astype(BF16), wb_ref[bi])
        h = t if h is None else h + t
    y_ref[...] = x_ref[...] + _dot(h.astype(BF16), wo_ref[...])


def _combine(x2d, outs, gate, w_branch, w_out, *, tm):
    m, d = x2d.shape
    tm = min(tm, m)
    bw = outs[0].shape[-1]
    o_spec = pl.BlockSpec((tm, bw), lambda i: (i, 0))
    return pl.pallas_call(
        _combine_kernel,
        grid=(m // tm,),
        in_specs=[pl.BlockSpec((tm, d), lambda i: (i, 0)), o_spec, o_spec, o_spec, o_spec,
                  pl.BlockSpec((tm, N_BRANCH * d), lambda i: (i, 0)),
                  pl.BlockSpec((N_BRANCH, bw, d), lambda i: (0, 0, 0)),
                  pl.BlockSpec((d, d), lambda i: (0, 0))],
        out_specs=pl.BlockSpec((tm, d), lambda i: (i, 0)),
        out_shape=jax.ShapeDtypeStruct((m, d), F32),
        compiler_params=_params(("parallel",)),
        name="branch_mix",
    )(x2d, *outs, gate, w_branch, w_out)


FF_CHUNK = 256
HALO = 16


def _rms(x, g):
    return x * lax.rsqrt(jnp.mean(x * x, axis=-1, keepdims=True) + RMS_EPS) * g


def _conv3(cw, u2, u1, u0):
    return cw[3:4] + cw[0:1] * u2 + cw[1:2] * u1 + cw[2:3] * u0


def _ffn_kernel(x_ref, xh_ref, ha_ref, hb_ref, g_ref, wa_ref, wb_ref, cwa_ref, cwb_ref, wdn_ref,
                y_ref, sta_ref, stb_ref, xn_ref, xhn_ref, acc_ref, *, tm):
    i = pl.program_id(1)
    j = pl.program_id(2)

    @pl.when(j == 0)
    def _():
        xn_ref[...] = _rms(x_ref[0], g_ref[...]).astype(BF16)
        xhn_ref[...] = _rms(xh_ref[0], g_ref[...]).astype(BF16)
        acc_ref[...] = jnp.zeros(acc_ref.shape, F32)

    def half(w_ref, hist_ref, cw_ref, st_ref):
        u = _dot(xn_ref[...], w_ref[...])
        u_prev = _dot(xhn_ref[...], w_ref[...])[HALO - SUBLANES:HALO]
        prev = jnp.where(i == 0, hist_ref[0], u_prev)
        ext = jnp.concatenate([prev, u], axis=0)
        st_ref[0, 0] = ext[tm:tm + SUBLANES]
        return _conv3(cw_ref[...], pltpu.roll(ext, 2, 0)[SUBLANES:], pltpu.roll(ext, 1, 0)[SUBLANES:], u)

    a = half(wa_ref, ha_ref, cwa_ref, sta_ref)
    b = half(wb_ref, hb_ref, cwb_ref, stb_ref)
    acc_ref[...] += _dot((a * jax.nn.sigmoid(a) * b).astype(BF16), wdn_ref[...])

    @pl.when(j == pl.num_programs(2) - 1)
    def _():
        y_ref[0] = x_ref[0] + acc_ref[...]


def _conv_table(conv_w, conv_b):
    return jnp.concatenate([conv_w, conv_b[None, :], jnp.zeros((SUBLANES - CONV_WIDTH - 1, conv_b.shape[0]), F32)])


def _conv_ffn(x, hist, g, w_up, conv_w, conv_b, w_down, *, tm):
    b, t, d = x.shape
    d_ff = w_down.shape[0]
    n_j = d_ff // FF_CHUNK
    tm = min(tm, t)
    cw = _conv_table(conv_w, conv_b)
    hist8 = jnp.concatenate([jnp.zeros((b, SUBLANES - 2, 2 * d_ff), F32), hist], axis=1)
    a_col = lambda bi, i, j: (0, j)
    b_col = lambda bi, i, j: (0, n_j + j)
    st_spec = pl.BlockSpec((1, 1, SUBLANES, FF_CHUNK), lambda bi, i, j: (bi, i, 0, j))
    st_shape = jax.ShapeDtypeStruct((b, t // tm, SUBLANES, d_ff), F32)
    y, st_a, st_b = pl.pallas_call(
        functools.partial(_ffn_kernel, tm=tm),
        grid=(b, t // tm, n_j),
        in_specs=[pl.BlockSpec((1, tm, d), lambda bi, i, j: (bi, i, 0)),
                  pl.BlockSpec((1, HALO, d), lambda bi, i, j: (bi, jnp.maximum(i * (tm // HALO) - 1, 0), 0)),
                  pl.BlockSpec((1, SUBLANES, FF_CHUNK), lambda bi, i, j: (bi, 0, j)),
                  pl.BlockSpec((1, SUBLANES, FF_CHUNK), lambda bi, i, j: (bi, 0, n_j + j)),
                  pl.BlockSpec((1, d), lambda bi, i, j: (0, 0)),
                  pl.BlockSpec((d, FF_CHUNK), a_col), pl.BlockSpec((d, FF_CHUNK), b_col),
                  pl.BlockSpec((SUBLANES, FF_CHUNK), a_col), pl.BlockSpec((SUBLANES, FF_CHUNK), b_col),
                  pl.BlockSpec((FF_CHUNK, d), lambda bi, i, j: (j, 0))],
        out_specs=[pl.BlockSpec((1, tm, d), lambda bi, i, j: (bi, i, 0)), st_spec, st_spec],
        out_shape=[jax.ShapeDtypeStruct((b, t, d), F32), st_shape, st_shape],
        scratch_shapes=[pltpu.VMEM((tm, d), BF16), pltpu.VMEM((HALO, d), BF16), pltpu.VMEM((tm, d), F32)],
        compiler_params=_params(("parallel", "arbitrary", "arbitrary")),
        name="conv_ffn",
    )(x, x, hist8, hist8, g.reshape(1, d), w_up, w_up, cw, cw, w_down)
    return y, jnp.concatenate([st_a[:, -1, SUBLANES - 2:], st_b[:, -1, SUBLANES - 2:]], axis=-1)


def _ffn_row_kernel(x_ref, h0a_ref, h0b_ref, h1a_ref, h1b_ref, g_ref, wa_ref, wb_ref, cwa_ref, cwb_ref, wdn_ref,
                    y_ref, ua_ref, ub_ref, xn_ref, acc_ref):
    j = pl.program_id(0)

    @pl.when(j == 0)
    def _():
        xn_ref[...] = _rms(x_ref[...], g_ref[...]).astype(BF16)
        acc_ref[...] = jnp.zeros(acc_ref.shape, F32)

    ua = _dot(xn_ref[...], wa_ref[...])
    ub = _dot(xn_ref[...], wb_ref[...])
    ua_ref[...] = ua
    ub_ref[...] = ub
    a = _conv3(cwa_ref[...], h0a_ref[...], h1a_ref[...], ua)
    b = _conv3(cwb_ref[...], h0b_ref[...], h1b_ref[...], ub)
    acc_ref[...] += _dot((a * jax.nn.sigmoid(a) * b).astype(BF16), wdn_ref[...])

    @pl.when(j == pl.num_programs(0) - 1)
    def _():
        y_ref[...] = x_ref[...] + acc_ref[...]


def _conv_ffn_rows(x2d, hist, g, w_up, conv_w, conv_b, w_down):
    b, d = x2d.shape
    d_ff = w_down.shape[0]
    n_j = d_ff // FF_CHUNK
    cw = _conv_table(conv_w, conv_b)
    h0, h1 = hist[:, 0], hist[:, 1]
    a_col = lambda j: (0, j)
    b_col = lambda j: (0, n_j + j)
    row_a, row_b = pl.BlockSpec((b, FF_CHUNK), a_col), pl.BlockSpec((b, FF_CHUNK), b_col)
    y, ua, ub = pl.pallas_call(
        _ffn_row_kernel,
        grid=(n_j,),
        in_specs=[pl.BlockSpec((b, d), lambda j: (0, 0)), row_a, row_b, row_a, row_b,
                  pl.BlockSpec((1, d), lambda j: (0, 0)),
                  pl.BlockSpec((d, FF_CHUNK), a_col), pl.BlockSpec((d, FF_CHUNK), b_col),
                  pl.BlockSpec((SUBLANES, FF_CHUNK), a_col), pl.BlockSpec((SUBLANES, FF_CHUNK), b_col),
                  pl.BlockSpec((FF_CHUNK, d), lambda j: (j, 0))],
        out_specs=[pl.BlockSpec((b, d), lambda j: (0, 0)), row_a, row_a],
        out_shape=[jax.ShapeDtypeStruct((b, d), F32), jax.ShapeDtypeStruct((b, d_ff), F32),
                   jax.ShapeDtypeStruct((b, d_ff), F32)],
        scratch_shapes=[pltpu.VMEM((b, d), BF16), pltpu.VMEM((b, d), F32)],
        compiler_params=_params(("arbitrary",)),
        name="conv_ffn_rows",
    )(x2d, h0, h0, h1, h1, g.reshape(1, d), w_up, w_up, cw, cw, w_down)
    return y, jnp.stack([h1, jnp.concatenate([ua, ub], axis=-1)], axis=1)


def _cols(zh, name, width):
    return zh[..., _NEW[name]:_NEW[name] + width]


def _new_rows(zh):
    b, t, _ = zh.shape
    dsa = jnp.concatenate([_cols(zh, "k_a", 2 * HEAD_DIM), _cols(zh, "ik", HEAD_DIM)], axis=-1)
    nsa = _cols(zh, "kc", 4 * HEAD_DIM)
    moba = _cols(zh, "k_c", 2 * N_HEADS * HEAD_DIM)
    win = _cols(zh, "kw", 2 * HEAD_DIM)
    return (dsa.reshape(b, t, 3, HEAD_DIM), nsa.reshape(b, t, 4, HEAD_DIM),
            moba.reshape(b, t, 2, N_HEADS, HEAD_DIM), win.reshape(b, t, 2, HEAD_DIM))


def _prompt_layer(x, mem, p):
    b, t, d = x.shape
    zh2d, gate = _projections(x.reshape(b * t, d), jnp.arange(t, dtype=jnp.int32), p["ln"][0], p["w_heads"],
                              p["cfg"], p["w_gate"], tm=256)
    zh = zh2d.reshape(b, t, N_HEADCOLS)
    o_a = _dsa_call(zh, zh, zh, kv_blk=BLK_KV_A, ik_blk=BLK_IK, tq=128, n_keys=t)
    cmp = _nsa_compress(zh, p["cmp_w"], blk=BLK_CMP)
    o_b = _nsa_call(zh, cmp, zh, zh, sel_blk=BLK_SEL, win_blk=BLK_WIN, tq=128, n_keys=t)
    o_c = _moba_call(zh, zh, tq=MOBA_BLOCK)
    mkv = _memory_kv(mem, p["ln"][2], p["w_mem_kv"], p["g_mem"][1])
    o_m = _mem_call(zh, mkv, tq=256)
    outs = [o.reshape(b * t, o.shape[-1]) for o in (o_a, o_b, o_c, o_m)]
    x1 = _combine(x.reshape(b * t, d), outs, gate, p["w_branch"], p["w_out"], tm=512).reshape(b, t, d)
    hist = jnp.zeros((b, CONV_WIDTH - 1, p["w_up"].shape[1]), F32)
    y, conv = _conv_ffn(x1, hist, p["ln"][1], p["w_up"], p["conv_w"], p["conv_b"], p["w_down"], tm=1024)
    dsa, nsa, moba, win = _new_rows(zh)
    keep = min(NSA_WINDOW, t)
    return y, dsa, nsa, moba, win[:, t - keep:], mkv.reshape(b, mkv.shape[1], 2, N_HEADS, HEAD_DIM), conv


def _first_row(x, n):
    return jnp.pad(x, ((0, 0), (0, n - 1)) + ((0, 0),) * (x.ndim - 2))


def _sample_layer(x, layer, caches, page_table, win_state, mem_kv, conv_hist, p):
    b, _, d = x.shape
    cache_dsa, cache_nsa, cache_moba = caches
    page = cache_dsa.shape[2]
    past = page_table.shape[1] * page
    x2d = x.reshape(b, d)
    zh, gate = _projections(x2d, jnp.full((b,), past, jnp.int32), p["ln"][0], p["w_heads"], p["cfg"],
                            p["w_gate"], tm=b)
    zq = _first_row(zh[:, None, :], DEC_ROWS)
    dsa, nsa, moba, win = _new_rows(zh[:, None, :])
    o_a = _dsa_decode_call(zq, cache_dsa, layer, page_table, _first_row(dsa.reshape(b, 1, -1), DEC_KEY_CHUNK))
    win_all = jnp.concatenate([win_state.reshape(b, -1, 2 * HEAD_DIM), win.reshape(b, 1, 2 * HEAD_DIM)], axis=1)
    w_pad = -(-win_all.shape[1] // LANES) * LANES
    win_rows = jnp.pad(win_all, ((0, 0), (0, w_pad - win_all.shape[1]), (0, 0)))
    o_b = _nsa_decode_call(zq, cache_nsa, layer, page_table, _first_row(nsa.reshape(b, 1, -1), DEC_KEY_CHUNK),
                           win_rows, p["cmp_w"])
    o_c = _moba_decode_call(zq, cache_moba, layer, page_table, _first_row(moba.reshape(b, 1, -1), page))
    o_m = _mem_call(zq, mem_kv.reshape(b, mem_kv.shape[1], -1), tq=DEC_ROWS)
    outs = [o[:, 0, :] for o in (o_a, o_b, o_c, o_m)]
    x1 = _combine(x2d, outs, gate, p["w_branch"], p["w_out"], tm=b)
    y, conv = _conv_ffn_rows(x1, conv_hist, p["ln"][1], p["w_up"], p["conv_w"], p["conv_b"], p["w_down"])
    keep = win_state.shape[1]
    win_new = win_all[:, win_all.shape[1] - keep:].reshape(b, keep, 2, HEAD_DIM)
    return y.reshape(b, 1, d), dsa, nsa, moba, win_new, conv


def kernel(x_prompt, x_sample, cache_dsa, cache_nsa, cache_moba, state_nsa_win, cache_mem, state_ffn_conv,
           page_table, mem_prompt, ln, w_in, g_dsa, g_nsa, g_moba, g_mem, w_mem_kv, w_cmp1, w_cmp2, pe_cmp,
           w_branch, w_out, w_up, conv_w, conv_b, w_down):
    depth = ln.shape[0]
    caches = tuple(c.reshape(*c.shape[:3], -1) for c in (cache_dsa, cache_nsa, cache_moba))
    xp, xs = x_prompt, x_sample
    outs_p = [[] for _ in range(6)]
    outs_s = [[] for _ in range(5)]
    for l in range(depth):
        w_heads, cfg = _head_weights(w_in[l], g_dsa[l], g_nsa[l], g_moba[l], g_mem[l])
        p = dict(ln=ln[l], w_heads=w_heads, cfg=cfg, w_gate=w_in[l][:, GATE_ORIG:].astype(BF16), g_mem=g_mem[l],
                 w_mem_kv=w_mem_kv[l], cmp_w=_compress_weights(w_cmp1[l], w_cmp2[l], pe_cmp[l]),
                 w_branch=w_branch[l].astype(BF16), w_out=w_out[l].astype(BF16), w_up=w_up[l].astype(BF16),
                 conv_w=conv_w[l], conv_b=conv_b[l], w_down=w_down[l].astype(BF16))
        xp, *rest = _prompt_layer(xp, mem_prompt, p)
        for acc, r in zip(outs_p, rest):
            acc.append(r)
        xs, *rest = _sample_layer(xs, l, caches, page_table, state_nsa_win[l], cache_mem[l], state_ffn_conv[l], p)
        for acc, r in zip(outs_s, rest):
            acc.append(r)
    dsa_p, nsa_p, moba_p, win_p, memkv_p, conv_p = [jnp.stack(a) for a in outs_p]
    dsa_s, nsa_s, moba_s, win_s, conv_s = [jnp.stack(a) for a in outs_s]
    return (xp, xs, dsa_p, dsa_s, nsa_p, nsa_s, moba_p, moba_s, win_p, win_s, memkv_p, conv_p, conv_s)
```

```python
import functools
import math

import numpy as np
import jax
import jax.numpy as jnp
from jax import lax
from jax.experimental import pallas as pl
from jax.experimental.pallas import tpu as pltpu

HEAD_DIM = 64
ROPE_DIM = HEAD_DIM // 4
ROPE_THETA = 500000.0
N_HEADS = 4
DSA_TOPK = 256
NSA_CMP_LEN = 32
NSA_CMP_STRIDE = 16
NSA_SEL_BLOCK = 64
NSA_SEL_TOPN = 16
NSA_WINDOW = 512
MOBA_BLOCK = 256
MOBA_TOPK = 3
N_BRANCH = 4
CONV_WIDTH = 3
RMS_EPS = 1e-6

LANES = 128
SUBLANES = 8
VMEM_LIMIT = 56 * 1024 * 1024
DEC_ROWS = SUBLANES
KEY_CHUNK = 512
DEC_KEY_CHUNK = 2048
PAGES_PER_STEP = 8

F32 = jnp.float32
BF16 = jnp.bfloat16
NEG_INF = float("-inf")
SCALE = HEAD_DIM ** -0.5

KEY_NEG_INF = int(np.uint32(0xFF800000) ^ np.uint32(0x7FFFFFFF)) - 2 ** 32
KEY_POS_INF = 0x7F800000
INT_MIN = -2 ** 31
STRIP = 32
GROUP = 256

_NT = (((1,), (1,)), ((), ()))

_ORIG = dict(q_a=0, k_a=256, v_a=320, iq=384, ik=640, iw=704, q_b=708, kc=964, vc=1028, ks=1092, vs=1156,
             kw=1220, vw=1284, g_b=1348, q_c=1360, k_c=1616, v_c=1872, q_m=2128)
GATE_ORIG = 2384
_NEW = dict(q_a=0, iq=256, q_b=512, q_c=768, q_m=1024, k_c=1280, v_c=1536, k_a=1792, v_a=1856, ik=1920,
            iw=1984, kc=2048, vc=2112, ks=2176, vs=2240, kw=2304, vw=2368, g_b=2432)
_WIDTH = dict(q_a=256, iq=256, q_b=256, q_c=256, q_m=256, k_c=256, v_c=256, k_a=64, v_a=64, ik=64, iw=4,
              kc=64, vc=64, ks=64, vs=64, kw=64, vw=64, g_b=12)
N_HEADCOLS = 2560
_NORMED = ("q_a", "q_b", "q_c", "q_m", "k_c", "k_a", "kc", "ks", "kw")
_ROPED = ("q_a", "iq", "q_b", "q_c", "k_c", "k_a", "ik", "kc", "ks", "kw")
_SIGMOID = ("g_b",)

BLK_KV_A, BLK_IK, BLK_CMP, BLK_SEL, BLK_WIN, BLK_GB = 14, 15, 16, 17, 18, 19
BLK_KC, BLK_VC = 10, 12


def _head_layout():
    masks = {k: np.zeros((N_HEADCOLS,), np.float32) for k in ("norm", "rope", "sig")}
    for name, new in _NEW.items():
        w = _WIDTH[name]
        if name in _NORMED:
            masks["norm"][new:new + w] = 1
        if name in _ROPED:
            masks["rope"][new:new + w] = 1
        if name in _SIGMOID:
            masks["sig"][new:new + w] = 1
    return masks


_COL_MASKS = _head_layout()


def _chunk_any(mask):
    return tuple(bool(mask[c * LANES:(c + 1) * LANES].any()) for c in range(mask.shape[0] // LANES))


def _params(sem):
    return pltpu.CompilerParams(dimension_semantics=sem, vmem_limit_bytes=VMEM_LIMIT)


def _lane_iota(n=LANES):
    return lax.broadcasted_iota(jnp.int32, (1, n), 1)


def _row_iota(n):
    return lax.broadcasted_iota(jnp.int32, (n, 1), 0)


def _split_bf16(x):
    hi = x.astype(BF16)
    lo = (x - hi.astype(F32)).astype(BF16)
    return hi, lo


def _dot(a, b):
    return jnp.dot(a, b, preferred_element_type=F32)


def _dot_nt(a, b):
    return lax.dot_general(a, b, _NT, preferred_element_type=F32)


def _proj_kernel(x_ref, g_ref, w_ref, cfg_ref, rope_ref, o_ref, *, norm_chunks, rope_chunks, sig_chunks):
    x = x_ref[...]
    xn = x * lax.rsqrt(jnp.mean(x * x, axis=-1, keepdims=True) + RMS_EPS) * g_ref[...]
    z = _dot(xn.astype(BF16), w_ref[...])
    rr = lax.broadcasted_iota(jnp.int32, (LANES, LANES), 0) // HEAD_DIM
    cc = lax.broadcasted_iota(jnp.int32, (LANES, LANES), 1) // HEAD_DIM
    seg = (rr == cc).astype(BF16)
    for c in range(z.shape[1] // LANES):
        sl = slice(c * LANES, (c + 1) * LANES)
        zc = z[:, sl]
        if norm_chunks[c]:
            hi, lo = _split_bf16(zc * zc)
            ss = _dot(hi, seg) + _dot(lo, seg)
            r = lax.rsqrt(ss * (1.0 / HEAD_DIM) + RMS_EPS)
            zc = jnp.where(cfg_ref[0:1, sl] > 0, zc * r * cfg_ref[1:2, sl], zc)
        if rope_chunks[c]:
            rot = (zc * rope_ref[:, 0:LANES]
                   + pltpu.roll(zc, LANES - ROPE_DIM // 2, 1) * rope_ref[:, LANES:2 * LANES]
                   + pltpu.roll(zc, ROPE_DIM // 2, 1) * rope_ref[:, 2 * LANES:3 * LANES])
            zc = jnp.where(cfg_ref[2:3, sl] > 0, rot, zc)
        if sig_chunks[c]:
            zc = jnp.where(cfg_ref[3:4, sl] > 0, jax.nn.sigmoid(zc), zc)
        o_ref[:, sl] = zc


def _project(x2d, g, w_bf16, cfg, rope, masks, *, tm, tn):
    m, d = x2d.shape
    n = w_bf16.shape[1]
    tn = min(tn, n)
    tm = min(tm, m)
    flags = {k: _chunk_any(v) for k, v in masks.items()}
    per_tile = tn // LANES
    for k, v in flags.items():
        assert all(v[t * per_tile:(t + 1) * per_tile] == v[:per_tile] for t in range(n // tn)), k
    rope_tiles = rope.shape[0] // tm
    kern = functools.partial(_proj_kernel, norm_chunks=flags["norm"][:per_tile],
                             rope_chunks=flags["rope"][:per_tile], sig_chunks=flags["sig"][:per_tile])
    return pl.pallas_call(
        kern,
        grid=(m // tm, n // tn),
        in_specs=[pl.BlockSpec((tm, d), lambda i, j: (i, 0)),
                  pl.BlockSpec((1, d), lambda i, j: (0, 0)),
                  pl.BlockSpec((d, tn), lambda i, j: (0, j)),
                  pl.BlockSpec((SUBLANES, tn), lambda i, j: (0, j)),
                  pl.BlockSpec((tm, 3 * LANES), lambda i, j: (i % rope_tiles, 0))],
        out_specs=pl.BlockSpec((tm, tn), lambda i, j: (i, j)),
        out_shape=jax.ShapeDtypeStruct((m, n), F32),
        compiler_params=_params(("parallel", "arbitrary")),
        name="project",
    )(x2d, g, w_bf16, cfg, rope)


def _rope_table(pos):
    half = ROPE_DIM // 2
    inv_freq = ROPE_THETA ** (-jnp.arange(half, dtype=F32) / half)
    ang = pos.astype(F32)[:, None] * inv_freq[None, :]
    cos, sin = jnp.cos(ang), jnp.sin(ang)
    t = pos.shape[0]
    ones = jnp.ones((t, HEAD_DIM - ROPE_DIM), F32)
    zeros = jnp.zeros((t, HEAD_DIM - ROPE_DIM), F32)
    zh = jnp.zeros((t, half), F32)
    c64 = jnp.concatenate([cos, cos, ones], axis=1)
    s1 = jnp.concatenate([-sin, zh, zeros], axis=1)
    s2 = jnp.concatenate([zh, sin, zeros], axis=1)
    return jnp.concatenate([c64, c64, s1, s1, s2, s2], axis=1)


def _head_weights(w_in_l, g_dsa, g_nsa, g_moba, g_mem):
    d = w_in_l.shape[0]
    pieces, at = [], 0
    for name, new in sorted(_NEW.items(), key=lambda kv: kv[1]):
        if new > at:
            pieces.append(jnp.zeros((d, new - at), w_in_l.dtype))
        pieces.append(w_in_l[:, _ORIG[name]:_ORIG[name] + _WIDTH[name]])
        at = new + _WIDTH[name]
    pieces.append(jnp.zeros((d, N_HEADCOLS - at), w_in_l.dtype))
    w = jnp.concatenate(pieces, axis=1).astype(BF16)
    gain = jnp.ones((N_HEADCOLS,), F32)
    for name, gvec in (("q_a", g_dsa[0]), ("k_a", g_dsa[1]), ("q_b", g_nsa[0]), ("kc", g_nsa[1]),
                       ("ks", g_nsa[2]), ("kw", g_nsa[3]), ("q_c", g_moba[0]), ("k_c", g_moba[1]),
                       ("q_m", g_mem[0])):
        reps = _WIDTH[name] // HEAD_DIM
        gain = lax.dynamic_update_slice(gain, jnp.tile(gvec.astype(F32), reps), (_NEW[name],))
    zero = jnp.zeros((N_HEADCOLS,), F32)
    cfg = jnp.stack([jnp.asarray(_COL_MASKS["norm"]), gain, jnp.asarray(_COL_MASKS["rope"]),
                     jnp.asarray(_COL_MASKS["sig"]), zero, zero, zero, zero])
    return w, cfg


def _gate_cfg(n):
    z = jnp.zeros((n,), F32)
    o = jnp.ones((n,), F32)
    return jnp.stack([z, o, z, o, z, z, z, z])


def _projections(x2d, pos, ln0, w_heads, cfg, w_gate, *, tm):
    d = x2d.shape[1]
    rope = _rope_table(pos)
    g = ln0.reshape(1, d)
    zh = _project(x2d, g, w_heads, cfg, rope, _COL_MASKS, tm=tm, tn=N_HEADCOLS)
    n_g = w_gate.shape[1]
    gmask = dict(norm=np.zeros((n_g,), np.float32), rope=np.zeros((n_g,), np.float32),
                 sig=np.ones((n_g,), np.float32))
    gate = _project(x2d, g, w_gate, _gate_cfg(n_g), rope, gmask, tm=tm, tn=1024)
    return zh, gate


def _stack_heads(x256):
    lane = _lane_iota()
    parts = []
    for c in range(2):
        ch = x256[:, c * LANES:(c + 1) * LANES]
        parts.append(jnp.where(lane < HEAD_DIM, ch, 0.0))
        parts.append(jnp.where(lane < HEAD_DIM, pltpu.roll(ch, HEAD_DIM, 1), 0.0))
    return jnp.concatenate(parts, axis=0)


def _unstack_heads_t(o_t, tq):
    o = o_t.T
    lane = _lane_iota()
    chunks = []
    for c in range(2):
        even, odd = o[2 * c * tq:(2 * c + 1) * tq], o[(2 * c + 1) * tq:(2 * c + 2) * tq]
        chunks.append(jnp.where(lane < HEAD_DIM, pltpu.roll(even, HEAD_DIM, 1), odd))
    return jnp.concatenate(chunks, axis=1)


def _pair_heads(q128):
    lane = _lane_iota()
    return jnp.concatenate([jnp.where(lane < HEAD_DIM, q128, 0.0), jnp.where(lane >= HEAD_DIM, q128, 0.0)], axis=0)


def _rows_t(x):
    tq = x.shape[0]
    if tq < LANES:
        x = jnp.concatenate([x, jnp.zeros((LANES - tq, x.shape[1]), x.dtype)], axis=0)
    return x.T[:, 0:tq]


def _tile_lanes(x, n):
    return jnp.concatenate([x] * n, axis=1)


def _tsoftmax_init(m_ref, l_ref, acc_ref):
    m_ref[...] = jnp.full(m_ref.shape, NEG_INF, F32)
    l_ref[...] = jnp.zeros(l_ref.shape, F32)
    acc_ref[...] = jnp.zeros(acc_ref.shape, F32)


def _attend_loop(n_chunks, score_fn, mask_fn, load_vt, s_ref, p_ref, m_ref, l_ref, acc_ref, *, reps):
    kc, r_cols = p_ref.shape
    w = r_cols // reps
    _tsoftmax_init(m_ref, l_ref, acc_ref)

    def scores_into(half, c):
        cc = jnp.minimum(c, n_chunks - 1)
        for g0 in range(0, kc, GROUP):
            s_ref[half * kc + g0:half * kc + g0 + GROUP, :] = score_fn(cc, g0, GROUP)

    def consume(half, c):
        live = c < n_chunks
        cc = jnp.minimum(c, n_chunks - 1)
        strip_mask = mask_fn(cc)
        mx = jnp.full((STRIP, r_cols), NEG_INF, F32)
        for r0 in range(0, kc, STRIP):
            at = slice(half * kc + r0, half * kc + r0 + STRIP)
            s = s_ref[at, :]
            ok = strip_mask(r0, STRIP) & live
            s = jnp.concatenate([jnp.where(ok, s[:, k * w:(k + 1) * w], NEG_INF) for k in range(reps)], axis=1)
            s_ref[at, :] = s
            mx = jnp.maximum(mx, s)
        m_old = m_ref[...]
        m_new = jnp.maximum(m_old, jnp.max(mx, axis=0, keepdims=True))
        m_safe = jnp.where(m_new == NEG_INF, 0.0, m_new)
        tot = jnp.zeros((STRIP, r_cols), F32)
        for r0 in range(0, kc, STRIP):
            p = jnp.exp(s_ref[half * kc + r0:half * kc + r0 + STRIP, :] - m_safe)
            tot = tot + p
            p_ref[r0:r0 + STRIP, :] = p.astype(BF16)
        alpha = jnp.exp(m_old - m_safe)
        l_ref[...] = alpha * l_ref[...] + jnp.sum(tot, axis=0, keepdims=True)
        acc_ref[...] = alpha * acc_ref[...] + _dot(load_vt(cc), p_ref[...])
        m_ref[...] = m_new

    scores_into(0, 0)

    def body(pair, carry):
        scores_into(1, 2 * pair + 1)
        consume(0, 2 * pair)
        scores_into(0, 2 * pair + 2)
        consume(1, 2 * pair + 1)
        return carry

    lax.fori_loop(0, (n_chunks + 1) // 2, body, 0)


def _tsoftmax_result(l_ref, acc_ref):
    return acc_ref[...] / jnp.maximum(l_ref[...], 1e-30)


def _tmasked_softmax(s_t, valid_t):
    s = jnp.where(valid_t, s_t, NEG_INF)
    m = jnp.max(s, axis=0, keepdims=True)
    m = jnp.where(m == NEG_INF, 0.0, m)
    e = jnp.exp(s - m)
    return e / jnp.maximum(jnp.sum(e, axis=0, keepdims=True), 1e-30)


def _attend_scratch(d, r, kc):
    return [pltpu.VMEM((2 * kc, r), F32), pltpu.VMEM((kc, r), BF16),
            pltpu.VMEM((1, r), F32), pltpu.VMEM((1, r), F32), pltpu.VMEM((d, r), F32)]


def _chunk_loader(ref, kc):
    def load(c, r0=0, n=kc):
        rows = pl.ds(pl.multiple_of(c * kc + r0, math.gcd(kc, n)), n)
        x = ref[0, rows, :] if len(ref.shape) == 3 else ref[rows, :]
        return x.astype(BF16)
    return load


def _fill_transposed(src_ref, dst_ref, kc):
    def body(c, carry):
        dst_ref[c] = src_ref[0, pl.ds(pl.multiple_of(c * kc, kc), kc), :].T.astype(BF16)
        return carry
    lax.fori_loop(0, dst_ref.shape[0], body, 0)


def _rank_rows(val_t, src_ref, n_rows):
    src_ref[...] = val_t
    j = _row_iota(val_t.shape[0])

    def body(i, rank):
        row = src_ref[pl.ds(i, 1), :]
        beats = (row > val_t) | ((row == val_t) & (i < j))
        return rank + jnp.where(beats, 1, 0)

    return lax.fori_loop(0, n_rows, body, jnp.zeros(val_t.shape, jnp.int32))


def _select_top_t(val_t, rank_ref, n_valid, n_sel):
    rank = _rank_rows(val_t, rank_ref, n_valid)
    return jnp.where((rank < n_sel) & (val_t > NEG_INF), 1.0, 0.0)


def _dsa_core(q, iq, iw, load_kv, load_ik, load_kvt, key_ref, s_ref, p_ref, m_ref, l_ref, acc_ref,
              *, tq, kc, n_top, p0, idx_bits):
    n_chunks = (p0 + tq - 1) // kc + 1
    pos = p0 + _lane_iota(tq)
    iqs_t = _stack_heads(iq).T.astype(BF16)
    qs_t = (_stack_heads(q) * SCALE).T.astype(BF16)
    iw_t = _rows_t(iw)
    w_rows = [iw_t[HEAD_DIM + h:HEAD_DIM + h + 1, :] for h in range(N_HEADS)]

    def chunk_pos(c):
        return c * kc + _row_iota(kc)

    def score_chunk(c, carry):
        lg = _dot(load_ik(c), iqs_t)
        sc = w_rows[0] * jnp.maximum(lg[:, 0:tq], 0.0)
        for h in range(1, N_HEADS):
            sc = sc + w_rows[h] * jnp.maximum(lg[:, h * tq:(h + 1) * tq], 0.0)
        sc = jnp.where(chunk_pos(c) <= pos, sc, NEG_INF)
        bits = pltpu.bitcast(sc, jnp.int32)
        key_ref[c] = bits ^ ((bits >> 31) & 0x7FFFFFFF)
        return carry

    lax.fori_loop(0, n_chunks, score_chunk, 0)

    def count(pred):
        def body(c, acc):
            hit = jnp.where(pred(key_ref[c], chunk_pos(c)), 1, 0)
            parts = [hit[r * SUBLANES:(r + 1) * SUBLANES] for r in range(kc // SUBLANES)]
            while len(parts) > 1:
                parts = [a + b for a, b in zip(parts[0::2], parts[1::2])]
            return acc + parts[0]
        acc = lax.fori_loop(0, n_chunks, body, jnp.zeros((SUBLANES, tq), jnp.int32))
        return jnp.sum(acc, axis=0, keepdims=True)

    def thr_bit(b, thr):
        cand = thr + jnp.left_shift(jnp.int32(1), 31 - b)
        cnt = count(lambda key, kpos: key >= cand)
        return jnp.where(cnt >= n_top, cand, thr)

    thr = lax.fori_loop(0, 32, thr_bit, jnp.full((1, tq), INT_MIN, jnp.int32))

    def last_tied():
        need = n_top - count(lambda key, kpos: key > thr)

        def idx_bit(b, last):
            cand = last + jnp.left_shift(jnp.int32(1), idx_bits - 1 - b)
            cnt = count(lambda key, kpos: (key == thr) & (kpos < cand))
            return jnp.where(cnt < need, cand, last)

        return lax.fori_loop(0, idx_bits, idx_bit, jnp.zeros((1, tq), jnp.int32))

    over = (count(lambda key, kpos: key >= thr) > n_top) & (thr > KEY_NEG_INF)
    last = lax.cond(jnp.max(jnp.where(over, 1, 0)) > 0, last_tied,
                    lambda: jnp.full((1, tq), 2 ** idx_bits, jnp.int32))
    last = jnp.where(thr > KEY_NEG_INF, last, -1)

    def selected(c):
        def strip(r0, rows):
            key = key_ref[c, r0:r0 + rows, :]
            kpos = c * kc + r0 + _row_iota(rows)
            return ((key > thr) | ((key == thr) & (kpos <= last))) & (key < KEY_POS_INF)
        return strip

    _attend_loop(n_chunks, lambda c, r0, n: _dot(load_kv(c, r0, n), qs_t), selected, load_kvt,
                 s_ref, p_ref, m_ref, l_ref, acc_ref, reps=N_HEADS)
    return _unstack_heads_t(_tsoftmax_result(l_ref, acc_ref), tq)


def _dsa_kernel(q_ref, iq_ref, iw_ref, kv_ref, ik_ref, o_ref, kvt_ref, key_ref, s_ref, p_ref, m_ref, l_ref, acc_ref,
                *, tq, kc, n_top, idx_bits):
    i = pl.program_id(1)

    @pl.when(i == 0)
    def _():
        _fill_transposed(kv_ref, kvt_ref, kc)

    o_ref[0] = _dsa_core(q_ref[0], iq_ref[0], iw_ref[0], _chunk_loader(kv_ref, kc), _chunk_loader(ik_ref, kc),
                         lambda c: kvt_ref[c], key_ref, s_ref, p_ref, m_ref, l_ref, acc_ref,
                         tq=tq, kc=kc, n_top=n_top, p0=i * tq, idx_bits=idx_bits)


def _row_softmax_step(rows, s, valid, v_bf16, m_ref, l_ref, acc_ref):
    s = jnp.where(valid, s, NEG_INF)
    m_old = m_ref[rows, :]
    m_new = jnp.maximum(m_old, jnp.max(s, axis=-1, keepdims=True))
    m_safe = jnp.where(m_new == NEG_INF, 0.0, m_new)
    p = jnp.exp(s - m_safe)
    alpha = jnp.exp(m_old - m_safe)
    l_ref[rows, :] = alpha * l_ref[rows, :] + jnp.sum(p, axis=-1, keepdims=True)
    acc_ref[rows, :] = alpha * acc_ref[rows, :] + _dot(p.astype(BF16), v_bf16)
    m_ref[rows, :] = m_new


def _row_softmax_init(m_ref, l_ref, acc_ref):
    m_ref[...] = jnp.full(m_ref.shape, NEG_INF, F32)
    l_ref[...] = jnp.zeros(l_ref.shape, F32)
    acc_ref[...] = jnp.zeros(acc_ref.shape, F32)


def _row_softmax_result(rows, l_ref, acc_ref):
    return acc_ref[rows, :] / jnp.maximum(l_ref[rows, :], 1e-30)


def _row_masked_softmax(s, valid):
    s = jnp.where(valid, s, NEG_INF)
    m = jnp.max(s, axis=-1, keepdims=True)
    m = jnp.where(m == NEG_INF, 0.0, m)
    e = jnp.exp(s - m)
    return e / jnp.maximum(jnp.sum(e, axis=-1, keepdims=True), 1e-30)


def _row_softmax_scratch(rows, d):
    return [pltpu.VMEM((rows, 1), F32), pltpu.VMEM((rows, 1), F32), pltpu.VMEM((rows, d), F32)]


def _unstack_heads(o_heads):
    lane = _lane_iota()
    chunks = []
    for c in range(2):
        chunks.append(jnp.where(lane < HEAD_DIM, pltpu.roll(o_heads[2 * c], HEAD_DIM, 1), o_heads[2 * c + 1]))
    return jnp.concatenate(chunks, axis=1)


def _dsa_rows_core(q, iq, iw, load_kv, load_ik, key_ref, m_ref, l_ref, acc_ref, *, tq, kc, n_top, p0, idx_bits):
    n_chunks = (p0 + tq - 1) // kc + 1
    pos = p0 + _row_iota(tq)
    iqs = _stack_heads(iq).astype(BF16)
    qs = (_stack_heads(q) * SCALE).astype(BF16)
    w_cols = [iw[:, HEAD_DIM + h:HEAD_DIM + h + 1] for h in range(N_HEADS)]

    def chunk_pos(c):
        return c * kc + _lane_iota(kc)

    def score_chunk(c, carry):
        lg = _dot_nt(iqs, load_ik(c))
        sc = w_cols[0] * jnp.maximum(lg[0:tq], 0.0)
        for h in range(1, N_HEADS):
            sc = sc + w_cols[h] * jnp.maximum(lg[h * tq:(h + 1) * tq], 0.0)
        sc = jnp.where(chunk_pos(c) <= pos, sc, NEG_INF)
        bits = pltpu.bitcast(sc, jnp.int32)
        key_ref[c] = bits ^ ((bits >> 31) & 0x7FFFFFFF)
        return carry

    lax.fori_loop(0, n_chunks, score_chunk, 0)

    def count(pred):
        def body(c, acc):
            hit = jnp.where(pred(key_ref[c], chunk_pos(c)), 1, 0)
            parts = [hit[:, t * LANES:(t + 1) * LANES] for t in range(kc // LANES)]
            while len(parts) > 1:
                parts = [a + b for a, b in zip(parts[0::2], parts[1::2])]
            return acc + parts[0]
        acc = lax.fori_loop(0, n_chunks, body, jnp.zeros((tq, LANES), jnp.int32))
        return jnp.sum(acc, axis=1, keepdims=True)

    def thr_bit(b, thr):
        cand = thr + jnp.left_shift(jnp.int32(1), 31 - b)
        cnt = count(lambda key, kpos: key >= cand)
        return jnp.where(cnt >= n_top, cand, thr)

    thr = lax.fori_loop(0, 32, thr_bit, jnp.full((tq, 1), INT_MIN, jnp.int32))

    def last_tied():
        need = n_top - count(lambda key, kpos: key > thr)

        def idx_bit(b, last):
            cand = last + jnp.left_shift(jnp.int32(1), idx_bits - 1 - b)
            cnt = count(lambda key, kpos: (key == thr) & (kpos < cand))
            return jnp.where(cnt < need, cand, last)

        return lax.fori_loop(0, idx_bits, idx_bit, jnp.zeros((tq, 1), jnp.int32))

    over = (count(lambda key, kpos: key >= thr) > n_top) & (thr > KEY_NEG_INF)
    last = lax.cond(jnp.max(jnp.where(over, 1, 0)) > 0, last_tied,
                    lambda: jnp.full((tq, 1), 2 ** idx_bits, jnp.int32))

    _row_softmax_init(m_ref, l_ref, acc_ref)

    def attend(c, carry):
        kvc = load_kv(c)
        s = _dot_nt(qs, kvc)
        key = key_ref[c]
        sel = (key > thr) | ((key == thr) & (chunk_pos(c) <= last))
        sel = sel & (key > KEY_NEG_INF) & (key < KEY_POS_INF)
        for h in range(N_HEADS):
            _row_softmax_step(pl.ds(h * tq, tq), s[h * tq:(h + 1) * tq], sel, kvc, m_ref, l_ref, acc_ref)
        return carry

    lax.fori_loop(0, n_chunks, attend, 0)
    return _unstack_heads([_row_softmax_result(pl.ds(h * tq, tq), l_ref, acc_ref) for h in range(N_HEADS)])


def _dsa_decode_kernel(pt_ref, q_ref, iq_ref, iw_ref, *rest, tq, kc, n_top, idx_bits, n_pages, page, pps):
    page_refs, (new_ref, o_ref, kv_ref, ik_ref, key_ref, m_ref, l_ref, acc_ref) = rest[:pps], rest[pps:]
    s = pl.program_id(1)
    n_steps = n_pages // pps

    def put(at, rows):
        kv_ref[at, :] = rows[:, 0:LANES].astype(BF16)
        ik = rows[:, LANES:]
        ik_ref[at, :] = jnp.concatenate([ik, jnp.zeros((rows.shape[0], 2 * LANES - rows.shape[1]), F32)],
                                        axis=1).astype(BF16)

    @pl.when(s < n_steps)
    def _():
        for r in range(pps):
            put(pl.ds(pl.multiple_of((s * pps + r) * page, page), page), page_refs[r][0, 0])

    @pl.when(s == n_steps)
    def _():
        put(pl.ds(n_pages * page, kc), new_ref[0])
        o_ref[0] = _dsa_rows_core(q_ref[0], iq_ref[0], iw_ref[0], _chunk_loader(kv_ref, kc), _chunk_loader(ik_ref, kc),
                                  key_ref, m_ref, l_ref, acc_ref, tq=tq, kc=kc, n_top=n_top, p0=n_pages * page,
                                  idx_bits=idx_bits)


def _dsa_scratch(n_kc, tq, kc):
    return ([pltpu.VMEM((n_kc, LANES, kc), BF16), pltpu.VMEM((n_kc, kc, tq), jnp.int32)]
            + _attend_scratch(LANES, N_HEADS * tq, kc))


def _page_specs(cache, layer, n_pages, pps):
    block = (1, 1) + cache.shape[2:]
    zeros = (0,) * (len(cache.shape) - 2)

    def spec(r):
        return pl.BlockSpec(block, lambda bi, s, pt: (layer, pt[bi, jnp.minimum(s * pps + r, n_pages - 1)]) + zeros)

    return [spec(r) for r in range(pps)]


def _dsa_decode_call(zq, cache, layer, page_table, new_rows, *, kc=DEC_KEY_CHUNK, pps=PAGES_PER_STEP):
    b, tq, _ = zq.shape
    n_pages = page_table.shape[1]
    page = cache.shape[2]
    past = n_pages * page
    assert past % kc == 0 and n_pages % pps == 0
    l_rows = past + kc
    n_kc = l_rows // kc
    kern = functools.partial(_dsa_decode_kernel, tq=tq, kc=kc, n_top=min(DSA_TOPK, (past + 1) // 4),
                             idx_bits=max(1, math.ceil(math.log2(l_rows))), n_pages=n_pages, page=page, pps=pps)
    grid_spec = pltpu.PrefetchScalarGridSpec(
        num_scalar_prefetch=1,
        grid=(b, n_pages // pps + 1),
        in_specs=[pl.BlockSpec((1, tq, 2 * LANES), lambda bi, s, pt: (bi, 0, _NEW["q_a"] // 256)),
                  pl.BlockSpec((1, tq, 2 * LANES), lambda bi, s, pt: (bi, 0, _NEW["iq"] // 256)),
                  pl.BlockSpec((1, tq, LANES), lambda bi, s, pt: (bi, 0, BLK_IK))]
        + _page_specs(cache, layer, n_pages, pps)
        + [pl.BlockSpec((1,) + new_rows.shape[1:], lambda bi, s, pt: (bi, 0, 0))],
        out_specs=pl.BlockSpec((1, tq, 2 * LANES), lambda bi, s, pt: (bi, 0, 0)),
        scratch_shapes=[pltpu.VMEM((l_rows, LANES), BF16), pltpu.VMEM((l_rows, LANES), BF16),
                        pltpu.VMEM((n_kc, tq, kc), jnp.int32)] + _row_softmax_scratch(N_HEADS * tq, LANES))
    return pl.pallas_call(
        kern, grid_spec=grid_spec,
        out_shape=jax.ShapeDtypeStruct((b, tq, 2 * LANES), F32),
        compiler_params=_params(("parallel", "arbitrary")),
        name="dsa_decode",
    )(page_table, zq, zq, zq, *([cache] * pps), new_rows)


def _dsa_call(zq, kv, ik, *, kv_blk, ik_blk, tq, n_keys, kc=KEY_CHUNK):
    b, t_q, _ = zq.shape
    l_rows = kv.shape[1]
    n_kc = l_rows // kc
    kern = functools.partial(_dsa_kernel, tq=tq, kc=kc, n_top=min(DSA_TOPK, n_keys // 4),
                             idx_bits=max(1, math.ceil(math.log2(l_rows))))
    return pl.pallas_call(
        kern,
        grid=(b, t_q // tq),
        in_specs=[pl.BlockSpec((1, tq, 2 * LANES), lambda bi, i: (bi, i, _NEW["q_a"] // 256)),
                  pl.BlockSpec((1, tq, 2 * LANES), lambda bi, i: (bi, i, _NEW["iq"] // 256)),
                  pl.BlockSpec((1, tq, LANES), lambda bi, i: (bi, i, BLK_IK)),
                  pl.BlockSpec((1, l_rows, LANES), lambda bi, i: (bi, 0, kv_blk)),
                  pl.BlockSpec((1, l_rows, LANES), lambda bi, i: (bi, 0, ik_blk))],
        out_specs=pl.BlockSpec((1, tq, 2 * LANES), lambda bi, i: (bi, i, 0)),
        out_shape=jax.ShapeDtypeStruct((b, t_q, 2 * LANES), F32),
        scratch_shapes=_dsa_scratch(n_kc, tq, kc),
        compiler_params=_params(("parallel", "arbitrary")),
        name="dsa_attention",
    )(zq, zq, zq, kv, ik)


def _compress(load_rows, wlo_ref, whi_ref, pe_ref, w2_ref, n_blocks):
    a = jnp.zeros((n_blocks, LANES), F32)
    b = jnp.zeros((n_blocks, LANES), F32)
    for p in range(NSA_CMP_STRIDE):
        xp = load_rows(p)
        a = a + _dot((xp + pe_ref[p:p + 1, :]).astype(BF16), wlo_ref[p])
        b = b + _dot((xp + pe_ref[NSA_CMP_STRIDE + p:NSA_CMP_STRIDE + p + 1, :]).astype(BF16), whi_ref[p])
    h = jax.nn.gelu(a + pltpu.roll(b, n_blocks - 1, 0))
    return _dot(h.astype(BF16), w2_ref[...])


def _cmp_kernel(rows_ref, wlo_ref, whi_ref, pe_ref, w2_ref, o_ref, *, n_blocks):
    o_ref[0] = _compress(lambda p: rows_ref[0, pl.ds(p, n_blocks, stride=NSA_CMP_STRIDE), :],
                         wlo_ref, whi_ref, pe_ref, w2_ref, n_blocks)


def _block_diag2(m0, m1):
    z = jnp.zeros_like(m0)
    return jnp.concatenate([jnp.concatenate([m0, z], axis=-1), jnp.concatenate([z, m1], axis=-1)], axis=-2)


def _compress_weights(w_cmp1, w_cmp2, pe_cmp):
    w1 = w_cmp1.reshape(2, NSA_CMP_LEN, HEAD_DIM, HEAD_DIM)
    wlo = _block_diag2(w1[0, :NSA_CMP_STRIDE], w1[1, :NSA_CMP_STRIDE]).astype(BF16)
    whi = _block_diag2(w1[0, NSA_CMP_STRIDE:], w1[1, NSA_CMP_STRIDE:]).astype(BF16)
    pe = jnp.concatenate([pe_cmp[0], pe_cmp[1]], axis=-1)
    w2 = _block_diag2(w_cmp2[0], w_cmp2[1]).astype(BF16)
    return wlo, whi, pe, w2


def _compress_specs(index_map3, index_map2):
    return [pl.BlockSpec((NSA_CMP_STRIDE, LANES, LANES), index_map3),
            pl.BlockSpec((NSA_CMP_STRIDE, LANES, LANES), index_map3),
            pl.BlockSpec((NSA_CMP_LEN, LANES), index_map2),
            pl.BlockSpec((LANES, LANES), index_map2)]


def _nsa_compress(rows, cmp_w, *, blk):
    b, l_rows, _ = rows.shape
    n_blocks = l_rows // NSA_CMP_STRIDE
    return pl.pallas_call(
        functools.partial(_cmp_kernel, n_blocks=n_blocks),
        grid=(b,),
        in_specs=[pl.BlockSpec((1, l_rows, LANES), lambda bi: (bi, 0, blk))]
        + _compress_specs(lambda bi: (0, 0, 0), lambda bi: (0, 0)),
        out_specs=pl.BlockSpec((1, n_blocks, LANES), lambda bi: (bi, 0, 0)),
        out_shape=jax.ShapeDtypeStruct((b, n_blocks, LANES), F32),
        compiler_params=_params(("parallel",)),
        name="nsa_compress",
    )(rows, *cmp_w)


def _nsa_core(q, g, cmp_rows, load_sel, load_selt, wrows, wpos, rank_ref, s_ref, p_ref, m_ref, l_ref, acc_ref,
              *, tq, kc, p0, n_cmp, n_blk, n_sel):
    pos = p0 + _lane_iota(tq)
    qs_t = (_stack_heads(q) * SCALE).T.astype(BF16)

    ncp = cmp_rows.shape[0]
    cmp_t = cmp_rows.T.astype(BF16)
    s_c = _dot(cmp_rows.astype(BF16), qs_t)
    n_idx = _row_iota(ncp)
    valid_c = (n_idx * NSA_CMP_STRIDE + (NSA_CMP_LEN - 1) <= pos) & (n_idx < n_cmp)
    p_c = [_tmasked_softmax(s_c[:, h * tq:(h + 1) * tq], valid_c) for h in range(N_HEADS)]
    o_c = _dot(cmp_t, jnp.concatenate(p_c, axis=1).astype(BF16))
    p_sum = p_c[0] + p_c[1] + p_c[2] + p_c[3]

    nbp = rank_ref.shape[0]
    jb = lax.broadcasted_iota(jnp.int32, (nbp, ncp), 0)
    nn = lax.broadcasted_iota(jnp.int32, (nbp, ncp), 1)
    cover_t = ((nn * NSA_CMP_STRIDE < (jb + 1) * NSA_SEL_BLOCK)
               & (nn * NSA_CMP_STRIDE + (NSA_CMP_LEN - 1) >= jb * NSA_SEL_BLOCK) & (nn < n_cmp)).astype(BF16)
    hi, lo = _split_bf16(p_sum)
    imp = _dot(cover_t, hi) + _dot(cover_t, lo)
    j = _row_iota(nbp)
    cur = pos // NSA_SEL_BLOCK
    forced = (j == 0) | (j == cur) | (j == cur - 1)
    imp = jnp.where(forced, jnp.inf, jnp.where(j <= cur, imp, NEG_INF))
    n_vis = jnp.minimum((p0 + tq - 1) // NSA_SEL_BLOCK + 1, n_blk)
    sel = _select_top_t(imp, rank_ref, n_vis, n_sel).astype(BF16)

    n_chunks = (p0 + tq - 1) // kc + 1
    ek = lax.broadcasted_iota(jnp.int32, (kc, nbp), 0)
    eb = lax.broadcasted_iota(jnp.int32, (kc, nbp), 1)

    def selected(c):
        expand = (eb == (c * kc + ek) // NSA_SEL_BLOCK).astype(BF16)
        picked = _dot(expand, sel)

        def strip(r0, rows):
            return (picked[r0:r0 + rows] > 0.5) & (c * kc + r0 + _row_iota(rows) <= pos)
        return strip

    _attend_loop(n_chunks, lambda c, r0, n: _dot(load_sel(c, r0, n), qs_t), selected, load_selt,
                 s_ref, p_ref, m_ref, l_ref, acc_ref, reps=N_HEADS)
    o_s = _tsoftmax_result(l_ref, acc_ref)

    s_w = _dot(wrows.astype(BF16), qs_t)
    valid_w = (wpos <= pos) & (pos - wpos < NSA_WINDOW)
    p_w = [_tmasked_softmax(s_w[:, h * tq:(h + 1) * tq], valid_w) for h in range(N_HEADS)]
    o_w = _dot(wrows.T.astype(BF16), jnp.concatenate(p_w, axis=1).astype(BF16))

    g_t = _rows_t(g)
    mixed = []
    for h in range(N_HEADS):
        cols = slice(h * tq, (h + 1) * tq)
        mixed.append(g_t[3 * h:3 * h + 1] * o_c[:, cols] + g_t[3 * h + 1:3 * h + 2] * o_s[:, cols]
                     + g_t[3 * h + 2:3 * h + 3] * o_w[:, cols])
    return _unstack_heads_t(jnp.concatenate(mixed, axis=1), tq)


def _nsa_kernel(q_ref, g_ref, cmp_ref, sel_ref, win_ref, o_ref, selt_ref, rank_ref, s_ref, p_ref, m_ref, l_ref, acc_ref,
                *, tq, kc, n_cmp, n_blk, n_sel, win_rows):
    i = pl.program_id(1)

    @pl.when(i == 0)
    def _():
        _fill_transposed(sel_ref, selt_ref, kc)

    p0 = i * tq
    start = pl.multiple_of(jnp.clip(p0 - NSA_WINDOW, 0, win_ref.shape[1] - win_rows), SUBLANES)
    o_ref[0] = _nsa_core(q_ref[0], g_ref[0], cmp_ref[0], _chunk_loader(sel_ref, kc), lambda c: selt_ref[c],
                         win_ref[0, pl.ds(start, win_rows), :], start + _row_iota(win_rows),
                         rank_ref, s_ref, p_ref, m_ref, l_ref, acc_ref,
                         tq=tq, kc=kc, p0=p0, n_cmp=n_cmp, n_blk=n_blk, n_sel=n_sel)


def _top_mask_row(v, n_sel):
    n = v.shape[1]
    i = lax.broadcasted_iota(jnp.int32, (n, n), 0)
    j = lax.broadcasted_iota(jnp.int32, (n, n), 1)
    v_col = jnp.sum(jnp.where(i == j, v, 0.0), axis=1, keepdims=True)
    beats = (v_col > v) | ((v_col == v) & (i < j))
    rank = jnp.sum(jnp.where(beats, 1, 0), axis=0, keepdims=True)
    return jnp.where((rank < n_sel) & (v > NEG_INF), 1.0, 0.0)


def _nsa_rows_core(q, g, cmpv, load_sel, wrows, wpos, m_ref, l_ref, acc_ref, *, tq, kc, p0, n_cmp, n_blk, n_sel):
    pos = p0 + _row_iota(tq)
    qs = (_stack_heads(q) * SCALE).astype(BF16)

    ncp = cmpv.shape[0]
    s_c = _dot_nt(qs, cmpv)
    n_idx = _lane_iota(ncp)
    valid_c = (n_idx * NSA_CMP_STRIDE + (NSA_CMP_LEN - 1) <= pos) & (n_idx < n_cmp)
    o_c = []
    p_sum = jnp.zeros((tq, ncp), F32)
    for h in range(N_HEADS):
        p = _row_masked_softmax(s_c[h * tq:(h + 1) * tq], valid_c)
        p_sum = p_sum + p
        o_c.append(_dot(p.astype(BF16), cmpv))

    nbp = -(-n_blk // LANES) * LANES
    nn = lax.broadcasted_iota(jnp.int32, (ncp, nbp), 0)
    jb = lax.broadcasted_iota(jnp.int32, (ncp, nbp), 1)
    cover = ((nn * NSA_CMP_STRIDE < (jb + 1) * NSA_SEL_BLOCK)
             & (nn * NSA_CMP_STRIDE + (NSA_CMP_LEN - 1) >= jb * NSA_SEL_BLOCK) & (nn < n_cmp)).astype(BF16)
    hi, lo = _split_bf16(p_sum)
    imp = _dot(hi, cover) + _dot(lo, cover)
    j = _lane_iota(nbp)
    cur = pos // NSA_SEL_BLOCK
    forced = (j == 0) | (j == cur) | (j == cur - 1)
    imp = jnp.where(forced, jnp.inf, jnp.where(j <= cur, imp, NEG_INF))
    sel = jnp.broadcast_to(_top_mask_row(imp[0:1], n_sel), (tq, nbp)).astype(BF16)

    _row_softmax_init(m_ref, l_ref, acc_ref)
    n_chunks = (p0 + tq - 1) // kc + 1
    eb = lax.broadcasted_iota(jnp.int32, (nbp, kc), 0)
    ek = lax.broadcasted_iota(jnp.int32, (nbp, kc), 1)

    def attend(c, carry):
        rows = load_sel(c)
        s = _dot_nt(qs, rows)
        expand = (eb == (c * kc + ek) // NSA_SEL_BLOCK).astype(BF16)
        picked = _dot(sel, expand)
        valid = (picked > 0.5) & (c * kc + _lane_iota(kc) <= pos)
        for h in range(N_HEADS):
            _row_softmax_step(pl.ds(h * tq, tq), s[h * tq:(h + 1) * tq], valid, rows, m_ref, l_ref, acc_ref)
        return carry

    lax.fori_loop(0, n_chunks, attend, 0)

    s_w = _dot_nt(qs, wrows)
    valid_w = (wpos <= pos) & (pos - wpos < NSA_WINDOW)
    heads = []
    for h in range(N_HEADS):
        p_w = _row_masked_softmax(s_w[h * tq:(h + 1) * tq], valid_w)
        o_w = _dot(p_w.astype(BF16), wrows)
        o_s = _row_softmax_result(pl.ds(h * tq, tq), l_ref, acc_ref)
        heads.append(g[:, 3 * h:3 * h + 1] * o_c[h] + g[:, 3 * h + 1:3 * h + 2] * o_s
                     + g[:, 3 * h + 2:3 * h + 3] * o_w)
    return _unstack_heads(heads)


def _nsa_decode_kernel(pt_ref, q_ref, g_ref, *rest, tq, kc, n_cmp, n_blk, n_sel, n_pages, page, pps):
    page_refs = rest[:pps]
    (new_ref, win_ref, wlo_ref, whi_ref, pe_ref, w2_ref, o_ref, cbuf_ref, sbuf_ref, m_ref, l_ref, acc_ref) = rest[pps:]
    s = pl.program_id(1)
    n_steps = n_pages // pps

    @pl.when(s < n_steps)
    def _():
        for r in range(pps):
            at = pl.ds(pl.multiple_of((s * pps + r) * page, page), page)
            cbuf_ref[at, :] = page_refs[r][0, 0, :, 0:LANES]
            sbuf_ref[at, :] = page_refs[r][0, 0, :, LANES:2 * LANES].astype(BF16)

    @pl.when(s == n_steps)
    def _():
        past = n_pages * page
        sbuf_ref[pl.ds(past, kc), :] = new_ref[0, :, LANES:2 * LANES].astype(BF16)
        n_blocks = past // NSA_CMP_STRIDE
        cmpv = _compress(lambda r: cbuf_ref[pl.ds(r, n_blocks, stride=NSA_CMP_STRIDE), :],
                         wlo_ref, whi_ref, pe_ref, w2_ref, n_blocks).astype(BF16)
        wrows = win_ref[0].astype(BF16)
        o_ref[0] = _nsa_rows_core(q_ref[0], g_ref[0], cmpv, _chunk_loader(sbuf_ref, kc), wrows,
                                  past - NSA_WINDOW + _lane_iota(wrows.shape[0]), m_ref, l_ref, acc_ref,
                                  tq=tq, kc=kc, p0=past, n_cmp=n_cmp, n_blk=n_blk, n_sel=n_sel)


def _nsa_sizes(n_keys):
    n_cmp = (n_keys - NSA_CMP_LEN) // NSA_CMP_STRIDE + 1
    n_blk = -(-n_keys // NSA_SEL_BLOCK)
    return n_cmp, n_blk, min(NSA_SEL_TOPN, n_blk), -(-n_blk // LANES) * LANES


def _nsa_scratch(n_kc, nbp, tq, kc):
    return ([pltpu.VMEM((n_kc, LANES, kc), BF16), pltpu.VMEM((nbp, tq), F32)]
            + _attend_scratch(LANES, N_HEADS * tq, kc))


def _nsa_decode_call(zq, cache, layer, page_table, new_rows, win_rows, cmp_w, *, kc=DEC_KEY_CHUNK,
                     pps=PAGES_PER_STEP):
    b, tq, _ = zq.shape
    n_pages = page_table.shape[1]
    page = cache.shape[2]
    past = n_pages * page
    assert past % kc == 0 and n_pages % pps == 0
    n_cmp, n_blk, n_sel, _ = _nsa_sizes(past + 1)
    kern = functools.partial(_nsa_decode_kernel, tq=tq, kc=kc, n_cmp=n_cmp, n_blk=n_blk, n_sel=n_sel,
                             n_pages=n_pages, page=page, pps=pps)
    grid_spec = pltpu.PrefetchScalarGridSpec(
        num_scalar_prefetch=1,
        grid=(b, n_pages // pps + 1),
        in_specs=[pl.BlockSpec((1, tq, 2 * LANES), lambda bi, s, pt: (bi, 0, _NEW["q_b"] // 256)),
                  pl.BlockSpec((1, tq, LANES), lambda bi, s, pt: (bi, 0, BLK_GB))]
        + _page_specs(cache, layer, n_pages, pps)
        + [pl.BlockSpec((1,) + new_rows.shape[1:], lambda bi, s, pt: (bi, 0, 0)),
           pl.BlockSpec((1,) + win_rows.shape[1:], lambda bi, s, pt: (bi, 0, 0))]
        + _compress_specs(lambda bi, s, pt: (0, 0, 0), lambda bi, s, pt: (0, 0)),
        out_specs=pl.BlockSpec((1, tq, 2 * LANES), lambda bi, s, pt: (bi, 0, 0)),
        scratch_shapes=[pltpu.VMEM((past, LANES), F32), pltpu.VMEM((past + kc, LANES), BF16)]
        + _row_softmax_scratch(N_HEADS * tq, LANES))
    return pl.pallas_call(
        kern, grid_spec=grid_spec,
        out_shape=jax.ShapeDtypeStruct((b, tq, 2 * LANES), F32),
        compiler_params=_params(("parallel", "arbitrary")),
        name="nsa_decode",
    )(page_table, zq, zq, *([cache] * pps), new_rows, win_rows, *cmp_w)


def _nsa_call(zq, cmp, sel, win, *, sel_blk, win_blk, tq, n_keys, kc=KEY_CHUNK):
    b, t_q, _ = zq.shape
    l_rows = sel.shape[1]
    n_cmp, n_blk, n_sel, nbp = _nsa_sizes(n_keys)
    kern = functools.partial(_nsa_kernel, tq=tq, kc=kc, n_cmp=n_cmp, n_blk=n_blk, n_sel=n_sel,
                             win_rows=min(NSA_WINDOW + tq, l_rows))
    return pl.pallas_call(
        kern,
        grid=(b, t_q // tq),
        in_specs=[pl.BlockSpec((1, tq, 2 * LANES), lambda bi, i: (bi, i, _NEW["q_b"] // 256)),
                  pl.BlockSpec((1, tq, LANES), lambda bi, i: (bi, i, BLK_GB)),
                  pl.BlockSpec((1, cmp.shape[1], LANES), lambda bi, i: (bi, 0, 0)),
                  pl.BlockSpec((1, l_rows, LANES), lambda bi, i: (bi, 0, sel_blk)),
                  pl.BlockSpec((1, l_rows, LANES), lambda bi, i: (bi, 0, win_blk))],
        out_specs=pl.BlockSpec((1, tq, 2 * LANES), lambda bi, i: (bi, i, 0)),
        out_shape=jax.ShapeDtypeStruct((b, t_q, 2 * LANES), F32),
        scratch_shapes=_nsa_scratch(l_rows // kc, nbp, tq, kc),
        compiler_params=_params(("parallel", "arbitrary")),
        name="nsa_attention",
    )(zq, zq, cmp, sel, win)


def _moba_kernel(q_ref, k_ref, v_ref, o_ref, kmean_ref, vt_ref, rank_ref, sel_ref, s_ref, p_ref, m_ref, l_ref, acc_ref,
                 *, tq, n_blocks, n_sel):
    c0 = pl.program_id(2)

    @pl.when(c0 == 0)
    def _():
        kmean_ref[...] = jnp.zeros(kmean_ref.shape, F32)

        def mean_block(blk, carry):
            rows = k_ref[0, pl.ds(pl.multiple_of(blk * MOBA_BLOCK, MOBA_BLOCK), MOBA_BLOCK), :]
            kmean_ref[pl.ds(blk, 1), :] = jnp.sum(rows, axis=0, keepdims=True) * (1.0 / MOBA_BLOCK)
            return carry

        lax.fori_loop(0, n_blocks, mean_block, 0)
        _fill_transposed(v_ref, vt_ref, MOBA_BLOCK)

    nbp = kmean_ref.shape[0]
    qst = _pair_heads(q_ref[0])
    qst_t = qst.T
    q_hi, q_lo = _split_bf16(qst_t)
    k_hi, k_lo = _split_bf16(kmean_ref[...])
    gate = _dot(k_hi, q_hi) + _dot(k_hi, q_lo) + _dot(k_lo, q_hi)
    gate = jnp.where(_row_iota(nbp) < c0, gate, NEG_INF)
    sel_ref[...] = _select_top_t(gate, rank_ref, c0, n_sel)
    qs_t = (qst_t * SCALE).astype(BF16)
    load_k = _chunk_loader(k_ref, MOBA_BLOCK)
    q_in_block = _lane_iota(2 * tq) % tq

    def admissible(blk):
        picked = sel_ref[pl.ds(blk, 1), :] > 0.5

        def strip(r0, rows):
            return picked | ((blk == c0) & (r0 + _row_iota(rows) <= q_in_block))
        return strip

    _attend_loop(c0 + 1, lambda blk, r0, n: _dot(load_k(blk, r0, n), qs_t), admissible,
                 lambda blk: vt_ref[blk],
                 s_ref, p_ref, m_ref, l_ref, acc_ref, reps=1)
    o = _tsoftmax_result(l_ref, acc_ref).T
    o_ref[0] = jnp.where(_lane_iota() < HEAD_DIM, o[0:tq], o[tq:2 * tq])


def _moba_call(zq, kv, *, tq):
    b, t_q, _ = zq.shape
    l_rows = kv.shape[1]
    assert tq == MOBA_BLOCK
    n_blocks = l_rows // MOBA_BLOCK
    nbp = -(-n_blocks // LANES) * LANES
    kern = functools.partial(_moba_kernel, tq=tq, n_blocks=n_blocks, n_sel=min(MOBA_TOPK, n_blocks))
    q_blk0 = _NEW["q_c"] // LANES
    return pl.pallas_call(
        kern,
        grid=(b, 2, t_q // tq),
        in_specs=[pl.BlockSpec((1, tq, LANES), lambda bi, c, i: (bi, i, q_blk0 + c)),
                  pl.BlockSpec((1, l_rows, LANES), lambda bi, c, i: (bi, 0, BLK_KC + c)),
                  pl.BlockSpec((1, l_rows, LANES), lambda bi, c, i: (bi, 0, BLK_VC + c))],
        out_specs=pl.BlockSpec((1, tq, LANES), lambda bi, c, i: (bi, i, c)),
        out_shape=jax.ShapeDtypeStruct((b, t_q, 2 * LANES), F32),
        scratch_shapes=[pltpu.VMEM((nbp, LANES), F32), pltpu.VMEM((n_blocks, LANES, MOBA_BLOCK), BF16),
                        pltpu.VMEM((nbp, 2 * tq), F32), pltpu.VMEM((nbp, 2 * tq), F32)]
        + _attend_scratch(LANES, 2 * tq, MOBA_BLOCK),
        compiler_params=_params(("parallel", "parallel", "arbitrary")),
        name="moba_attention",
    )(zq, kv, kv)


def _moba_pick_kernel(pt_ref, q_ref, *rest, n_blocks, n_sel, n_pages, page, pps):
    page_refs, (o_ref, kmean_ref, rank_ref) = rest[:pps], rest[pps:]
    s = pl.program_id(1)
    per_block = MOBA_BLOCK // page
    n_steps = n_pages // pps
    nbp = kmean_ref.shape[0]

    @pl.when(s == 0)
    def _():
        kmean_ref[...] = jnp.zeros(kmean_ref.shape, F32)

    @pl.when(s < n_steps)
    def _():
        for j in range(pps // per_block):
            tot = None
            for r in range(per_block):
                part = jnp.sum(page_refs[j * per_block + r][0, 0], axis=0, keepdims=True)
                tot = part if tot is None else tot + part
            kmean_ref[pl.ds(s * (pps // per_block) + j, 1), :] = tot * (1.0 / MOBA_BLOCK)

    @pl.when(s == n_steps)
    def _():
        prod = kmean_ref[...] * q_ref[0, 0:1, :]
        seg = (lax.broadcasted_iota(jnp.int32, (N_HEADS * HEAD_DIM, LANES), 0) // HEAD_DIM
               == lax.broadcasted_iota(jnp.int32, (N_HEADS * HEAD_DIM, LANES), 1)).astype(BF16)
        hi, lo = _split_bf16(prod)
        lo2 = (prod - hi.astype(F32) - lo.astype(F32)).astype(BF16)
        gate = _dot(hi, seg) + _dot(lo, seg) + _dot(lo2, seg)
        j = _row_iota(nbp)
        gate = jnp.where(j < n_blocks, gate, NEG_INF)
        rank = _rank_rows(gate, rank_ref, n_blocks)
        rows = []
        for r in range(n_sel):
            hit = (rank == r) & (gate > NEG_INF)
            rows.append(jnp.sum(jnp.where(hit, j, 0), axis=0, keepdims=True))
        for r in range(n_sel):
            hit = (rank == r) & (gate > NEG_INF)
            rows.append(jnp.sum(jnp.where(hit, 1, 0), axis=0, keepdims=True))
        rows.append(jnp.zeros((SUBLANES - 2 * n_sel, LANES), jnp.int32))
        o_ref[0] = jnp.concatenate(rows, axis=0)


def _moba_pick_call(zq, cache, layer, page_table, *, pps=PAGES_PER_STEP):
    b, tq, _ = zq.shape
    n_pages = page_table.shape[1]
    page = cache.shape[2]
    n_blocks = n_pages * page // MOBA_BLOCK
    n_sel = min(MOBA_TOPK, n_blocks + 1)
    assert MOBA_BLOCK % page == 0 and n_pages % pps == 0 and pps % (MOBA_BLOCK // page) == 0
    assert 2 * n_sel <= SUBLANES
    nbp = -(-n_blocks // LANES) * LANES
    block = (1, 1, page, N_HEADS * HEAD_DIM)

    def spec(r):
        return pl.BlockSpec(block, lambda bi, s, pt: (layer, pt[bi, jnp.minimum(s * pps + r, n_pages - 1)], 0, 0))

    grid_spec = pltpu.PrefetchScalarGridSpec(
        num_scalar_prefetch=1,
        grid=(b, n_pages // pps + 1),
        in_specs=[pl.BlockSpec((1, tq, 2 * LANES), lambda bi, s, pt: (bi, 0, _NEW["q_c"] // 256))]
        + [spec(r) for r in range(pps)],
        out_specs=pl.BlockSpec((1, SUBLANES, LANES), lambda bi, s, pt: (bi, 0, 0)),
        scratch_shapes=[pltpu.VMEM((nbp, N_HEADS * HEAD_DIM), F32), pltpu.VMEM((nbp, LANES), F32)])
    kern = functools.partial(_moba_pick_kernel, n_blocks=n_blocks, n_sel=n_sel, n_pages=n_pages, page=page, pps=pps)
    return pl.pallas_call(
        kern, grid_spec=grid_spec,
        out_shape=jax.ShapeDtypeStruct((b, SUBLANES, LANES), jnp.int32),
        compiler_params=_params(("parallel", "arbitrary")),
        name="moba_pick",
    )(page_table, zq, *([cache] * pps)), n_sel


def _moba_gather_kernel(pid_ref, ok_ref, q_ref, *rest, tq, n_sel, per_block, page):
    n_pg = N_HEADS * per_block
    k_refs, v_refs = rest[:n_pg], rest[n_pg:2 * n_pg]
    new_ref, o_ref, m_ref, l_ref, acc_ref = rest[2 * n_pg:]
    bi, r = pl.program_id(0), pl.program_id(1)

    @pl.when(r == 0)
    def _():
        m_ref[...] = jnp.full(m_ref.shape, NEG_INF, F32)
        l_ref[...] = jnp.zeros(l_ref.shape, F32)
        acc_ref[...] = jnp.zeros(acc_ref.shape, F32)

    qs = (jnp.concatenate([_pair_heads(q_ref[0, :, c * LANES:(c + 1) * LANES]) for c in range(2)], axis=0)
          * SCALE).astype(BF16)
    for h in range(N_HEADS):
        rows = pl.ds(h * tq, tq)
        kb = jnp.concatenate([k_refs[h * per_block + j][0, 0] for j in range(per_block)], axis=0).astype(BF16)
        vb = jnp.concatenate([v_refs[h * per_block + j][0, 0] for j in range(per_block)], axis=0).astype(BF16)
        ok = ok_ref[bi, r * N_HEADS + h] > 0
        _row_softmax_step(rows, _dot_nt(qs[h * tq:(h + 1) * tq], kb), ok, vb, m_ref, l_ref, acc_ref)

    @pl.when(r == n_sel - 1)
    def _():
        ki = lax.broadcasted_iota(jnp.int32, (tq, page), 1)
        qi = lax.broadcasted_iota(jnp.int32, (tq, page), 0)
        n_k = new_ref.shape[2] // 2
        outs = []
        for h in range(N_HEADS):
            rows = pl.ds(h * tq, tq)
            c = h // 2
            kb = new_ref[0, :, c * LANES:(c + 1) * LANES].astype(BF16)
            vb = new_ref[0, :, n_k + c * LANES:n_k + (c + 1) * LANES].astype(BF16)
            _row_softmax_step(rows, _dot_nt(qs[h * tq:(h + 1) * tq], kb), ki <= qi, vb, m_ref, l_ref, acc_ref)
            outs.append(acc_ref[rows, :] / jnp.maximum(l_ref[rows, :], 1e-30))
        lane = _lane_iota()
        o_ref[0] = jnp.concatenate([jnp.where(lane < HEAD_DIM, outs[0], outs[1]),
                                    jnp.where(lane < HEAD_DIM, outs[2], outs[3])], axis=1)


def _moba_decode_call(zq, cache, layer, page_table, new_page):
    b, tq, _ = zq.shape
    page = cache.shape[2]
    per_block = MOBA_BLOCK // page
    picks, n_sel = _moba_pick_call(zq, cache, layer, page_table)
    blk = picks[:, :n_sel, :N_HEADS]
    ok = picks[:, n_sel:2 * n_sel, :N_HEADS].reshape(b, n_sel * N_HEADS)
    logical = blk[..., None] * per_block + jnp.arange(per_block, dtype=jnp.int32)
    pid = jnp.take_along_axis(page_table, logical.reshape(b, -1), axis=1)
    block = (1, 1, page, LANES)

    def spec(h, j, field):
        return pl.BlockSpec(block, lambda bi, r, pid_ref, ok_ref:
                            (layer, pid_ref[bi, (r * N_HEADS + h) * per_block + j], 0, 2 * field + h // 2))

    hj = [(h, j) for h in range(N_HEADS) for j in range(per_block)]
    grid_spec = pltpu.PrefetchScalarGridSpec(
        num_scalar_prefetch=2,
        grid=(b, n_sel),
        in_specs=[pl.BlockSpec((1, tq, 2 * LANES), lambda bi, r, pid_ref, ok_ref: (bi, 0, _NEW["q_c"] // 256))]
        + [spec(h, j, 0) for h, j in hj] + [spec(h, j, 1) for h, j in hj]
        + [pl.BlockSpec((1,) + new_page.shape[1:], lambda bi, r, pid_ref, ok_ref: (bi, 0, 0))],
        out_specs=pl.BlockSpec((1, tq, 2 * LANES), lambda bi, r, pid_ref, ok_ref: (bi, 0, 0)),
        scratch_shapes=_row_softmax_scratch(N_HEADS * tq, LANES))
    kern = functools.partial(_moba_gather_kernel, tq=tq, n_sel=n_sel, per_block=per_block, page=page)
    return pl.pallas_call(
        kern, grid_spec=grid_spec,
        out_shape=jax.ShapeDtypeStruct((b, tq, 2 * LANES), F32),
        compiler_params=_params(("parallel", "arbitrary")),
        name="moba_decode",
    )(pid, ok, zq, *([cache] * (2 * len(hj))), new_page)


def _memory_kv(mem, g_ln, w_kv, g_k):
    b, m_rows, d = mem.shape
    n = w_kv.shape[1]
    half = n // 2
    mask = np.zeros((n,), np.float32)
    mask[:half] = 1
    zero = jnp.zeros((n,), F32)
    gain = jnp.concatenate([jnp.tile(g_k.astype(F32), half // HEAD_DIM), jnp.ones((half,), F32)])
    cfg = jnp.stack([jnp.asarray(mask), gain, zero, zero, zero, zero, zero, zero])
    masks = dict(norm=mask, rope=np.zeros_like(mask), sig=np.zeros_like(mask))
    rope = jnp.zeros((m_rows, 3 * LANES), F32)
    out = _project(mem.reshape(b * m_rows, d), g_ln.reshape(1, d), w_kv.astype(BF16), cfg, rope, masks,
                   tm=m_rows, tn=n)
    return out.reshape(b, m_rows, n)


def _mem_kernel(q_ref, kv_ref, o_ref, *, tq):
    lane = _lane_iota()
    n_kv = kv_ref.shape[2] // 2
    chunks = []
    for c in range(2):
        qst = _pair_heads(q_ref[0, :, c * LANES:(c + 1) * LANES])
        kb = kv_ref[0, :, c * LANES:(c + 1) * LANES].astype(BF16)
        vb = kv_ref[0, :, n_kv + c * LANES:n_kv + (c + 1) * LANES].astype(BF16)
        s = _dot_nt((qst * SCALE).astype(BF16), kb)
        e = jnp.exp(s - jnp.max(s, axis=-1, keepdims=True))
        p = e / jnp.sum(e, axis=-1, keepdims=True)
        o = _dot(p.astype(BF16), vb)
        chunks.append(jnp.where(lane < HEAD_DIM, o[0:tq], o[tq:2 * tq]))
    o_ref[0] = jnp.concatenate(chunks, axis=1)


def _mem_call(zq, mkv, *, tq):
    b, t_q, _ = zq.shape
    tq = min(tq, t_q)
    return pl.pallas_call(
        functools.partial(_mem_kernel, tq=tq),
        grid=(b, t_q // tq),
        in_specs=[pl.BlockSpec((1, tq, 2 * LANES), lambda bi, i: (bi, i, _NEW["q_m"] // 256)),
                  pl.BlockSpec((1,) + mkv.shape[1:], lambda bi, i: (bi, 0, 0))],
        out_specs=pl.BlockSpec((1, tq, 2 * LANES), lambda bi, i: (bi, i, 0)),
        out_shape=jax.ShapeDtypeStruct((b, t_q, 2 * LANES), F32),
        compiler_params=_params(("parallel", "parallel")),
        name="mem_attention",
    )(zq, mkv)


def _combine_kernel(x_ref, oa_ref, ob_ref, oc_ref, om_ref, gate_ref, wb_ref, wo_ref, y_ref):
    d = x_ref.shape[-1]
    h = None
    for bi, o_ref in enumerate((oa_ref, ob_ref, oc_ref, om_ref)):
        t = gate_ref[:, bi * d:(bi + 1) * d] * _dot(o_ref[...].astype(BF16), wb_ref[bi])
        h = t if h is None else h + t
    y_ref[...] = x_ref[...] + _dot(h.astype(BF16), wo_ref[...])


def _combine(x2d, outs, gate, w_branch, w_out, *, tm):
    m, d = x2d.shape
    tm = min(tm, m)
    bw = outs[0].shape[-1]
    o_spec = pl.BlockSpec((tm, bw), lambda i: (i, 0))
    return pl.pallas_call(
        _combine_kernel,
        grid=(m // tm,),
        in_specs=[pl.BlockSpec((tm, d), lambda i: (i, 0)), o_spec, o_spec, o_spec, o_spec,
                  pl.BlockSpec((tm, N_BRANCH * d), lambda i: (i, 0)),
                  pl.BlockSpec((N_BRANCH, bw, d), lambda i: (0, 0, 0)),
                  pl.BlockSpec((d, d), lambda i: (0, 0))],
        out_specs=pl.BlockSpec((tm, d), lambda i: (i, 0)),
        out_shape=jax.ShapeDtypeStruct((m, d), F32),
        compiler_params=_params(("parallel",)),
        name="branch_mix",
    )(x2d, *outs, gate, w_branch, w_out)


FF_CHUNK = 256
HALO = 16


def _rms(x, g):
    return x * lax.rsqrt(jnp.mean(x * x, axis=-1, keepdims=True) + RMS_EPS) * g


def _conv3(cw, u2, u1, u0):
    return cw[3:4] + cw[0:1] * u2 + cw[1:2] * u1 + cw[2:3] * u0


def _ffn_kernel(x_ref, xh_ref, ha_ref, hb_ref, g_ref, wa_ref, wb_ref, cwa_ref, cwb_ref, wdn_ref,
                y_ref, sta_ref, stb_ref, xn_ref, xhn_ref, acc_ref, *, tm):
    i = pl.program_id(1)
    j = pl.program_id(2)

    @pl.when(j == 0)
    def _():
        xn_ref[...] = _rms(x_ref[0], g_ref[...]).astype(BF16)
        xhn_ref[...] = _rms(xh_ref[0], g_ref[...]).astype(BF16)
        acc_ref[...] = jnp.zeros(acc_ref.shape, F32)

    def half(w_ref, hist_ref, cw_ref, st_ref):
        u = _dot(xn_ref[...], w_ref[...])
        u_prev = _dot(xhn_ref[...], w_ref[...])[HALO - SUBLANES:HALO]
        prev = jnp.where(i == 0, hist_ref[0], u_prev)
        ext = jnp.concatenate([prev, u], axis=0)
        st_ref[0, 0] = ext[tm:tm + SUBLANES]
        return _conv3(cw_ref[...], pltpu.roll(ext, 2, 0)[SUBLANES:], pltpu.roll(ext, 1, 0)[SUBLANES:], u)

    a = half(wa_ref, ha_ref, cwa_ref, sta_ref)
    b = half(wb_ref, hb_ref, cwb_ref, stb_ref)
    acc_ref[...] += _dot((a * jax.nn.sigmoid(a) * b).astype(BF16), wdn_ref[...])

    @pl.when(j == pl.num_programs(2) - 1)
    def _():
        y_ref[0] = x_ref[0] + acc_ref[...]


def _conv_table(conv_w, conv_b):
    return jnp.concatenate([conv_w, conv_b[None, :], jnp.zeros((SUBLANES - CONV_WIDTH - 1, conv_b.shape[0]), F32)])


def _conv_ffn(x, hist, g, w_up, conv_w, conv_b, w_down, *, tm):
    b, t, d = x.shape
    d_ff = w_down.shape[0]
    n_j = d_ff // FF_CHUNK
    tm = min(tm, t)
    cw = _conv_table(conv_w, conv_b)
    hist8 = jnp.concatenate([jnp.zeros((b, SUBLANES - 2, 2 * d_ff), F32), hist], axis=1)
    a_col = lambda bi, i, j: (0, j)
    b_col = lambda bi, i, j: (0, n_j + j)
    st_spec = pl.BlockSpec((1, 1, SUBLANES, FF_CHUNK), lambda bi, i, j: (bi, i, 0, j))
    st_shape = jax.ShapeDtypeStruct((b, t // tm, SUBLANES, d_ff), F32)
    y, st_a, st_b = pl.pallas_call(
        functools.partial(_ffn_kernel, tm=tm),
        grid=(b, t // tm, n_j),
        in_specs=[pl.BlockSpec((1, tm, d), lambda bi, i, j: (bi, i, 0)),
                  pl.BlockSpec((1, HALO, d), lambda bi, i, j: (bi, jnp.maximum(i * (tm // HALO) - 1, 0), 0)),
                  pl.BlockSpec((1, SUBLANES, FF_CHUNK), lambda bi, i, j: (bi, 0, j)),
                  pl.BlockSpec((1, SUBLANES, FF_CHUNK), lambda bi, i, j: (bi, 0, n_j + j)),
                  pl.BlockSpec((1, d), lambda bi, i, j: (0, 0)),
                  pl.BlockSpec((d, FF_CHUNK), a_col), pl.BlockSpec((d, FF_CHUNK), b_col),
                  pl.BlockSpec((SUBLANES, FF_CHUNK), a_col), pl.BlockSpec((SUBLANES, FF_CHUNK), b_col),
                  pl.BlockSpec((FF_CHUNK, d), lambda bi, i, j: (j, 0))],
        out_specs=[pl.BlockSpec((1, tm, d), lambda bi, i, j: (bi, i, 0)), st_spec, st_spec],
        out_shape=[jax.ShapeDtypeStruct((b, t, d), F32), st_shape, st_shape],
        scratch_shapes=[pltpu.VMEM((tm, d), BF16), pltpu.VMEM((HALO, d), BF16), pltpu.VMEM((tm, d), F32)],
        compiler_params=_params(("parallel", "arbitrary", "arbitrary")),
        name="conv_ffn",
    )(x, x, hist8, hist8, g.reshape(1, d), w_up, w_up, cw, cw, w_down)
    return y, jnp.concatenate([st_a[:, -1, SUBLANES - 2:], st_b[:, -1, SUBLANES - 2:]], axis=-1)


def _ffn_row_kernel(x_ref, h0a_ref, h0b_ref, h1a_ref, h1b_ref, g_ref, wa_ref, wb_ref, cwa_ref, cwb_ref, wdn_ref,
                    y_ref, ua_ref, ub_ref, xn_ref, acc_ref):
    j = pl.program_id(0)

    @pl.when(j == 0)
    def _():
        xn_ref[...] = _rms(x_ref[...], g_ref[...]).astype(BF16)
        acc_ref[...] = jnp.zeros(acc_ref.shape, F32)

    ua = _dot(xn_ref[...], wa_ref[...])
    ub = _dot(xn_ref[...], wb_ref[...])
    ua_ref[...] = ua
    ub_ref[...] = ub
    a = _conv3(cwa_ref[...], h0a_ref[...], h1a_ref[...], ua)
    b = _conv3(cwb_ref[...], h0b_ref[...], h1b_ref[...], ub)
    acc_ref[...] += _dot((a * jax.nn.sigmoid(a) * b).astype(BF16), wdn_ref[...])

    @pl.when(j == pl.num_programs(0) - 1)
    def _():
        y_ref[...] = x_ref[...] + acc_ref[...]


def _conv_ffn_rows(x2d, hist, g, w_up, conv_w, conv_b, w_down):
    b, d = x2d.shape
    d_ff = w_down.shape[0]
    n_j = d_ff // FF_CHUNK
    cw = _conv_table(conv_w, conv_b)
    h0, h1 = hist[:, 0], hist[:, 1]
    a_col = lambda j: (0, j)
    b_col = lambda j: (0, n_j + j)
    row_a, row_b = pl.BlockSpec((b, FF_CHUNK), a_col), pl.BlockSpec((b, FF_CHUNK), b_col)
    y, ua, ub = pl.pallas_call(
        _ffn_row_kernel,
        grid=(n_j,),
        in_specs=[pl.BlockSpec((b, d), lambda j: (0, 0)), row_a, row_b, row_a, row_b,
                  pl.BlockSpec((1, d), lambda j: (0, 0)),
                  pl.BlockSpec((d, FF_CHUNK), a_col), pl.BlockSpec((d, FF_CHUNK), b_col),
                  pl.BlockSpec((SUBLANES, FF_CHUNK), a_col), pl.BlockSpec((SUBLANES, FF_CHUNK), b_col),
                  pl.BlockSpec((FF_CHUNK, d), lambda j: (j, 0))],
        out_specs=[pl.BlockSpec((b, d), lambda j: (0, 0)), row_a, row_a],
        out_shape=[jax.ShapeDtypeStruct((b, d), F32), jax.ShapeDtypeStruct((b, d_ff), F32),
                   jax.ShapeDtypeStruct((b, d_ff), F32)],
        scratch_shapes=[pltpu.VMEM((b, d), BF16), pltpu.VMEM((b, d), F32)],
        compiler_params=_params(("arbitrary",)),
        name="conv_ffn_rows",
    )(x2d, h0, h0, h1, h1, g.reshape(1, d), w_up, w_up, cw, cw, w_down)
    return y, jnp.stack([h1, jnp.concatenate([ua, ub], axis=-1)], axis=1)


def _cols(zh, name, width):
    return zh[..., _NEW[name]:_NEW[name] + width]


def _new_rows(zh):
    b, t, _ = zh.shape
    dsa = jnp.concatenate([_cols(zh, "k_a", 2 * HEAD_DIM), _cols(zh, "ik", HEAD_DIM)], axis=-1)
    nsa = _cols(zh, "kc", 4 * HEAD_DIM)
    moba = _cols(zh, "k_c", 2 * N_HEADS * HEAD_DIM)
    win = _cols(zh, "kw", 2 * HEAD_DIM)
    return (dsa.reshape(b, t, 3, HEAD_DIM), nsa.reshape(b, t, 4, HEAD_DIM),
            moba.reshape(b, t, 2, N_HEADS, HEAD_DIM), win.reshape(b, t, 2, HEAD_DIM))


def _prompt_layer(x, mem, p):
    b, t, d = x.shape
    zh2d, gate = _projections(x.reshape(b * t, d), jnp.arange(t, dtype=jnp.int32), p["ln"][0], p["w_heads"],
                              p["cfg"], p["w_gate"], tm=256)
    zh = zh2d.reshape(b, t, N_HEADCOLS)
    o_a = _dsa_call(zh, zh, zh, kv_blk=BLK_KV_A, ik_blk=BLK_IK, tq=128, n_keys=t)
    cmp = _nsa_compress(zh, p["cmp_w"], blk=BLK_CMP)
    o_b = _nsa_call(zh, cmp, zh, zh, sel_blk=BLK_SEL, win_blk=BLK_WIN, tq=128, n_keys=t)
    o_c = _moba_call(zh, zh, tq=MOBA_BLOCK)
    mkv = _memory_kv(mem, p["ln"][2], p["w_mem_kv"], p["g_mem"][1])
    o_m = _mem_call(zh, mkv, tq=256)
    outs = [o.reshape(b * t, o.shape[-1]) for o in (o_a, o_b, o_c, o_m)]
    x1 = _combine(x.reshape(b * t, d), outs, gate, p["w_branch"], p["w_out"], tm=512).reshape(b, t, d)
    hist = jnp.zeros((b, CONV_WIDTH - 1, p["w_up"].shape[1]), F32)
    y, conv = _conv_ffn(x1, hist, p["ln"][1], p["w_up"], p["conv_w"], p["conv_b"], p["w_down"], tm=1024)
    dsa, nsa, moba, win = _new_rows(zh)
    keep = min(NSA_WINDOW, t)
    return y, dsa, nsa, moba, win[:, t - keep:], mkv.reshape(b, mkv.shape[1], 2, N_HEADS, HEAD_DIM), conv


def _first_row(x, n):
    return jnp.pad(x, ((0, 0), (0, n - 1)) + ((0, 0),) * (x.ndim - 2))


def _sample_layer(x, layer, caches, page_table, win_state, mem_kv, conv_hist, p):
    b, _, d = x.shape
    cache_dsa, cache_nsa, cache_moba = caches
    page = cache_dsa.shape[2]
    past = page_table.shape[1] * page
    x2d = x.reshape(b, d)
    zh, gate = _projections(x2d, jnp.full((b,), past, jnp.int32), p["ln"][0], p["w_heads"], p["cfg"],
                            p["w_gate"], tm=b)
    zq = _first_row(zh[:, None, :], DEC_ROWS)
    dsa, nsa, moba, win = _new_rows(zh[:, None, :])
    o_a = _dsa_decode_call(zq, cache_dsa, layer, page_table, _first_row(dsa.reshape(b, 1, -1), DEC_KEY_CHUNK))
    win_all = jnp.concatenate([win_state.reshape(b, -1, 2 * HEAD_DIM), win.reshape(b, 1, 2 * HEAD_DIM)], axis=1)
    w_pad = -(-win_all.shape[1] // LANES) * LANES
    win_rows = jnp.pad(win_all, ((0, 0), (0, w_pad - win_all.shape[1]), (0, 0)))
    o_b = _nsa_decode_call(zq, cache_nsa, layer, page_table, _first_row(nsa.reshape(b, 1, -1), DEC_KEY_CHUNK),
                           win_rows, p["cmp_w"])
    o_c = _moba_decode_call(zq, cache_moba, layer, page_table, _first_row(moba.reshape(b, 1, -1), page))
    o_m = _mem_call(zq, mem_kv.reshape(b, mem_kv.shape[1], -1), tq=DEC_ROWS)
    outs = [o[:, 0, :] for o in (o_a, o_b, o_c, o_m)]
    x1 = _combine(x2d, outs, gate, p["w_branch"], p["w_out"], tm=b)
    y, conv = _conv_ffn_rows(x1, conv_hist, p["ln"][1], p["w_up"], p["conv_w"], p["conv_b"], p["w_down"])
    keep = win_state.shape[1]
    win_new = win_all[:, win_all.shape[1] - keep:].reshape(b, keep, 2, HEAD_DIM)
    return y.reshape(b, 1, d), dsa, nsa, moba, win_new, conv


def kernel(x_prompt, x_sample, cache_dsa, cache_nsa, cache_moba, state_nsa_win, cache_mem, state_ffn_conv,
           page_table, mem_prompt, ln, w_in, g_dsa, g_nsa, g_moba, g_mem, w_mem_kv, w_cmp1, w_cmp2, pe_cmp,
           w_branch, w_out, w_up, conv_w, conv_b, w_down):
    depth = ln.shape[0]
    caches = tuple(c.reshape(*c.shape[:3], -1) for c in (cache_dsa, cache_nsa, cache_moba))
    xp, xs = x_prompt, x_sample
    outs_p = [[] for _ in range(6)]
    outs_s = [[] for _ in range(5)]
    for l in range(depth):
        w_heads, cfg = _head_weights(w_in[l], g_dsa[l], g_nsa[l], g_moba[l], g_mem[l])
        p = dict(ln=ln[l], w_heads=w_heads, cfg=cfg, w_gate=w_in[l][:, GATE_ORIG:].astype(BF16), g_mem=g_mem[l],
                 w_mem_kv=w_mem_kv[l], cmp_w=_compress_weights(w_cmp1[l], w_cmp2[l], pe_cmp[l]),
                 w_branch=w_branch[l].astype(BF16), w_out=w_out[l].astype(BF16), w_up=w_up[l].astype(BF16),
                 conv_w=conv_w[l], conv_b=conv_b[l], w_down=w_down[l].astype(BF16))
        xp, *rest = _prompt_layer(xp, mem_prompt, p)
        for acc, r in zip(outs_p, rest):
            acc.append(r)
        xs, *rest = _sample_layer(xs, l, caches, page_table, state_nsa_win[l], cache_mem[l], state_ffn_conv[l], p)
        for acc, r in zip(outs_s, rest):
            acc.append(r)
    dsa_p, nsa_p, moba_p, win_p, memkv_p, conv_p = [jnp.stack(a) for a in outs_p]
    dsa_s, nsa_s, moba_s, win_s, conv_s = [jnp.stack(a) for a in outs_s]
    return (xp, xs, dsa_p, dsa_s, nsa_p, nsa_s, moba_p, moba_s, win_p, win_s, memkv_p, conv_p, conv_s)
```

```python
import functools
import math

import numpy as np
import jax
import jax.numpy as jnp
from jax import lax
from jax.experimental import pallas as pl
from jax.experimental.pallas import tpu as pltpu

HEAD_DIM = 64
ROPE_DIM = HEAD_DIM // 4
ROPE_THETA = 500000.0
N_HEADS = 4
DSA_TOPK = 256
NSA_CMP_LEN = 32
NSA_CMP_STRIDE = 16
NSA_SEL_BLOCK = 64
NSA_SEL_TOPN = 16
NSA_WINDOW = 512
MOBA_BLOCK = 256
MOBA_TOPK = 3
N_BRANCH = 4
CONV_WIDTH = 3
RMS_EPS = 1e-6

LANES = 128
SUBLANES = 8
VMEM_LIMIT = 56 * 1024 * 1024
DEC_ROWS = SUBLANES
KEY_CHUNK = 512
DEC_KEY_CHUNK = 2048
PAGES_PER_STEP = 8

F32 = jnp.float32
BF16 = jnp.bfloat16
NEG_INF = float("-inf")
SCALE = HEAD_DIM ** -0.5

KEY_NEG_INF = int(np.uint32(0xFF800000) ^ np.uint32(0x7FFFFFFF)) - 2 ** 32
KEY_POS_INF = 0x7F800000
INT_MIN = -2 ** 31
STRIP = 32
GROUP = 256

_NT = (((1,), (1,)), ((), ()))

_ORIG = dict(q_a=0, k_a=256, v_a=320, iq=384, ik=640, iw=704, q_b=708, kc=964, vc=1028, ks=1092, vs=1156,
             kw=1220, vw=1284, g_b=1348, q_c=1360, k_c=1616, v_c=1872, q_m=2128)
GATE_ORIG = 2384
_NEW = dict(q_a=0, iq=256, q_b=512, q_c=768, q_m=1024, k_c=1280, v_c=1536, k_a=1792, v_a=1856, ik=1920,
            iw=1984, kc=2048, vc=2112, ks=2176, vs=2240, kw=2304, vw=2368, g_b=2432)
_WIDTH = dict(q_a=256, iq=256, q_b=256, q_c=256, q_m=256, k_c=256, v_c=256, k_a=64, v_a=64, ik=64, iw=4,
              kc=64, vc=64, ks=64, vs=64, kw=64, vw=64, g_b=12)
N_HEADCOLS = 2560
_NORMED = ("q_a", "q_b", "q_c", "q_m", "k_c", "k_a", "kc", "ks", "kw")
_ROPED = ("q_a", "iq", "q_b", "q_c", "k_c", "k_a", "ik", "kc", "ks", "kw")
_SIGMOID = ("g_b",)

BLK_KV_A, BLK_IK, BLK_CMP, BLK_SEL, BLK_WIN, BLK_GB = 14, 15, 16, 17, 18, 19
BLK_KC, BLK_VC = 10, 12


def _head_layout():
    masks = {k: np.zeros((N_HEADCOLS,), np.float32) for k in ("norm", "rope", "sig")}
    for name, new in _NEW.items():
        w = _WIDTH[name]
        if name in _NORMED:
            masks["norm"][new:new + w] = 1
        if name in _ROPED:
            masks["rope"][new:new + w] = 1
        if name in _SIGMOID:
            masks["sig"][new:new + w] = 1
    return masks


_COL_MASKS = _head_layout()


def _chunk_any(mask):
    return tuple(bool(mask[c * LANES:(c + 1) * LANES].any()) for c in range(mask.shape[0] // LANES))


def _params(sem):
    return pltpu.CompilerParams(dimension_semantics=sem, vmem_limit_bytes=VMEM_LIMIT)


def _lane_iota(n=LANES):
    return lax.broadcasted_iota(jnp.int32, (1, n), 1)


def _row_iota(n):
    return lax.broadcasted_iota(jnp.int32, (n, 1), 0)


def _split_bf16(x):
    hi = x.astype(BF16)
    lo = (x - hi.astype(F32)).astype(BF16)
    return hi, lo


def _dot(a, b):
    return jnp.dot(a, b, preferred_element_type=F32)


def _dot_nt(a, b):
    return lax.dot_general(a, b, _NT, preferred_element_type=F32)


def _proj_kernel(x_ref, g_ref, w_ref, cfg_ref, rope_ref, o_ref, *, norm_chunks, rope_chunks, sig_chunks):
    x = x_ref[...]
    xn = x * lax.rsqrt(jnp.mean(x * x, axis=-1, keepdims=True) + RMS_EPS) * g_ref[...]
    z = _dot(xn.astype(BF16), w_ref[...])
    rr = lax.broadcasted_iota(jnp.int32, (LANES, LANES), 0) // HEAD_DIM
    cc = lax.broadcasted_iota(jnp.int32, (LANES, LANES), 1) // HEAD_DIM
    seg = (rr == cc).astype(BF16)
    for c in range(z.shape[1] // LANES):
        sl = slice(c * LANES, (c + 1) * LANES)
        zc = z[:, sl]
        if norm_chunks[c]:
            hi, lo = _split_bf16(zc * zc)
            ss = _dot(hi, seg) + _dot(lo, seg)
            r = lax.rsqrt(ss * (1.0 / HEAD_DIM) + RMS_EPS)
            zc = jnp.where(cfg_ref[0:1, sl] > 0, zc * r * cfg_ref[1:2, sl], zc)
        if rope_chunks[c]:
            rot = (zc * rope_ref[:, 0:LANES]
                   + pltpu.roll(zc, LANES - ROPE_DIM // 2, 1) * rope_ref[:, LANES:2 * LANES]
                   + pltpu.roll(zc, ROPE_DIM // 2, 1) * rope_ref[:, 2 * LANES:3 * LANES])
            zc = jnp.where(cfg_ref[2:3, sl] > 0, rot, zc)
        if sig_chunks[c]:
            zc = jnp.where(cfg_ref[3:4, sl] > 0, jax.nn.sigmoid(zc), zc)
        o_ref[:, sl] = zc


def _project(x2d, g, w_bf16, cfg, rope, masks, *, tm, tn):
    m, d = x2d.shape
    n = w_bf16.shape[1]
    tn = min(tn, n)
    tm = min(tm, m)
    flags = {k: _chunk_any(v) for k, v in masks.items()}
    per_tile = tn // LANES
    for k, v in flags.items():
        assert all(v[t * per_tile:(t + 1) * per_tile] == v[:per_tile] for t in range(n // tn)), k
    rope_tiles = rope.shape[0] // tm
    kern = functools.partial(_proj_kernel, norm_chunks=flags["norm"][:per_tile],
                             rope_chunks=flags["rope"][:per_tile], sig_chunks=flags["sig"][:per_tile])
    return pl.pallas_call(
        kern,
        grid=(m // tm, n // tn),
        in_specs=[pl.BlockSpec((tm, d), lambda i, j: (i, 0)),
                  pl.BlockSpec((1, d), lambda i, j: (0, 0)),
                  pl.BlockSpec((d, tn), lambda i, j: (0, j)),
                  pl.BlockSpec((SUBLANES, tn), lambda i, j: (0, j)),
                  pl.BlockSpec((tm, 3 * LANES), lambda i, j: (i % rope_tiles, 0))],
        out_specs=pl.BlockSpec((tm, tn), lambda i, j: (i, j)),
        out_shape=jax.ShapeDtypeStruct((m, n), F32),
        compiler_params=_params(("parallel", "arbitrary")),
        name="project",
    )(x2d, g, w_bf16, cfg, rope)


def _rope_table(pos):
    half = ROPE_DIM // 2
    inv_freq = ROPE_THETA ** (-jnp.arange(half, dtype=F32) / half)
    ang = pos.astype(F32)[:, None] * inv_freq[None, :]
    cos, sin = jnp.cos(ang), jnp.sin(ang)
    t = pos.shape[0]
    ones = jnp.ones((t, HEAD_DIM - ROPE_DIM), F32)
    zeros = jnp.zeros((t, HEAD_DIM - ROPE_DIM), F32)
    zh = jnp.zeros((t, half), F32)
    c64 = jnp.concatenate([cos, cos, ones], axis=1)
    s1 = jnp.concatenate([-sin, zh, zeros], axis=1)
    s2 = jnp.concatenate([zh, sin, zeros], axis=1)
    return jnp.concatenate([c64, c64, s1, s1, s2, s2], axis=1)


def _head_weights(w_in_l, g_dsa, g_nsa, g_moba, g_mem):
    d = w_in_l.shape[0]
    pieces, at = [], 0
    for name, new in sorted(_NEW.items(), key=lambda kv: kv[1]):
        if new > at:
            pieces.append(jnp.zeros((d, new - at), w_in_l.dtype))
        pieces.append(w_in_l[:, _ORIG[name]:_ORIG[name] + _WIDTH[name]])
        at = new + _WIDTH[name]
    pieces.append(jnp.zeros((d, N_HEADCOLS - at), w_in_l.dtype))
    w = jnp.concatenate(pieces, axis=1).astype(BF16)
    gain = jnp.ones((N_HEADCOLS,), F32)
    for name, gvec in (("q_a", g_dsa[0]), ("k_a", g_dsa[1]), ("q_b", g_nsa[0]), ("kc", g_nsa[1]),
                       ("ks", g_nsa[2]), ("kw", g_nsa[3]), ("q_c", g_moba[0]), ("k_c", g_moba[1]),
                       ("q_m", g_mem[0])):
        reps = _WIDTH[name] // HEAD_DIM
        gain = lax.dynamic_update_slice(gain, jnp.tile(gvec.astype(F32), reps), (_NEW[name],))
    zero = jnp.zeros((N_HEADCOLS,), F32)
    cfg = jnp.stack([jnp.asarray(_COL_MASKS["norm"]), gain, jnp.asarray(_COL_MASKS["rope"]),
                     jnp.asarray(_COL_MASKS["sig"]), zero, zero, zero, zero])
    return w, cfg


def _gate_cfg(n):
    z = jnp.zeros((n,), F32)
    o = jnp.ones((n,), F32)
    return jnp.stack([z, o, z, o, z, z, z, z])


def _projections(x2d, pos, ln0, w_heads, cfg, w_gate, *, tm):
    d = x2d.shape[1]
    rope = _rope_table(pos)
    g = ln0.reshape(1, d)
    zh = _project(x2d, g, w_heads, cfg, rope, _COL_MASKS, tm=tm, tn=N_HEADCOLS)
    n_g = w_gate.shape[1]
    gmask = dict(norm=np.zeros((n_g,), np.float32), rope=np.zeros((n_g,), np.float32),
                 sig=np.ones((n_g,), np.float32))
    gate = _project(x2d, g, w_gate, _gate_cfg(n_g), rope, gmask, tm=tm, tn=1024)
    return zh, gate


def _stack_heads(x256):
    lane = _lane_iota()
    parts = []
    for c in range(2):
        ch = x256[:, c * LANES:(c + 1) * LANES]
        parts.append(jnp.where(lane < HEAD_DIM, ch, 0.0))
        parts.append(jnp.where(lane < HEAD_DIM, pltpu.roll(ch, HEAD_DIM, 1), 0.0))
    return jnp.concatenate(parts, axis=0)


def _unstack_heads_t(o_t, tq):
    o = o_t.T
    lane = _lane_iota()
    chunks = []
    for c in range(2):
        even, odd = o[2 * c * tq:(2 * c + 1) * tq], o[(2 * c + 1) * tq:(2 * c + 2) * tq]
        chunks.append(jnp.where(lane < HEAD_DIM, pltpu.roll(even, HEAD_DIM, 1), odd))
    return jnp.concatenate(chunks, axis=1)


def _pair_heads(q128):
    lane = _lane_iota()
    return jnp.concatenate([jnp.where(lane < HEAD_DIM, q128, 0.0), jnp.where(lane >= HEAD_DIM, q128, 0.0)], axis=0)


def _rows_t(x):
    tq = x.shape[0]
    if tq < LANES:
        x = jnp.concatenate([x, jnp.zeros((LANES - tq, x.shape[1]), x.dtype)], axis=0)
    return x.T[:, 0:tq]


def _tile_lanes(x, n):
    return jnp.concatenate([x] * n, axis=1)


def _tsoftmax_init(m_ref, l_ref, acc_ref):
    m_ref[...] = jnp.full(m_ref.shape, NEG_INF, F32)
    l_ref[...] = jnp.zeros(l_ref.shape, F32)
    acc_ref[...] = jnp.zeros(acc_ref.shape, F32)


def _attend_loop(n_chunks, score_fn, mask_fn, load_vt, s_ref, p_ref, m_ref, l_ref, acc_ref, *, reps):
    kc, r_cols = p_ref.shape
    w = r_cols // reps
    _tsoftmax_init(m_ref, l_ref, acc_ref)

    def scores_into(half, c):
        cc = jnp.minimum(c, n_chunks - 1)
        for g0 in range(0, kc, GROUP):
            s_ref[half * kc + g0:half * kc + g0 + GROUP, :] = score_fn(cc, g0, GROUP)

    def consume(half, c):
        live = c < n_chunks
        cc = jnp.minimum(c, n_chunks - 1)
        strip_mask = mask_fn(cc)
        mx = jnp.full((STRIP, r_cols), NEG_INF, F32)
        for r0 in range(0, kc, STRIP):
            at = slice(half * kc + r0, half * kc + r0 + STRIP)
            s = s_ref[at, :]
            ok = strip_mask(r0, STRIP) & live
            s = jnp.concatenate([jnp.where(ok, s[:, k * w:(k + 1) * w], NEG_INF) for k in range(reps)], axis=1)
            s_ref[at, :] = s
            mx = jnp.maximum(mx, s)
        m_old = m_ref[...]
        m_new = jnp.maximum(m_old, jnp.max(mx, axis=0, keepdims=True))
        m_safe = jnp.where(m_new == NEG_INF, 0.0, m_new)
        tot = jnp.zeros((STRIP, r_cols), F32)
        for r0 in range(0, kc, STRIP):
            p = jnp.exp(s_ref[half * kc + r0:half * kc + r0 + STRIP, :] - m_safe)
            tot = tot + p
            p_ref[r0:r0 + STRIP, :] = p.astype(BF16)
        alpha = jnp.exp(m_old - m_safe)
        l_ref[...] = alpha * l_ref[...] + jnp.sum(tot, axis=0, keepdims=True)
        acc_ref[...] = alpha * acc_ref[...] + _dot(load_vt(cc), p_ref[...])
        m_ref[...] = m_new

    scores_into(0, 0)

    def body(pair, carry):
        scores_into(1, 2 * pair + 1)
        consume(0, 2 * pair)
        scores_into(0, 2 * pair + 2)
        consume(1, 2 * pair + 1)
        return carry

    lax.fori_loop(0, (n_chunks + 1) // 2, body, 0)


def _tsoftmax_result(l_ref, acc_ref):
    return acc_ref[...] / jnp.maximum(l_ref[...], 1e-30)


def _tmasked_softmax(s_t, valid_t):
    s = jnp.where(valid_t, s_t, NEG_INF)
    m = jnp.max(s, axis=0, keepdims=True)
    m = jnp.where(m == NEG_INF, 0.0, m)
    e = jnp.exp(s - m)
    return e / jnp.maximum(jnp.sum(e, axis=0, keepdims=True), 1e-30)


def _attend_scratch(d, r, kc):
    return [pltpu.VMEM((2 * kc, r), F32), pltpu.VMEM((kc, r), BF16),
            pltpu.VMEM((1, r), F32), pltpu.VMEM((1, r), F32), pltpu.VMEM((d, r), F32)]


def _chunk_loader(ref, kc):
    def load(c, r0=0, n=kc):
        rows = pl.ds(pl.multiple_of(c * kc + r0, math.gcd(kc, n)), n)
        x = ref[0, rows, :] if len(ref.shape) == 3 else ref[rows, :]
        return x.astype(BF16)
    return load


def _fill_transposed(src_ref, dst_ref, kc):
    def body(c, carry):
        dst_ref[c] = src_ref[0, pl.ds(pl.multiple_of(c * kc, kc), kc), :].T.astype(BF16)
        return carry
    lax.fori_loop(0, dst_ref.shape[0], body, 0)


def _rank_rows(val_t, src_ref, n_rows):
    src_ref[...] = val_t
    j = _row_iota(val_t.shape[0])

    def body(i, rank):
        row = src_ref[pl.ds(i, 1), :]
        beats = (row > val_t) | ((row == val_t) & (i < j))
        return rank + jnp.where(beats, 1, 0)

    return lax.fori_loop(0, n_rows, body, jnp.zeros(val_t.shape, jnp.int32))


def _select_top_t(val_t, rank_ref, n_valid, n_sel):
    rank = _rank_rows(val_t, rank_ref, n_valid)
    return jnp.where((rank < n_sel) & (val_t > NEG_INF), 1.0, 0.0)


def _dsa_core(q, iq, iw, load_kv, load_ik, load_kvt, key_ref, s_ref, p_ref, m_ref, l_ref, acc_ref,
              *, tq, kc, n_top, p0, idx_bits):
    n_chunks = (p0 + tq - 1) // kc + 1
    pos = p0 + _lane_iota(tq)
    iqs_t = _stack_heads(iq).T.astype(BF16)
    qs_t = (_stack_heads(q) * SCALE).T.astype(BF16)
    iw_t = _rows_t(iw)
    w_rows = [iw_t[HEAD_DIM + h:HEAD_DIM + h + 1, :] for h in range(N_HEADS)]

    def chunk_pos(c):
        return c * kc + _row_iota(kc)

    def score_chunk(c, carry):
        lg = _dot(load_ik(c), iqs_t)
        sc = w_rows[0] * jnp.maximum(lg[:, 0:tq], 0.0)
        for h in range(1, N_HEADS):
            sc = sc + w_rows[h] * jnp.maximum(lg[:, h * tq:(h + 1) * tq], 0.0)
        sc = jnp.where(chunk_pos(c) <= pos, sc, NEG_INF)
        bits = pltpu.bitcast(sc, jnp.int32)
        key_ref[c] = bits ^ ((bits >> 31) & 0x7FFFFFFF)
        return carry

    lax.fori_loop(0, n_chunks, score_chunk, 0)

    def count(pred):
        def body(c, acc):
            hit = jnp.where(pred(key_ref[c], chunk_pos(c)), 1, 0)
            parts = [hit[r * SUBLANES:(r + 1) * SUBLANES] for r in range(kc // SUBLANES)]
            while len(parts) > 1:
                parts = [a + b for a, b in zip(parts[0::2], parts[1::2])]
            return acc + parts[0]
        acc = lax.fori_loop(0, n_chunks, body, jnp.zeros((SUBLANES, tq), jnp.int32))
        return jnp.sum(acc, axis=0, keepdims=True)

    def thr_bit(b, thr):
        cand = thr + jnp.left_shift(jnp.int32(1), 31 - b)
        cnt = count(lambda key, kpos: key >= cand)
        return jnp.where(cnt >= n_top, cand, thr)

    thr = lax.fori_loop(0, 32, thr_bit, jnp.full((1, tq), INT_MIN, jnp.int32))

    def last_tied():
        need = n_top - count(lambda key, kpos: key > thr)

        def idx_bit(b, last):
            cand = last + jnp.left_shift(jnp.int32(1), idx_bits - 1 - b)
            cnt = count(lambda key, kpos: (key == thr) & (kpos < cand))
            return jnp.where(cnt < need, cand, last)

        return lax.fori_loop(0, idx_bits, idx_bit, jnp.zeros((1, tq), jnp.int32))

    over = (count(lambda key, kpos: key >= thr) > n_top) & (thr > KEY_NEG_INF)
    last = lax.cond(jnp.max(jnp.where(over, 1, 0)) > 0, last_tied,
                    lambda: jnp.full((1, tq), 2 ** idx_bits, jnp.int32))
    last = jnp.where(thr > KEY_NEG_INF, last, -1)

    def selected(c):
        def strip(r0, rows):
            key = key_ref[c, r0:r0 + rows, :]
            kpos = c * kc + r0 + _row_iota(rows)
            return ((key > thr) | ((key == thr) & (kpos <= last))) & (key < KEY_POS_INF)
        return strip

    _attend_loop(n_chunks, lambda c, r0, n: _dot(load_kv(c, r0, n), qs_t), selected, load_kvt,
                 s_ref, p_ref, m_ref, l_ref, acc_ref, reps=N_HEADS)
    return _unstack_heads_t(_tsoftmax_result(l_ref, acc_ref), tq)


def _dsa_kernel(q_ref, iq_ref, iw_ref, kv_ref, ik_ref, o_ref, kvt_ref, key_ref, s_ref, p_ref, m_ref, l_ref, acc_ref,
                *, tq, kc, n_top, idx_bits):
    i = pl.program_id(1)

    @pl.when(i == 0)
    def _():
        _fill_transposed(kv_ref, kvt_ref, kc)

    o_ref[0] = _dsa_core(q_ref[0], iq_ref[0], iw_ref[0], _chunk_loader(kv_ref, kc), _chunk_loader(ik_ref, kc),
                         lambda c: kvt_ref[c], key_ref, s_ref, p_ref, m_ref, l_ref, acc_ref,
                         tq=tq, kc=kc, n_top=n_top, p0=i * tq, idx_bits=idx_bits)


def _row_softmax_step(rows, s, valid, v_bf16, m_ref, l_ref, acc_ref):
    s = jnp.where(valid, s, NEG_INF)
    m_old = m_ref[rows, :]
    m_new = jnp.maximum(m_old, jnp.max(s, axis=-1, keepdims=True))
    m_safe = jnp.where(m_new == NEG_INF, 0.0, m_new)
    p = jnp.exp(s - m_safe)
    alpha = jnp.exp(m_old - m_safe)
    l_ref[rows, :] = alpha * l_ref[rows, :] + jnp.sum(p, axis=-1, keepdims=True)
    acc_ref[rows, :] = alpha * acc_ref[rows, :] + _dot(p.astype(BF16), v_bf16)
    m_ref[rows, :] = m_new


def _row_softmax_init(m_ref, l_ref, acc_ref):
    m_ref[...] = jnp.full(m_ref.shape, NEG_INF, F32)
    l_ref[...] = jnp.zeros(l_ref.shape, F32)
    acc_ref[...] = jnp.zeros(acc_ref.shape, F32)


def _row_softmax_result(rows, l_ref, acc_ref):
    return acc_ref[rows, :] / jnp.maximum(l_ref[rows, :], 1e-30)


def _row_masked_softmax(s, valid):
    s = jnp.where(valid, s, NEG_INF)
    m = jnp.max(s, axis=-1, keepdims=True)
    m = jnp.where(m == NEG_INF, 0.0, m)
    e = jnp.exp(s - m)
    return e / jnp.maximum(jnp.sum(e, axis=-1, keepdims=True), 1e-30)


def _row_softmax_scratch(rows, d):
    return [pltpu.VMEM((rows, 1), F32), pltpu.VMEM((rows, 1), F32), pltpu.VMEM((rows, d), F32)]


def _unstack_heads(o_heads):
    lane = _lane_iota()
    chunks = []
    for c in range(2):
        chunks.append(jnp.where(lane < HEAD_DIM, pltpu.roll(o_heads[2 * c], HEAD_DIM, 1), o_heads[2 * c + 1]))
    return jnp.concatenate(chunks, axis=1)


def _dsa_rows_core(q, iq, iw, load_kv, load_ik, key_ref, m_ref, l_ref, acc_ref, *, tq, kc, n_top, p0, idx_bits):
    n_chunks = (p0 + tq - 1) // kc + 1
    pos = p0 + _row_iota(tq)
    iqs = _stack_heads(iq).astype(BF16)
    qs = (_stack_heads(q) * SCALE).astype(BF16)
    w_cols = [iw[:, HEAD_DIM + h:HEAD_DIM + h + 1] for h in range(N_HEADS)]

    def chunk_pos(c):
        return c * kc + _lane_iota(kc)

    def score_chunk(c, carry):
        lg = _dot_nt(iqs, load_ik(c))
        sc = w_cols[0] * jnp.maximum(lg[0:tq], 0.0)
        for h in range(1, N_HEADS):
            sc = sc + w_cols[h] * jnp.maximum(lg[h * tq:(h + 1) * tq], 0.0)
        sc = jnp.where(chunk_pos(c) <= pos, sc, NEG_INF)
        bits = pltpu.bitcast(sc, jnp.int32)
        key_ref[c] = bits ^ ((bits >> 31) & 0x7FFFFFFF)
        return carry

    lax.fori_loop(0, n_chunks, score_chunk, 0)

    def count(pred):
        def body(c, acc):
            hit = jnp.where(pred(key_ref[c], chunk_pos(c)), 1, 0)
            parts = [hit[:, t * LANES:(t + 1) * LANES] for t in range(kc // LANES)]
            while len(parts) > 1:
                parts = [a + b for a, b in zip(parts[0::2], parts[1::2])]
            return acc + parts[0]
        acc = lax.fori_loop(0, n_chunks, body, jnp.zeros((tq, LANES), jnp.int32))
        return jnp.sum(acc, axis=1, keepdims=True)

    def thr_bit(b, thr):
        cand = thr + jnp.left_shift(jnp.int32(1), 31 - b)
        cnt = count(lambda key, kpos: key >= cand)
        return jnp.where(cnt >= n_top, cand, thr)

    thr = lax.fori_loop(0, 32, thr_bit, jnp.full((tq, 1), INT_MIN, jnp.int32))

    def last_tied():
        need = n_top - count(lambda key, kpos: key > thr)

        def idx_bit(b, last):
            cand = last + jnp.left_shift(jnp.int32(1), idx_bits - 1 - b)
            cnt = count(lambda key, kpos: (key == thr) & (kpos < cand))
            return jnp.where(cnt < need, cand, last)

        return lax.fori_loop(0, idx_bits, idx_bit, jnp.zeros((tq, 1), jnp.int32))

    over = (count(lambda key, kpos: key >= thr) > n_top) & (thr > KEY_NEG_INF)
    last = lax.cond(jnp.max(jnp.where(over, 1, 0)) > 0, last_tied,
                    lambda: jnp.full((tq, 1), 2 ** idx_bits, jnp.int32))

    _row_softmax_init(m_ref, l_ref, acc_ref)

    def attend(c, carry):
        kvc = load_kv(c)
        s = _dot_nt(qs, kvc)
        key = key_ref[c]
        sel = (key > thr) | ((key == thr) & (chunk_pos(c) <= last))
        sel = sel & (key > KEY_NEG_INF) & (key < KEY_POS_INF)
        for h in range(N_HEADS):
            _row_softmax_step(pl.ds(h * tq, tq), s[h * tq:(h + 1) * tq], sel, kvc, m_ref, l_ref, acc_ref)
        return carry

    lax.fori_loop(0, n_chunks, attend, 0)
    return _unstack_heads([_row_softmax_result(pl.ds(h * tq, tq), l_ref, acc_ref) for h in range(N_HEADS)])


def _dsa_decode_kernel(pt_ref, q_ref, iq_ref, iw_ref, *rest, tq, kc, n_top, idx_bits, n_pages, page, pps):
    page_refs, (new_ref, o_ref, kv_ref, ik_ref, key_ref, m_ref, l_ref, acc_ref) = rest[:pps], rest[pps:]
    s = pl.program_id(1)
    n_steps = n_pages // pps

    def put(at, rows):
        kv_ref[at, :] = rows[:, 0:LANES].astype(BF16)
        ik = rows[:, LANES:]
        ik_ref[at, :] = jnp.concatenate([ik, jnp.zeros((rows.shape[0], 2 * LANES - rows.shape[1]), rows.dtype)],
                                        axis=1).astype(BF16)

    @pl.when(s < n_steps)
    def _():
        for r in range(pps):
            put(pl.ds(pl.multiple_of((s * pps + r) * page, page), page), page_refs[r][0, 0])

    @pl.when(s == n_steps)
    def _():
        put(pl.ds(n_pages * page, kc), new_ref[0])
        o_ref[0] = _dsa_rows_core(q_ref[0], iq_ref[0], iw_ref[0], _chunk_loader(kv_ref, kc), _chunk_loader(ik_ref, kc),
                                  key_ref, m_ref, l_ref, acc_ref, tq=tq, kc=kc, n_top=n_top, p0=n_pages * page,
                                  idx_bits=idx_bits)


def _dsa_scratch(n_kc, tq, kc):
    return ([pltpu.VMEM((n_kc, LANES, kc), BF16), pltpu.VMEM((n_kc, kc, tq), jnp.int32)]
            + _attend_scratch(LANES, N_HEADS * tq, kc))


def _page_specs(cache, layer, n_pages, pps):
    block = (1, 1) + cache.shape[2:]
    zeros = (0,) * (len(cache.shape) - 2)

    def spec(r):
        return pl.BlockSpec(block, lambda bi, s, pt: (layer, pt[bi, jnp.minimum(s * pps + r, n_pages - 1)]) + zeros)

    return [spec(r) for r in range(pps)]


def _dsa_decode_call(zq, cache, layer, page_table, new_rows, *, kc=DEC_KEY_CHUNK, pps=PAGES_PER_STEP):
    b, tq, _ = zq.shape
    n_pages = page_table.shape[1]
    page = cache.shape[2]
    past = n_pages * page
    assert past % kc == 0 and n_pages % pps == 0
    l_rows = past + kc
    n_kc = l_rows // kc
    kern = functools.partial(_dsa_decode_kernel, tq=tq, kc=kc, n_top=min(DSA_TOPK, (past + 1) // 4),
                             idx_bits=max(1, math.ceil(math.log2(l_rows))), n_pages=n_pages, page=page, pps=pps)
    grid_spec = pltpu.PrefetchScalarGridSpec(
        num_scalar_prefetch=1,
        grid=(b, n_pages // pps + 1),
        in_specs=[pl.BlockSpec((1, tq, 2 * LANES), lambda bi, s, pt: (bi, 0, _NEW["q_a"] // 256)),
                  pl.BlockSpec((1, tq, 2 * LANES), lambda bi, s, pt: (bi, 0, _NEW["iq"] // 256)),
                  pl.BlockSpec((1, tq, LANES), lambda bi, s, pt: (bi, 0, BLK_IK))]
        + _page_specs(cache, layer, n_pages, pps)
        + [pl.BlockSpec((1,) + new_rows.shape[1:], lambda bi, s, pt: (bi, 0, 0))],
        out_specs=pl.BlockSpec((1, tq, 2 * LANES), lambda bi, s, pt: (bi, 0, 0)),
        scratch_shapes=[pltpu.VMEM((l_rows, LANES), BF16), pltpu.VMEM((l_rows, LANES), BF16),
                        pltpu.VMEM((n_kc, tq, kc), jnp.int32)] + _row_softmax_scratch(N_HEADS * tq, LANES))
    return pl.pallas_call(
        kern, grid_spec=grid_spec,
        out_shape=jax.ShapeDtypeStruct((b, tq, 2 * LANES), F32),
        compiler_params=_params(("parallel", "arbitrary")),
        name="dsa_decode",
    )(page_table, zq, zq, zq, *([cache] * pps), new_rows)


def _dsa_call(zq, kv, ik, *, kv_blk, ik_blk, tq, n_keys, kc=KEY_CHUNK):
    b, t_q, _ = zq.shape
    l_rows = kv.shape[1]
    n_kc = l_rows // kc
    kern = functools.partial(_dsa_kernel, tq=tq, kc=kc, n_top=min(DSA_TOPK, n_keys // 4),
                             idx_bits=max(1, math.ceil(math.log2(l_rows))))
    return pl.pallas_call(
        kern,
        grid=(b, t_q // tq),
        in_specs=[pl.BlockSpec((1, tq, 2 * LANES), lambda bi, i: (bi, i, _NEW["q_a"] // 256)),
                  pl.BlockSpec((1, tq, 2 * LANES), lambda bi, i: (bi, i, _NEW["iq"] // 256)),
                  pl.BlockSpec((1, tq, LANES), lambda bi, i: (bi, i, BLK_IK)),
                  pl.BlockSpec((1, l_rows, LANES), lambda bi, i: (bi, 0, kv_blk)),
                  pl.BlockSpec((1, l_rows, LANES), lambda bi, i: (bi, 0, ik_blk))],
        out_specs=pl.BlockSpec((1, tq, 2 * LANES), lambda bi, i: (bi, i, 0)),
        out_shape=jax.ShapeDtypeStruct((b, t_q, 2 * LANES), F32),
        scratch_shapes=_dsa_scratch(n_kc, tq, kc),
        compiler_params=_params(("parallel", "arbitrary")),
        name="dsa_attention",
    )(zq, zq, zq, kv, ik)


def _compress(load_rows, wlo_ref, whi_ref, pe_ref, w2_ref, n_blocks):
    a = jnp.zeros((n_blocks, LANES), F32)
    b = jnp.zeros((n_blocks, LANES), F32)
    for p in range(NSA_CMP_STRIDE):
        xp = load_rows(p)
        a = a + _dot((xp + pe_ref[p:p + 1, :]).astype(BF16), wlo_ref[p])
        b = b + _dot((xp + pe_ref[NSA_CMP_STRIDE + p:NSA_CMP_STRIDE + p + 1, :]).astype(BF16), whi_ref[p])
    h = jax.nn.gelu(a + pltpu.roll(b, n_blocks - 1, 0))
    return _dot(h.astype(BF16), w2_ref[...])


def _cmp_kernel(rows_ref, wlo_ref, whi_ref, pe_ref, w2_ref, o_ref, *, n_blocks):
    o_ref[0] = _compress(lambda p: rows_ref[0, pl.ds(p, n_blocks, stride=NSA_CMP_STRIDE), :],
                         wlo_ref, whi_ref, pe_ref, w2_ref, n_blocks)


def _block_diag2(m0, m1):
    z = jnp.zeros_like(m0)
    return jnp.concatenate([jnp.concatenate([m0, z], axis=-1), jnp.concatenate([z, m1], axis=-1)], axis=-2)


def _compress_weights(w_cmp1, w_cmp2, pe_cmp):
    w1 = w_cmp1.reshape(2, NSA_CMP_LEN, HEAD_DIM, HEAD_DIM)
    wlo = _block_diag2(w1[0, :NSA_CMP_STRIDE], w1[1, :NSA_CMP_STRIDE]).astype(BF16)
    whi = _block_diag2(w1[0, NSA_CMP_STRIDE:], w1[1, NSA_CMP_STRIDE:]).astype(BF16)
    pe = jnp.concatenate([pe_cmp[0], pe_cmp[1]], axis=-1)
    w2 = _block_diag2(w_cmp2[0], w_cmp2[1]).astype(BF16)
    return wlo, whi, pe, w2


def _compress_specs(index_map3, index_map2):
    return [pl.BlockSpec((NSA_CMP_STRIDE, LANES, LANES), index_map3),
            pl.BlockSpec((NSA_CMP_STRIDE, LANES, LANES), index_map3),
            pl.BlockSpec((NSA_CMP_LEN, LANES), index_map2),
            pl.BlockSpec((LANES, LANES), index_map2)]


def _nsa_compress(rows, cmp_w, *, blk):
    b, l_rows, _ = rows.shape
    n_blocks = l_rows // NSA_CMP_STRIDE
    return pl.pallas_call(
        functools.partial(_cmp_kernel, n_blocks=n_blocks),
        grid=(b,),
        in_specs=[pl.BlockSpec((1, l_rows, LANES), lambda bi: (bi, 0, blk))]
        + _compress_specs(lambda bi: (0, 0, 0), lambda bi: (0, 0)),
        out_specs=pl.BlockSpec((1, n_blocks, LANES), lambda bi: (bi, 0, 0)),
        out_shape=jax.ShapeDtypeStruct((b, n_blocks, LANES), F32),
        compiler_params=_params(("parallel",)),
        name="nsa_compress",
    )(rows, *cmp_w)


def _nsa_core(q, g, cmp_rows, load_sel, load_selt, wrows, wpos, rank_ref, s_ref, p_ref, m_ref, l_ref, acc_ref,
              *, tq, kc, p0, n_cmp, n_blk, n_sel):
    pos = p0 + _lane_iota(tq)
    qs_t = (_stack_heads(q) * SCALE).T.astype(BF16)

    ncp = cmp_rows.shape[0]
    cmp_t = cmp_rows.T.astype(BF16)
    s_c = _dot(cmp_rows.astype(BF16), qs_t)
    n_idx = _row_iota(ncp)
    valid_c = (n_idx * NSA_CMP_STRIDE + (NSA_CMP_LEN - 1) <= pos) & (n_idx < n_cmp)
    p_c = [_tmasked_softmax(s_c[:, h * tq:(h + 1) * tq], valid_c) for h in range(N_HEADS)]
    o_c = _dot(cmp_t, jnp.concatenate(p_c, axis=1).astype(BF16))
    p_sum = p_c[0] + p_c[1] + p_c[2] + p_c[3]

    nbp = rank_ref.shape[0]
    jb = lax.broadcasted_iota(jnp.int32, (nbp, ncp), 0)
    nn = lax.broadcasted_iota(jnp.int32, (nbp, ncp), 1)
    cover_t = ((nn * NSA_CMP_STRIDE < (jb + 1) * NSA_SEL_BLOCK)
               & (nn * NSA_CMP_STRIDE + (NSA_CMP_LEN - 1) >= jb * NSA_SEL_BLOCK) & (nn < n_cmp)).astype(BF16)
    hi, lo = _split_bf16(p_sum)
    imp = _dot(cover_t, hi) + _dot(cover_t, lo)
    j = _row_iota(nbp)
    cur = pos // NSA_SEL_BLOCK
    forced = (j == 0) | (j == cur) | (j == cur - 1)
    imp = jnp.where(forced, jnp.inf, jnp.where(j <= cur, imp, NEG_INF))
    n_vis = jnp.minimum((p0 + tq - 1) // NSA_SEL_BLOCK + 1, n_blk)
    sel = _select_top_t(imp, rank_ref, n_vis, n_sel).astype(BF16)

    n_chunks = (p0 + tq - 1) // kc + 1
    ek = lax.broadcasted_iota(jnp.int32, (kc, nbp), 0)
    eb = lax.broadcasted_iota(jnp.int32, (kc, nbp), 1)

    def selected(c):
        expand = (eb == (c * kc + ek) // NSA_SEL_BLOCK).astype(BF16)
        picked = _dot(expand, sel)

        def strip(r0, rows):
            return (picked[r0:r0 + rows] > 0.5) & (c * kc + r0 + _row_iota(rows) <= pos)
        return strip

    _attend_loop(n_chunks, lambda c, r0, n: _dot(load_sel(c, r0, n), qs_t), selected, load_selt,
                 s_ref, p_ref, m_ref, l_ref, acc_ref, reps=N_HEADS)
    o_s = _tsoftmax_result(l_ref, acc_ref)

    s_w = _dot(wrows.astype(BF16), qs_t)
    valid_w = (wpos <= pos) & (pos - wpos < NSA_WINDOW)
    p_w = [_tmasked_softmax(s_w[:, h * tq:(h + 1) * tq], valid_w) for h in range(N_HEADS)]
    o_w = _dot(wrows.T.astype(BF16), jnp.concatenate(p_w, axis=1).astype(BF16))

    g_t = _rows_t(g)
    mixed = []
    for h in range(N_HEADS):
        cols = slice(h * tq, (h + 1) * tq)
        mixed.append(g_t[3 * h:3 * h + 1] * o_c[:, cols] + g_t[3 * h + 1:3 * h + 2] * o_s[:, cols]
                     + g_t[3 * h + 2:3 * h + 3] * o_w[:, cols])
    return _unstack_heads_t(jnp.concatenate(mixed, axis=1), tq)


def _nsa_kernel(q_ref, g_ref, cmp_ref, sel_ref, win_ref, o_ref, selt_ref, rank_ref, s_ref, p_ref, m_ref, l_ref, acc_ref,
                *, tq, kc, n_cmp, n_blk, n_sel, win_rows):
    i = pl.program_id(1)

    @pl.when(i == 0)
    def _():
        _fill_transposed(sel_ref, selt_ref, kc)

    p0 = i * tq
    start = pl.multiple_of(jnp.clip(p0 - NSA_WINDOW, 0, win_ref.shape[1] - win_rows), SUBLANES)
    o_ref[0] = _nsa_core(q_ref[0], g_ref[0], cmp_ref[0], _chunk_loader(sel_ref, kc), lambda c: selt_ref[c],
                         win_ref[0, pl.ds(start, win_rows), :], start + _row_iota(win_rows),
                         rank_ref, s_ref, p_ref, m_ref, l_ref, acc_ref,
                         tq=tq, kc=kc, p0=p0, n_cmp=n_cmp, n_blk=n_blk, n_sel=n_sel)


def _top_mask_row(v, n_sel):
    n = v.shape[1]
    i = lax.broadcasted_iota(jnp.int32, (n, n), 0)
    j = lax.broadcasted_iota(jnp.int32, (n, n), 1)
    v_col = jnp.sum(jnp.where(i == j, v, 0.0), axis=1, keepdims=True)
    beats = (v_col > v) | ((v_col == v) & (i < j))
    rank = jnp.sum(jnp.where(beats, 1, 0), axis=0, keepdims=True)
    return jnp.where((rank < n_sel) & (v > NEG_INF), 1.0, 0.0)


def _nsa_rows_core(q, g, cmpv, load_sel, wrows, wpos, m_ref, l_ref, acc_ref, *, tq, kc, p0, n_cmp, n_blk, n_sel):
    pos = p0 + _row_iota(tq)
    qs = (_stack_heads(q) * SCALE).astype(BF16)

    ncp = cmpv.shape[0]
    s_c = _dot_nt(qs, cmpv)
    n_idx = _lane_iota(ncp)
    valid_c = (n_idx * NSA_CMP_STRIDE + (NSA_CMP_LEN - 1) <= pos) & (n_idx < n_cmp)
    o_c = []
    p_sum = jnp.zeros((tq, ncp), F32)
    for h in range(N_HEADS):
        p = _row_masked_softmax(s_c[h * tq:(h + 1) * tq], valid_c)
        p_sum = p_sum + p
        o_c.append(_dot(p.astype(BF16), cmpv))

    nbp = -(-n_blk // LANES) * LANES
    nn = lax.broadcasted_iota(jnp.int32, (ncp, nbp), 0)
    jb = lax.broadcasted_iota(jnp.int32, (ncp, nbp), 1)
    cover = ((nn * NSA_CMP_STRIDE < (jb + 1) * NSA_SEL_BLOCK)
             & (nn * NSA_CMP_STRIDE + (NSA_CMP_LEN - 1) >= jb * NSA_SEL_BLOCK) & (nn < n_cmp)).astype(BF16)
    hi, lo = _split_bf16(p_sum)
    imp = _dot(hi, cover) + _dot(lo, cover)
    j = _lane_iota(nbp)
    cur = pos // NSA_SEL_BLOCK
    forced = (j == 0) | (j == cur) | (j == cur - 1)
    imp = jnp.where(forced, jnp.inf, jnp.where(j <= cur, imp, NEG_INF))
    sel = jnp.broadcast_to(_top_mask_row(imp[0:1], n_sel), (tq, nbp)).astype(BF16)

    _row_softmax_init(m_ref, l_ref, acc_ref)
    n_chunks = (p0 + tq - 1) // kc + 1
    eb = lax.broadcasted_iota(jnp.int32, (nbp, kc), 0)
    ek = lax.broadcasted_iota(jnp.int32, (nbp, kc), 1)

    def attend(c, carry):
        rows = load_sel(c)
        s = _dot_nt(qs, rows)
        expand = (eb == (c * kc + ek) // NSA_SEL_BLOCK).astype(BF16)
        picked = _dot(sel, expand)
        valid = (picked > 0.5) & (c * kc + _lane_iota(kc) <= pos)
        for h in range(N_HEADS):
            _row_softmax_step(pl.ds(h * tq, tq), s[h * tq:(h + 1) * tq], valid, rows, m_ref, l_ref, acc_ref)
        return carry

    lax.fori_loop(0, n_chunks, attend, 0)

    s_w = _dot_nt(qs, wrows)
    valid_w = (wpos <= pos) & (pos - wpos < NSA_WINDOW)
    heads = []
    for h in range(N_HEADS):
        p_w = _row_masked_softmax(s_w[h * tq:(h + 1) * tq], valid_w)
        o_w = _dot(p_w.astype(BF16), wrows)
        o_s = _row_softmax_result(pl.ds(h * tq, tq), l_ref, acc_ref)
        heads.append(g[:, 3 * h:3 * h + 1] * o_c[h] + g[:, 3 * h + 1:3 * h + 2] * o_s
                     + g[:, 3 * h + 2:3 * h + 3] * o_w)
    return _unstack_heads(heads)


def _nsa_decode_kernel(pt_ref, q_ref, g_ref, *rest, tq, kc, n_cmp, n_blk, n_sel, n_pages, page, pps):
    page_refs = rest[:pps]
    (new_ref, win_ref, wlo_ref, whi_ref, pe_ref, w2_ref, o_ref, cbuf_ref, sbuf_ref, m_ref, l_ref, acc_ref) = rest[pps:]
    s = pl.program_id(1)
    n_steps = n_pages // pps

    @pl.when(s < n_steps)
    def _():
        for r in range(pps):
            at = pl.ds(pl.multiple_of((s * pps + r) * page, page), page)
            cbuf_ref[at, :] = page_refs[r][0, 0, :, 0:LANES].astype(F32)
            sbuf_ref[at, :] = page_refs[r][0, 0, :, LANES:2 * LANES].astype(BF16)

    @pl.when(s == n_steps)
    def _():
        past = n_pages * page
        sbuf_ref[pl.ds(past, kc), :] = new_ref[0, :, LANES:2 * LANES].astype(BF16)
        n_blocks = past // NSA_CMP_STRIDE
        cmpv = _compress(lambda r: cbuf_ref[pl.ds(r, n_blocks, stride=NSA_CMP_STRIDE), :],
                         wlo_ref, whi_ref, pe_ref, w2_ref, n_blocks).astype(BF16)
        wrows = win_ref[0].astype(BF16)
        o_ref[0] = _nsa_rows_core(q_ref[0], g_ref[0], cmpv, _chunk_loader(sbuf_ref, kc), wrows,
                                  past - NSA_WINDOW + _lane_iota(wrows.shape[0]), m_ref, l_ref, acc_ref,
                                  tq=tq, kc=kc, p0=past, n_cmp=n_cmp, n_blk=n_blk, n_sel=n_sel)


def _nsa_sizes(n_keys):
    n_cmp = (n_keys - NSA_CMP_LEN) // NSA_CMP_STRIDE + 1
    n_blk = -(-n_keys // NSA_SEL_BLOCK)
    return n_cmp, n_blk, min(NSA_SEL_TOPN, n_blk), -(-n_blk // LANES) * LANES


def _nsa_scratch(n_kc, nbp, tq, kc):
    return ([pltpu.VMEM((n_kc, LANES, kc), BF16), pltpu.VMEM((nbp, tq), F32)]
            + _attend_scratch(LANES, N_HEADS * tq, kc))


def _nsa_decode_call(zq, cache, layer, page_table, new_rows, win_rows, cmp_w, *, kc=DEC_KEY_CHUNK,
                     pps=PAGES_PER_STEP):
    b, tq, _ = zq.shape
    n_pages = page_table.shape[1]
    page = cache.shape[2]
    past = n_pages * page
    assert past % kc == 0 and n_pages % pps == 0
    n_cmp, n_blk, n_sel, _ = _nsa_sizes(past + 1)
    kern = functools.partial(_nsa_decode_kernel, tq=tq, kc=kc, n_cmp=n_cmp, n_blk=n_blk, n_sel=n_sel,
                             n_pages=n_pages, page=page, pps=pps)
    grid_spec = pltpu.PrefetchScalarGridSpec(
        num_scalar_prefetch=1,
        grid=(b, n_pages // pps + 1),
        in_specs=[pl.BlockSpec((1, tq, 2 * LANES), lambda bi, s, pt: (bi, 0, _NEW["q_b"] // 256)),
                  pl.BlockSpec((1, tq, LANES), lambda bi, s, pt: (bi, 0, BLK_GB))]
        + _page_specs(cache, layer, n_pages, pps)
        + [pl.BlockSpec((1,) + new_rows.shape[1:], lambda bi, s, pt: (bi, 0, 0)),
           pl.BlockSpec((1,) + win_rows.shape[1:], lambda bi, s, pt: (bi, 0, 0))]
        + _compress_specs(lambda bi, s, pt: (0, 0, 0), lambda bi, s, pt: (0, 0)),
        out_specs=pl.BlockSpec((1, tq, 2 * LANES), lambda bi, s, pt: (bi, 0, 0)),
        scratch_shapes=[pltpu.VMEM((past, LANES), F32), pltpu.VMEM((past + kc, LANES), BF16)]
        + _row_softmax_scratch(N_HEADS * tq, LANES))
    return pl.pallas_call(
        kern, grid_spec=grid_spec,
        out_shape=jax.ShapeDtypeStruct((b, tq, 2 * LANES), F32),
        compiler_params=_params(("parallel", "arbitrary")),
        name="nsa_decode",
    )(page_table, zq, zq, *([cache] * pps), new_rows, win_rows, *cmp_w)


def _nsa_call(zq, cmp, sel, win, *, sel_blk, win_blk, tq, n_keys, kc=KEY_CHUNK):
    b, t_q, _ = zq.shape
    l_rows = sel.shape[1]
    n_cmp, n_blk, n_sel, nbp = _nsa_sizes(n_keys)
    kern = functools.partial(_nsa_kernel, tq=tq, kc=kc, n_cmp=n_cmp, n_blk=n_blk, n_sel=n_sel,
                             win_rows=min(NSA_WINDOW + tq, l_rows))
    return pl.pallas_call(
        kern,
        grid=(b, t_q // tq),
        in_specs=[pl.BlockSpec((1, tq, 2 * LANES), lambda bi, i: (bi, i, _NEW["q_b"] // 256)),
                  pl.BlockSpec((1, tq, LANES), lambda bi, i: (bi, i, BLK_GB)),
                  pl.BlockSpec((1, cmp.shape[1], LANES), lambda bi, i: (bi, 0, 0)),
                  pl.BlockSpec((1, l_rows, LANES), lambda bi, i: (bi, 0, sel_blk)),
                  pl.BlockSpec((1, l_rows, LANES), lambda bi, i: (bi, 0, win_blk))],
        out_specs=pl.BlockSpec((1, tq, 2 * LANES), lambda bi, i: (bi, i, 0)),
        out_shape=jax.ShapeDtypeStruct((b, t_q, 2 * LANES), F32),
        scratch_shapes=_nsa_scratch(l_rows // kc, nbp, tq, kc),
        compiler_params=_params(("parallel", "arbitrary")),
        name="nsa_attention",
    )(zq, zq, cmp, sel, win)


def _moba_kernel(q_ref, k_ref, v_ref, o_ref, kmean_ref, vt_ref, rank_ref, sel_ref, s_ref, p_ref, m_ref, l_ref, acc_ref,
                 *, tq, n_blocks, n_sel):
    c0 = pl.program_id(2)

    @pl.when(c0 == 0)
    def _():
        kmean_ref[...] = jnp.zeros(kmean_ref.shape, F32)

        def mean_block(blk, carry):
            rows = k_ref[0, pl.ds(pl.multiple_of(blk * MOBA_BLOCK, MOBA_BLOCK), MOBA_BLOCK), :]
            kmean_ref[pl.ds(blk, 1), :] = jnp.sum(rows, axis=0, keepdims=True) * (1.0 / MOBA_BLOCK)
            return carry

        lax.fori_loop(0, n_blocks, mean_block, 0)
        _fill_transposed(v_ref, vt_ref, MOBA_BLOCK)

    nbp = kmean_ref.shape[0]
    qst = _pair_heads(q_ref[0])
    qst_t = qst.T
    q_hi, q_lo = _split_bf16(qst_t)
    k_hi, k_lo = _split_bf16(kmean_ref[...])
    gate = _dot(k_hi, q_hi) + _dot(k_hi, q_lo) + _dot(k_lo, q_hi)
    gate = jnp.where(_row_iota(nbp) < c0, gate, NEG_INF)
    sel_ref[...] = _select_top_t(gate, rank_ref, c0, n_sel)
    qs_t = (qst_t * SCALE).astype(BF16)
    load_k = _chunk_loader(k_ref, MOBA_BLOCK)
    q_in_block = _lane_iota(2 * tq) % tq

    def admissible(blk):
        picked = sel_ref[pl.ds(blk, 1), :] > 0.5

        def strip(r0, rows):
            return picked | ((blk == c0) & (r0 + _row_iota(rows) <= q_in_block))
        return strip

    _attend_loop(c0 + 1, lambda blk, r0, n: _dot(load_k(blk, r0, n), qs_t), admissible,
                 lambda blk: vt_ref[blk],
                 s_ref, p_ref, m_ref, l_ref, acc_ref, reps=1)
    o = _tsoftmax_result(l_ref, acc_ref).T
    o_ref[0] = jnp.where(_lane_iota() < HEAD_DIM, o[0:tq], o[tq:2 * tq])


def _moba_call(zq, kv, *, tq):
    b, t_q, _ = zq.shape
    l_rows = kv.shape[1]
    assert tq == MOBA_BLOCK
    n_blocks = l_rows // MOBA_BLOCK
    nbp = -(-n_blocks // LANES) * LANES
    kern = functools.partial(_moba_kernel, tq=tq, n_blocks=n_blocks, n_sel=min(MOBA_TOPK, n_blocks))
    q_blk0 = _NEW["q_c"] // LANES
    return pl.pallas_call(
        kern,
        grid=(b, 2, t_q // tq),
        in_specs=[pl.BlockSpec((1, tq, LANES), lambda bi, c, i: (bi, i, q_blk0 + c)),
                  pl.BlockSpec((1, l_rows, LANES), lambda bi, c, i: (bi, 0, BLK_KC + c)),
                  pl.BlockSpec((1, l_rows, LANES), lambda bi, c, i: (bi, 0, BLK_VC + c))],
        out_specs=pl.BlockSpec((1, tq, LANES), lambda bi, c, i: (bi, i, c)),
        out_shape=jax.ShapeDtypeStruct((b, t_q, 2 * LANES), F32),
        scratch_shapes=[pltpu.VMEM((nbp, LANES), F32), pltpu.VMEM((n_blocks, LANES, MOBA_BLOCK), BF16),
                        pltpu.VMEM((nbp, 2 * tq), F32), pltpu.VMEM((nbp, 2 * tq), F32)]
        + _attend_scratch(LANES, 2 * tq, MOBA_BLOCK),
        compiler_params=_params(("parallel", "parallel", "arbitrary")),
        name="moba_attention",
    )(zq, kv, kv)


def _moba_pick_kernel(pt_ref, q_ref, *rest, n_blocks, n_sel, n_pages, page, pps):
    page_refs, (o_ref, kmean_ref, rank_ref) = rest[:pps], rest[pps:]
    s = pl.program_id(1)
    per_block = MOBA_BLOCK // page
    n_steps = n_pages // pps
    nbp = kmean_ref.shape[0]

    @pl.when(s == 0)
    def _():
        kmean_ref[...] = jnp.zeros(kmean_ref.shape, F32)

    @pl.when(s < n_steps)
    def _():
        for j in range(pps // per_block):
            tot = None
            for r in range(per_block):
                part = jnp.sum(page_refs[j * per_block + r][0, 0].astype(F32), axis=0, keepdims=True)
                tot = part if tot is None else tot + part
            kmean_ref[pl.ds(s * (pps // per_block) + j, 1), :] = tot * (1.0 / MOBA_BLOCK)

    @pl.when(s == n_steps)
    def _():
        prod = kmean_ref[...] * q_ref[0, 0:1, :]
        seg = (lax.broadcasted_iota(jnp.int32, (N_HEADS * HEAD_DIM, LANES), 0) // HEAD_DIM
               == lax.broadcasted_iota(jnp.int32, (N_HEADS * HEAD_DIM, LANES), 1)).astype(BF16)
        hi, lo = _split_bf16(prod)
        lo2 = (prod - hi.astype(F32) - lo.astype(F32)).astype(BF16)
        gate = _dot(hi, seg) + _dot(lo, seg) + _dot(lo2, seg)
        j = _row_iota(nbp)
        gate = jnp.where(j < n_blocks, gate, NEG_INF)
        rank = _rank_rows(gate, rank_ref, n_blocks)
        rows = []
        for r in range(n_sel):
            hit = (rank == r) & (gate > NEG_INF)
            rows.append(jnp.sum(jnp.where(hit, j, 0), axis=0, keepdims=True))
        for r in range(n_sel):
            hit = (rank == r) & (gate > NEG_INF)
            rows.append(jnp.sum(jnp.where(hit, 1, 0), axis=0, keepdims=True))
        rows.append(jnp.zeros((SUBLANES - 2 * n_sel, LANES), jnp.int32))
        o_ref[0] = jnp.concatenate(rows, axis=0)


def _moba_pick_call(zq, cache, layer, page_table, *, pps=PAGES_PER_STEP):
    b, tq, _ = zq.shape
    n_pages = page_table.shape[1]
    page = cache.shape[2]
    n_blocks = n_pages * page // MOBA_BLOCK
    n_sel = min(MOBA_TOPK, n_blocks + 1)
    assert MOBA_BLOCK % page == 0 and n_pages % pps == 0 and pps % (MOBA_BLOCK // page) == 0
    assert 2 * n_sel <= SUBLANES
    nbp = -(-n_blocks // LANES) * LANES
    block = (1, 1, page, N_HEADS * HEAD_DIM)

    def spec(r):
        return pl.BlockSpec(block, lambda bi, s, pt: (layer, pt[bi, jnp.minimum(s * pps + r, n_pages - 1)], 0, 0))

    grid_spec = pltpu.PrefetchScalarGridSpec(
        num_scalar_prefetch=1,
        grid=(b, n_pages // pps + 1),
        in_specs=[pl.BlockSpec((1, tq, 2 * LANES), lambda bi, s, pt: (bi, 0, _NEW["q_c"] // 256))]
        + [spec(r) for r in range(pps)],
        out_specs=pl.BlockSpec((1, SUBLANES, LANES), lambda bi, s, pt: (bi, 0, 0)),
        scratch_shapes=[pltpu.VMEM((nbp, N_HEADS * HEAD_DIM), F32), pltpu.VMEM((nbp, LANES), F32)])
    kern = functools.partial(_moba_pick_kernel, n_blocks=n_blocks, n_sel=n_sel, n_pages=n_pages, page=page, pps=pps)
    return pl.pallas_call(
        kern, grid_spec=grid_spec,
        out_shape=jax.ShapeDtypeStruct((b, SUBLANES, LANES), jnp.int32),
        compiler_params=_params(("parallel", "arbitrary")),
        name="moba_pick",
    )(page_table, zq, *([cache] * pps)), n_sel


def _moba_gather_kernel(pid_ref, ok_ref, q_ref, *rest, tq, n_sel, per_block, page):
    n_pg = N_HEADS * per_block
    k_refs, v_refs = rest[:n_pg], rest[n_pg:2 * n_pg]
    new_ref, o_ref, m_ref, l_ref, acc_ref = rest[2 * n_pg:]
    bi, r = pl.program_id(0), pl.program_id(1)

    @pl.when(r == 0)
    def _():
        m_ref[...] = jnp.full(m_ref.shape, NEG_INF, F32)
        l_ref[...] = jnp.zeros(l_ref.shape, F32)
        acc_ref[...] = jnp.zeros(acc_ref.shape, F32)

    qs = (jnp.concatenate([_pair_heads(q_ref[0, :, c * LANES:(c + 1) * LANES]) for c in range(2)], axis=0)
          * SCALE).astype(BF16)
    for h in range(N_HEADS):
        rows = pl.ds(h * tq, tq)
        kb = jnp.concatenate([k_refs[h * per_block + j][0, 0] for j in range(per_block)], axis=0).astype(BF16)
        vb = jnp.concatenate([v_refs[h * per_block + j][0, 0] for j in range(per_block)], axis=0).astype(BF16)
        ok = ok_ref[bi, r * N_HEADS + h] > 0
        _row_softmax_step(rows, _dot_nt(qs[h * tq:(h + 1) * tq], kb), ok, vb, m_ref, l_ref, acc_ref)

    @pl.when(r == n_sel - 1)
    def _():
        ki = lax.broadcasted_iota(jnp.int32, (tq, page), 1)
        qi = lax.broadcasted_iota(jnp.int32, (tq, page), 0)
        n_k = new_ref.shape[2] // 2
        outs = []
        for h in range(N_HEADS):
            rows = pl.ds(h * tq, tq)
            c = h // 2
            kb = new_ref[0, :, c * LANES:(c + 1) * LANES].astype(BF16)
            vb = new_ref[0, :, n_k + c * LANES:n_k + (c + 1) * LANES].astype(BF16)
            _row_softmax_step(rows, _dot_nt(qs[h * tq:(h + 1) * tq], kb), ki <= qi, vb, m_ref, l_ref, acc_ref)
            outs.append(acc_ref[rows, :] / jnp.maximum(l_ref[rows, :], 1e-30))
        lane = _lane_iota()
        o_ref[0] = jnp.concatenate([jnp.where(lane < HEAD_DIM, outs[0], outs[1]),
                                    jnp.where(lane < HEAD_DIM, outs[2], outs[3])], axis=1)


def _moba_decode_call(zq, cache, layer, page_table, new_page):
    b, tq, _ = zq.shape
    page = cache.shape[2]
    per_block = MOBA_BLOCK // page
    picks, n_sel = _moba_pick_call(zq, cache, layer, page_table)
    blk = picks[:, :n_sel, :N_HEADS]
    ok = picks[:, n_sel:2 * n_sel, :N_HEADS].reshape(b, n_sel * N_HEADS)
    logical = blk[..., None] * per_block + jnp.arange(per_block, dtype=jnp.int32)
    pid = jnp.take_along_axis(page_table, logical.reshape(b, -1), axis=1)
    block = (1, 1, page, LANES)

    def spec(h, j, field):
        return pl.BlockSpec(block, lambda bi, r, pid_ref, ok_ref:
                            (layer, pid_ref[bi, (r * N_HEADS + h) * per_block + j], 0, 2 * field + h // 2))

    hj = [(h, j) for h in range(N_HEADS) for j in range(per_block)]
    grid_spec = pltpu.PrefetchScalarGridSpec(
        num_scalar_prefetch=2,
        grid=(b, n_sel),
        in_specs=[pl.BlockSpec((1, tq, 2 * LANES), lambda bi, r, pid_ref, ok_ref: (bi, 0, _NEW["q_c"] // 256))]
        + [spec(h, j, 0) for h, j in hj] + [spec(h, j, 1) for h, j in hj]
        + [pl.BlockSpec((1,) + new_page.shape[1:], lambda bi, r, pid_ref, ok_ref: (bi, 0, 0))],
        out_specs=pl.BlockSpec((1, tq, 2 * LANES), lambda bi, r, pid_ref, ok_ref: (bi, 0, 0)),
        scratch_shapes=_row_softmax_scratch(N_HEADS * tq, LANES))
    kern = functools.partial(_moba_gather_kernel, tq=tq, n_sel=n_sel, per_block=per_block, page=page)
    return pl.pallas_call(
        kern, grid_spec=grid_spec,
        out_shape=jax.ShapeDtypeStruct((b, tq, 2 * LANES), F32),
        compiler_params=_params(("parallel", "arbitrary")),
        name="moba_decode",
    )(pid, ok, zq, *([cache] * (2 * len(hj))), new_page)


def _memory_kv(mem, g_ln, w_kv, g_k):
    b, m_rows, d = mem.shape
    n = w_kv.shape[1]
    half = n // 2
    mask = np.zeros((n,), np.float32)
    mask[:half] = 1
    zero = jnp.zeros((n,), F32)
    gain = jnp.concatenate([jnp.tile(g_k.astype(F32), half // HEAD_DIM), jnp.ones((half,), F32)])
    cfg = jnp.stack([jnp.asarray(mask), gain, zero, zero, zero, zero, zero, zero])
    masks = dict(norm=mask, rope=np.zeros_like(mask), sig=np.zeros_like(mask))
    rope = jnp.zeros((m_rows, 3 * LANES), F32)
    out = _project(mem.reshape(b * m_rows, d), g_ln.reshape(1, d), w_kv.astype(BF16), cfg, rope, masks,
                   tm=m_rows, tn=n)
    return out.reshape(b, m_rows, n)


def _mem_kernel(q_ref, kv_ref, o_ref, *, tq):
    lane = _lane_iota()
    n_kv = kv_ref.shape[2] // 2
    chunks = []
    for c in range(2):
        qst = _pair_heads(q_ref[0, :, c * LANES:(c + 1) * LANES])
        kb = kv_ref[0, :, c * LANES:(c + 1) * LANES].astype(BF16)
        vb = kv_ref[0, :, n_kv + c * LANES:n_kv + (c + 1) * LANES].astype(BF16)
        s = _dot_nt((qst * SCALE).astype(BF16), kb)
        e = jnp.exp(s - jnp.max(s, axis=-1, keepdims=True))
        p = e / jnp.sum(e, axis=-1, keepdims=True)
        o = _dot(p.astype(BF16), vb)
        chunks.append(jnp.where(lane < HEAD_DIM, o[0:tq], o[tq:2 * tq]))
    o_ref[0] = jnp.concatenate(chunks, axis=1)


def _mem_call(zq, mkv, *, tq):
    b, t_q, _ = zq.shape
    tq = min(tq, t_q)
    return pl.pallas_call(
        functools.partial(_mem_kernel, tq=tq),
        grid=(b, t_q // tq),
        in_specs=[pl.BlockSpec((1, tq, 2 * LANES), lambda bi, i: (bi, i, _NEW["q_m"] // 256)),
                  pl.BlockSpec((1,) + mkv.shape[1:], lambda bi, i: (bi, 0, 0))],
        out_specs=pl.BlockSpec((1, tq, 2 * LANES), lambda bi, i: (bi, i, 0)),
        out_shape=jax.ShapeDtypeStruct((b, t_q, 2 * LANES), F32),
        compiler_params=_params(("parallel", "parallel")),
        name="mem_attention",
    )(zq, mkv)


def _combine_kernel(x_ref, oa_ref, ob_ref, oc_ref, om_ref, gate_ref, wb_ref, wo_ref, y_ref):
    d = x_ref.shape[-1]
    h = None
    for bi, o_ref in enumerate((oa_ref, ob_ref, oc_ref, om_ref)):
        t = gate_ref[:, bi * d:(bi + 1) * d] * _dot(o_ref[...].astype(BF16), wb_ref[bi])
        h = t if h is None else h + t
    y_ref[...] = x_ref[...] + _dot(h.astype(BF16), wo_ref[...])


def _combine(x2d, outs, gate, w_branch, w_out, *, tm):
    m, d = x2d.shape
    tm = min(tm, m)
    bw = outs[0].shape[-1]
    o_spec = pl.BlockSpec((tm, bw), lambda i: (i, 0))
    return pl.pallas_call(
        _combine_kernel,
        grid=(m // tm,),
        in_specs=[pl.BlockSpec((tm, d), lambda i: (i, 0)), o_spec, o_spec, o_spec, o_spec,
                  pl.BlockSpec((tm, N_BRANCH * d), lambda i: (i, 0)),
                  pl.BlockSpec((N_BRANCH, bw, d), lambda i: (0, 0, 0)),
                  pl.BlockSpec((d, d), lambda i: (0, 0))],
        out_specs=pl.BlockSpec((tm, d), lambda i: (i, 0)),
        out_shape=jax.ShapeDtypeStruct((m, d), F32),
        compiler_params=_params(("parallel",)),
        name="branch_mix",
    )(x2d, *outs, gate, w_branch, w_out)


FF_CHUNK = 256
HALO = 16


def _rms(x, g):
    return x * lax.rsqrt(jnp.mean(x * x, axis=-1, keepdims=True) + RMS_EPS) * g


def _conv3(cw, u2, u1, u0):
    return cw[3:4] + cw[0:1] * u2 + cw[1:2] * u1 + cw[2:3] * u0


def _ffn_kernel(x_ref, xh_ref, ha_ref, hb_ref, g_ref, wa_ref, wb_ref, cwa_ref, cwb_ref, wdn_ref,
                y_ref, sta_ref, stb_ref, xn_ref, xhn_ref, acc_ref, *, tm):
    i = pl.program_id(1)
    j = pl.program_id(2)

    @pl.when(j == 0)
    def _():
        xn_ref[...] = _rms(x_ref[0], g_ref[...]).astype(BF16)
        xhn_ref[...] = _rms(xh_ref[0], g_ref[...]).astype(BF16)
        acc_ref[...] = jnp.zeros(acc_ref.shape, F32)

    def half(w_ref, hist_ref, cw_ref, st_ref):
        u = _dot(xn_ref[...], w_ref[...])
        u_prev = _dot(xhn_ref[...], w_ref[...])[HALO - SUBLANES:HALO]
        prev = jnp.where(i == 0, hist_ref[0], u_prev)
        ext = jnp.concatenate([prev, u], axis=0)
        st_ref[0, 0] = ext[tm:tm + SUBLANES]
        return _conv3(cw_ref[...], pltpu.roll(ext, 2, 0)[SUBLANES:], pltpu.roll(ext, 1, 0)[SUBLANES:], u)

    a = half(wa_ref, ha_ref, cwa_ref, sta_ref)
    b = half(wb_ref, hb_ref, cwb_ref, stb_ref)
    acc_ref[...] += _dot((a * jax.nn.sigmoid(a) * b).astype(BF16), wdn_ref[...])

    @pl.when(j == pl.num_programs(2) - 1)
    def _():
        y_ref[0] = x_ref[0] + acc_ref[...]


def _conv_table(conv_w, conv_b):
    return jnp.concatenate([conv_w, conv_b[None, :], jnp.zeros((SUBLANES - CONV_WIDTH - 1, conv_b.shape[0]), F32)])


def _conv_ffn(x, hist, g, w_up, conv_w, conv_b, w_down, *, tm):
    b, t, d = x.shape
    d_ff = w_down.shape[0]
    n_j = d_ff // FF_CHUNK
    tm = min(tm, t)
    cw = _conv_table(conv_w, conv_b)
    hist8 = jnp.concatenate([jnp.zeros((b, SUBLANES - 2, 2 * d_ff), F32), hist], axis=1)
    a_col = lambda bi, i, j: (0, j)
    b_col = lambda bi, i, j: (0, n_j + j)
    st_spec = pl.BlockSpec((1, 1, SUBLANES, FF_CHUNK), lambda bi, i, j: (bi, i, 0, j))
    st_shape = jax.ShapeDtypeStruct((b, t // tm, SUBLANES, d_ff), F32)
    y, st_a, st_b = pl.pallas_call(
        functools.partial(_ffn_kernel, tm=tm),
        grid=(b, t // tm, n_j),
        in_specs=[pl.BlockSpec((1, tm, d), lambda bi, i, j: (bi, i, 0)),
                  pl.BlockSpec((1, HALO, d), lambda bi, i, j: (bi, jnp.maximum(i * (tm // HALO) - 1, 0), 0)),
                  pl.BlockSpec((1, SUBLANES, FF_CHUNK), lambda bi, i, j: (bi, 0, j)),
                  pl.BlockSpec((1, SUBLANES, FF_CHUNK), lambda bi, i, j: (bi, 0, n_j + j)),
                  pl.BlockSpec((1, d), lambda bi, i, j: (0, 0)),
                  pl.BlockSpec((d, FF_CHUNK), a_col), pl.BlockSpec((d, FF_CHUNK), b_col),
                  pl.BlockSpec((SUBLANES, FF_CHUNK), a_col), pl.BlockSpec((SUBLANES, FF_CHUNK), b_col),
                  pl.BlockSpec((FF_CHUNK, d), lambda bi, i, j: (j, 0))],
        out_specs=[pl.BlockSpec((1, tm, d), lambda bi, i, j: (bi, i, 0)), st_spec, st_spec],
        out_shape=[jax.ShapeDtypeStruct((b, t, d), F32), st_shape, st_shape],
        scratch_shapes=[pltpu.VMEM((tm, d), BF16), pltpu.VMEM((HALO, d), BF16), pltpu.VMEM((tm, d), F32)],
        compiler_params=_params(("parallel", "arbitrary", "arbitrary")),
        name="conv_ffn",
    )(x, x, hist8, hist8, g.reshape(1, d), w_up, w_up, cw, cw, w_down)
    return y, jnp.concatenate([st_a[:, -1, SUBLANES - 2:], st_b[:, -1, SUBLANES - 2:]], axis=-1)


def _ffn_row_kernel(x_ref, h0a_ref, h0b_ref, h1a_ref, h1b_ref, g_ref, wa_ref, wb_ref, cwa_ref, cwb_ref, wdn_ref,
                    y_ref, ua_ref, ub_ref, xn_ref, acc_ref):
    j = pl.program_id(0)

    @pl.when(j == 0)
    def _():
        xn_ref[...] = _rms(x_ref[...], g_ref[...]).astype(BF16)
        acc_ref[...] = jnp.zeros(acc_ref.shape, F32)

    ua = _dot(xn_ref[...], wa_ref[...])
    ub = _dot(xn_ref[...], wb_ref[...])
    ua_ref[...] = ua
    ub_ref[...] = ub
    a = _conv3(cwa_ref[...], h0a_ref[...], h1a_ref[...], ua)
    b = _conv3(cwb_ref[...], h0b_ref[...], h1b_ref[...], ub)
    acc_ref[...] += _dot((a * jax.nn.sigmoid(a) * b).astype(BF16), wdn_ref[...])

    @pl.when(j == pl.num_programs(0) - 1)
    def _():
        y_ref[...] = x_ref[...] + acc_ref[...]


def _conv_ffn_rows(x2d, hist, g, w_up, conv_w, conv_b, w_down):
    b, d = x2d.shape
    d_ff = w_down.shape[0]
    n_j = d_ff // FF_CHUNK
    cw = _conv_table(conv_w, conv_b)
    h0, h1 = hist[:, 0], hist[:, 1]
    a_col = lambda j: (0, j)
    b_col = lambda j: (0, n_j + j)
    row_a, row_b = pl.BlockSpec((b, FF_CHUNK), a_col), pl.BlockSpec((b, FF_CHUNK), b_col)
    y, ua, ub = pl.pallas_call(
        _ffn_row_kernel,
        grid=(n_j,),
        in_specs=[pl.BlockSpec((b, d), lambda j: (0, 0)), row_a, row_b, row_a, row_b,
                  pl.BlockSpec((1, d), lambda j: (0, 0)),
                  pl.BlockSpec((d, FF_CHUNK), a_col), pl.BlockSpec((d, FF_CHUNK), b_col),
                  pl.BlockSpec((SUBLANES, FF_CHUNK), a_col), pl.BlockSpec((SUBLANES, FF_CHUNK), b_col),
                  pl.BlockSpec((FF_CHUNK, d), lambda j: (j, 0))],
        out_specs=[pl.BlockSpec((b, d), lambda j: (0, 0)), row_a, row_a],
        out_shape=[jax.ShapeDtypeStruct((b, d), F32), jax.ShapeDtypeStruct((b, d_ff), F32),
                   jax.ShapeDtypeStruct((b, d_ff), F32)],
        scratch_shapes=[pltpu.VMEM((b, d), BF16), pltpu.VMEM((b, d), F32)],
        compiler_params=_params(("arbitrary",)),
        name="conv_ffn_rows",
    )(x2d, h0, h0, h1, h1, g.reshape(1, d), w_up, w_up, cw, cw, w_down)
    return y, jnp.stack([h1, jnp.concatenate([ua, ub], axis=-1)], axis=1)


def _cols(zh, name, width):
    return zh[..., _NEW[name]:_NEW[name] + width]


def _new_rows(zh):
    b, t, _ = zh.shape
    dsa = jnp.concatenate([_cols(zh, "k_a", 2 * HEAD_DIM), _cols(zh, "ik", HEAD_DIM)], axis=-1)
    nsa = _cols(zh, "kc", 4 * HEAD_DIM)
    moba = _cols(zh, "k_c", 2 * N_HEADS * HEAD_DIM)
    win = _cols(zh, "kw", 2 * HEAD_DIM)
    return (dsa.reshape(b, t, 3, HEAD_DIM), nsa.reshape(b, t, 4, HEAD_DIM),
            moba.reshape(b, t, 2, N_HEADS, HEAD_DIM), win.reshape(b, t, 2, HEAD_DIM))


def _prompt_layer(x, mem, p):
    b, t, d = x.shape
    zh2d, gate = _projections(x.reshape(b * t, d), jnp.arange(t, dtype=jnp.int32), p["ln"][0], p["w_heads"],
                              p["cfg"], p["w_gate"], tm=256)
    zh = zh2d.reshape(b, t, N_HEADCOLS)
    o_a = _dsa_call(zh, zh, zh, kv_blk=BLK_KV_A, ik_blk=BLK_IK, tq=128, n_keys=t)
    cmp = _nsa_compress(zh, p["cmp_w"], blk=BLK_CMP)
    o_b = _nsa_call(zh, cmp, zh, zh, sel_blk=BLK_SEL, win_blk=BLK_WIN, tq=128, n_keys=t)
    o_c = _moba_call(zh, zh, tq=MOBA_BLOCK)
    mkv = _memory_kv(mem, p["ln"][2], p["w_mem_kv"], p["g_mem"][1])
    o_m = _mem_call(zh, mkv, tq=256)
    outs = [o.reshape(b * t, o.shape[-1]) for o in (o_a, o_b, o_c, o_m)]
    x1 = _combine(x.reshape(b * t, d), outs, gate, p["w_branch"], p["w_out"], tm=512).reshape(b, t, d)
    hist = jnp.zeros((b, CONV_WIDTH - 1, p["w_up"].shape[1]), F32)
    y, conv = _conv_ffn(x1, hist, p["ln"][1], p["w_up"], p["conv_w"], p["conv_b"], p["w_down"], tm=1024)
    dsa, nsa, moba, win = _new_rows(zh)
    keep = min(NSA_WINDOW, t)
    return y, dsa, nsa, moba, win[:, t - keep:], mkv.reshape(b, mkv.shape[1], 2, N_HEADS, HEAD_DIM), conv


def _first_row(x, n):
    return jnp.pad(x, ((0, 0), (0, n - 1)) + ((0, 0),) * (x.ndim - 2))


def _sample_layer(x, layer, caches, page_table, win_state, mem_kv, conv_hist, p):
    b, _, d = x.shape
    cache_dsa, cache_nsa, cache_moba = caches
    page = cache_dsa.shape[2]
    past = page_table.shape[1] * page
    x2d = x.reshape(b, d)
    zh, gate = _projections(x2d, jnp.full((b,), past, jnp.int32), p["ln"][0], p["w_heads"], p["cfg"],
                            p["w_gate"], tm=b)
    zq = _first_row(zh[:, None, :], DEC_ROWS)
    dsa, nsa, moba, win = _new_rows(zh[:, None, :])
    o_a = _dsa_decode_call(zq, cache_dsa, layer, page_table, _first_row(dsa.reshape(b, 1, -1), DEC_KEY_CHUNK))
    win_all = jnp.concatenate([win_state.reshape(b, -1, 2 * HEAD_DIM), win.reshape(b, 1, 2 * HEAD_DIM)], axis=1)
    w_pad = -(-win_all.shape[1] // LANES) * LANES
    win_rows = jnp.pad(win_all, ((0, 0), (0, w_pad - win_all.shape[1]), (0, 0)))
    o_b = _nsa_decode_call(zq, cache_nsa, layer, page_table, _first_row(nsa.reshape(b, 1, -1), DEC_KEY_CHUNK),
                           win_rows, p["cmp_w"])
    o_c = _moba_decode_call(zq, cache_moba, layer, page_table, _first_row(moba.reshape(b, 1, -1), page))
    o_m = _mem_call(zq, mem_kv.reshape(b, mem_kv.shape[1], -1), tq=DEC_ROWS)
    outs = [o[:, 0, :] for o in (o_a, o_b, o_c, o_m)]
    x1 = _combine(x2d, outs, gate, p["w_branch"], p["w_out"], tm=b)
    y, conv = _conv_ffn_rows(x1, conv_hist, p["ln"][1], p["w_up"], p["conv_w"], p["conv_b"], p["w_down"])
    keep = win_state.shape[1]
    win_new = win_all[:, win_all.shape[1] - keep:].reshape(b, keep, 2, HEAD_DIM)
    return y.reshape(b, 1, d), dsa, nsa, moba, win_new, conv


def kernel(x_prompt, x_sample, cache_dsa, cache_nsa, cache_moba, state_nsa_win, cache_mem, state_ffn_conv,
           page_table, mem_prompt, ln, w_in, g_dsa, g_nsa, g_moba, g_mem, w_mem_kv, w_cmp1, w_cmp2, pe_cmp,
           w_branch, w_out, w_up, conv_w, conv_b, w_down):
    depth = ln.shape[0]
    caches = tuple(c.astype(BF16).reshape(*c.shape[:3], -1) for c in (cache_dsa, cache_nsa, cache_moba))
    xp, xs = x_prompt, x_sample
    outs_p = [[] for _ in range(6)]
    outs_s = [[] for _ in range(5)]
    for l in range(depth):
        w_heads, cfg = _head_weights(w_in[l], g_dsa[l], g_nsa[l], g_moba[l], g_mem[l])
        p = dict(ln=ln[l], w_heads=w_heads, cfg=cfg, w_gate=w_in[l][:, GATE_ORIG:].astype(BF16), g_mem=g_mem[l],
                 w_mem_kv=w_mem_kv[l], cmp_w=_compress_weights(w_cmp1[l], w_cmp2[l], pe_cmp[l]),
                 w_branch=w_branch[l].astype(BF16), w_out=w_out[l].astype(BF16), w_up=w_up[l].astype(BF16),
                 conv_w=conv_w[l], conv_b=conv_b[l], w_down=w_down[l].astype(BF16))
        xp, *rest = _prompt_layer(xp, mem_prompt, p)
        for acc, r in zip(outs_p, rest):
            acc.append(r)
        xs, *rest = _sample_layer(xs, l, caches, page_table, state_nsa_win[l], cache_mem[l], state_ffn_conv[l], p)
        for acc, r in zip(outs_s, rest):
            acc.append(r)
    dsa_p, nsa_p, moba_p, win_p, memkv_p, conv_p = [jnp.stack(a) for a in outs_p]
    dsa_s, nsa_s, moba_s, win_s, conv_s = [jnp.stack(a) for a in outs_s]
    return (xp, xs, dsa_p, dsa_s, nsa_p, nsa_s, moba_p, moba_s, win_p, win_s, memkv_p, conv_p, conv_s)
```
